```python
import math
import jax, jax.numpy as jnp
from jax import lax
import numpy as np

D_MODEL = 1024
BATCH = 16
SEQ = 2048
DEPTH = 1

CHUNK = 64
Q_BLOCK = 128
A_QK_DIM = 64
A_V_DIM = 2 * A_QK_DIM
A_HEADS = D_MODEL // 256
A_WIDTH = A_HEADS * A_V_DIM
B_HEAD_DIM = 64
B_HEADS = D_MODEL // 128
B_WIDTH = B_HEADS * B_HEAD_DIM
B_KV_RANK = D_MODEL // 8
IDX_HEADS = 8
IDX_DIM = 64
TOPK_MAX = 256
N_ALIBI_HEADS = A_HEADS + B_HEADS
IN_SPLITS = (
    A_HEADS * 2 * A_QK_DIM,
    A_HEADS * 2 * A_QK_DIM,
    A_WIDTH,
    B_WIDTH,
    B_KV_RANK,
    IDX_HEADS * IDX_DIM,
    IDX_DIM,
    IDX_HEADS,
    2 * D_MODEL,
)
IN_COLS = sum(IN_SPLITS)
N_GROUPS = 4
EXPERTS_PER_GROUP = 4
N_EXPERTS = N_GROUPS * EXPERTS_PER_GROUP
TOP_K_IN_GROUP = 2
D_FF_EXPERT = D_MODEL // 4
LN_EPS = 1e-5
RMS_EPS = 1e-5
DEEPNORM_ALPHA = (2.0 * DEPTH) ** 0.25
DEEPNORM_BETA = (8.0 * DEPTH) ** -0.25

kernel_name = "hybrid_diffattn_dsa_hmoe_block"


def _layer_norm(x, g, b):
    xf = x.astype(jnp.float32)
    mu = jnp.mean(xf, axis=-1, keepdims=True)
    var = jnp.mean(jnp.square(xf - mu), axis=-1, keepdims=True)
    return ((xf - mu) * lax.rsqrt(var + LN_EPS) * g.astype(jnp.float32) + b.astype(jnp.float32)).astype(x.dtype)


def _rms_norm(x, g):
    xf = x.astype(jnp.float32)
    y = xf * lax.rsqrt(jnp.mean(xf * xf, axis=-1, keepdims=True) + RMS_EPS)
    return (y * g.astype(jnp.float32)).astype(x.dtype)


def _alibi_slopes():
    n = N_ALIBI_HEADS
    slopes = (2.0 ** (-8.0 * np.arange(1, n + 1) / n)).astype(np.float32)
    a_idx = np.arange(A_HEADS) * (n // A_HEADS)
    b_idx = np.setdiff1d(np.arange(n), a_idx)
    return jnp.asarray(slopes[a_idx]), jnp.asarray(slopes[b_idx])


def _to_blocks(t):
    b, s = t.shape[:2]
    t = t.reshape((b, s // Q_BLOCK, Q_BLOCK) + t.shape[2:])
    return jnp.moveaxis(t, 1, 0)


def _from_blocks(t):
    t = jnp.moveaxis(t, 0, 1)
    return t.reshape((t.shape[0], t.shape[1] * t.shape[2]) + t.shape[3:])


def _diff_attention(q, k, v, slopes, lam, lam_init, subln_g):
    b, s = q.shape[:2]
    n_blk = s // Q_BLOCK
    scale = A_QK_DIM ** -0.5
    k_pos = jnp.arange(s)

    def block(args):
        q_blk, blk = args
        q_pos = blk * Q_BLOCK + jnp.arange(Q_BLOCK)
        logits = jnp.einsum('bqhmd,bshmd->bhmqs', q_blk, k).astype(jnp.float32) * scale
        dist = jnp.abs(q_pos[:, None] - k_pos[None, :]).astype(jnp.float32)
        bias = -slopes[:, None, None] * dist
        visible = (k_pos // CHUNK)[None, :] <= (q_pos // CHUNK)[:, None]
        logits = jnp.where(visible, logits + bias[None, :, None], -jnp.inf)
        p = jax.nn.softmax(logits, axis=-1)
        attn = p[:, :, 0] - lam * p[:, :, 1]
        return jnp.einsum('bhqs,bshe->bqhe', attn.astype(v.dtype), v)

    o = _from_blocks(lax.map(block, (_to_blocks(q), jnp.arange(n_blk))))
    o = _rms_norm(o, subln_g) * (1.0 - lam_init)
    return o.reshape(b, s, A_WIDTH)


def _sparse_indexed_attention(q_abs, c_kv, idx_q, idx_k, idx_w, slopes):
    b, s = q_abs.shape[:2]
    n_blk = s // Q_BLOCK
    top_k = min(TOPK_MAX, s // 4)
    scale = B_HEAD_DIM ** -0.5
    k_pos = jnp.arange(s)

    def block(args):
        qa, iq, iw, blk = args
        q_pos = blk * Q_BLOCK + jnp.arange(Q_BLOCK)
        idx_logits = jnp.einsum('bqhd,bsd->bqhs', iq, idx_k).astype(jnp.float32)
        score = jnp.einsum('bqh,bqhs->bqs', iw.astype(jnp.float32), jax.nn.relu(idx_logits))
        visible = (k_pos // CHUNK)[None, :] <= (q_pos // CHUNK)[:, None]
        score = jnp.where(visible[None], score, -jnp.inf)
        _, sel = lax.top_k(score, top_k)
        kv_sel = jax.vmap(lambda kv, ix: kv[ix])(c_kv, sel)
        logits = jnp.einsum('bqhr,bqkr->bhqk', qa, kv_sel).astype(jnp.float32) * scale
        dist = jnp.abs(q_pos[None, :, None] - sel).astype(jnp.float32)
        sel_ok = (sel // CHUNK) <= (q_pos // CHUNK)[None, :, None]
        logits = jnp.where(sel_ok[:, None], logits - slopes[None, :, None, None] * dist[:, None], -jnp.inf)
        p = jax.nn.softmax(logits, axis=-1)
        return jnp.einsum('bhqk,bqkr->bqhr', p.astype(kv_sel.dtype), kv_sel)

    args = (_to_blocks(q_abs), _to_blocks(idx_q), _to_blocks(idx_w), jnp.arange(n_blk))
    return _from_blocks(lax.map(block, args))


def _hier_moe(u, w_group, b_group, w_expert_router, b_expert_router, w_gate, w_up, w_down):
    g_logits = (jnp.einsum('bsd,dg->bsg', u, w_group) + b_group).astype(jnp.float32)
    g_prob = jax.nn.softmax(g_logits, axis=-1)
    g_top_p, g_top_i = lax.top_k(g_prob, 1)
    e_logits = (jnp.einsum('bsd,de->bse', u, w_expert_router) + b_expert_router).astype(jnp.float32)
    e_logits = e_logits.reshape(u.shape[0], u.shape[1], N_GROUPS, EXPERTS_PER_GROUP)
    e_logits = jnp.einsum('bsge,bsg->bse', e_logits, jax.nn.one_hot(g_top_i[..., 0], N_GROUPS, dtype=jnp.float32))
    e_prob = jax.nn.softmax(e_logits, axis=-1)
    e_top_p, e_top_i = lax.top_k(e_prob, TOP_K_IN_GROUP)
    e_top_p = e_top_p / jnp.sum(e_top_p, axis=-1, keepdims=True)
    expert_id = g_top_i * EXPERTS_PER_GROUP + e_top_i
    weight = g_top_p * e_top_p
    comb = jnp.einsum('bsk,bske->bse', weight, jax.nn.one_hot(expert_id, N_EXPERTS, dtype=jnp.float32))
    comb = comb.astype(u.dtype)
    y = jnp.zeros_like(u)
    for e in range(N_EXPERTS):
        h = jax.nn.silu(u @ w_gate[e]) * (u @ w_up[e])
        y = y + comb[..., e:e + 1] * (h @ w_down[e])
    return y


def setup_inputs(seed: int = 0) -> dict:
    key = jax.random.key(seed)
    ks = jax.random.split(key, 27)
    f32 = jnp.float32
    L, D = DEPTH, D_MODEL

    def nrm(k, shape, scale):
        return jax.random.normal(k, shape, f32) * scale

    return {
        "x": nrm(ks[0], (BATCH, SEQ, D), 1.0),
        "c": nrm(ks[1], (BATCH, D), 1.0),
        "ada_w": nrm(ks[2], (L, D, 6 * D), 0.5 * D ** -0.5),
        "ada_b": nrm(ks[3], (L, 6 * D), 0.02),
        "w_in": nrm(ks[4], (L, D, IN_COLS), D ** -0.5),
        "lambda_q1": nrm(ks[5], (L, A_QK_DIM), 0.1),
        "lambda_k1": nrm(ks[6], (L, A_QK_DIM), 0.1),
        "lambda_q2": nrm(ks[7], (L, A_QK_DIM), 0.1),
        "lambda_k2": nrm(ks[8], (L, A_QK_DIM), 0.1),
        "a_subln_g": 1.0 + nrm(ks[9], (L, A_V_DIM), 0.02),
        "kv_norm_g": 1.0 + nrm(ks[10], (L, B_KV_RANK), 0.02),
        "w_uk": nrm(ks[11], (L, B_KV_RANK, B_HEADS, B_HEAD_DIM), B_KV_RANK ** -0.5),
        "w_uv": nrm(ks[12], (L, B_KV_RANK, B_HEADS, B_HEAD_DIM), B_KV_RANK ** -0.5),
        "w_a_proj": nrm(ks[13], (L, A_WIDTH, D), A_WIDTH ** -0.5),
        "w_b_proj": nrm(ks[14], (L, B_WIDTH, D), B_WIDTH ** -0.5),
        "w_o": nrm(ks[15], (L, D, D), DEEPNORM_BETA * D ** -0.5),
        "ln1_g": 1.0 + nrm(ks[16], (L, D), 0.02),
        "ln1_b": nrm(ks[17], (L, D), 0.02),
        "w_group": nrm(ks[18], (L, D, N_GROUPS), D ** -0.5),
        "b_group": nrm(ks[19], (L, N_GROUPS), 0.01),
        "w_expert_router": nrm(ks[20], (L, D, N_EXPERTS), D ** -0.5),
        "b_expert_router": nrm(ks[21], (L, N_EXPERTS), 0.01),
        "w_exp_gate": nrm(ks[22], (L, N_EXPERTS, D, D_FF_EXPERT), D ** -0.5),
        "w_exp_up": nrm(ks[23], (L, N_EXPERTS, D, D_FF_EXPERT), D ** -0.5),
        "w_exp_down": nrm(ks[24], (L, N_EXPERTS, D_FF_EXPERT, D), DEEPNORM_BETA * D_FF_EXPERT ** -0.5),
        "ln2_g": 1.0 + nrm(ks[25], (L, D), 0.02),
        "ln2_b": nrm(ks[26], (L, D), 0.02),
    }


def reference(x, c, ada_w, ada_b, w_in, lambda_q1, lambda_k1, lambda_q2, lambda_k2, a_subln_g,
              kv_norm_g, w_uk, w_uv, w_a_proj, w_b_proj, w_o, ln1_g, ln1_b, w_group, b_group,
              w_expert_router, b_expert_router, w_exp_gate, w_exp_up, w_exp_down, ln2_g, ln2_b):
    b, s, d = x.shape
    slopes_a, slopes_b = _alibi_slopes()
    cond = jax.nn.silu(c)
    offsets = [int(o) for o in np.cumsum(IN_SPLITS)[:-1]]
    for l in range(DEPTH):
        mod = cond @ ada_w[l] + ada_b[l]
        shift1, scale1, gate1, shift2, scale2, gate2 = [m[:, None, :] for m in jnp.split(mod, 6, axis=-1)]

        u = x * (1.0 + scale1) + shift1
        proj = u @ w_in[l]
        a_q, a_k, a_v, b_q, b_kv, i_q, i_k, i_w, gates = jnp.split(proj, offsets, axis=-1)

        lam_init = 0.8 - 0.6 * math.exp(-0.3 * l)
        lam = (jnp.exp(jnp.sum(lambda_q1[l].astype(jnp.float32) * lambda_k1[l].astype(jnp.float32)))
               - jnp.exp(jnp.sum(lambda_q2[l].astype(jnp.float32) * lambda_k2[l].astype(jnp.float32)))
               + lam_init)
        y_a = _diff_attention(a_q.reshape(b, s, A_HEADS, 2, A_QK_DIM),
                              a_k.reshape(b, s, A_HEADS, 2, A_QK_DIM),
                              a_v.reshape(b, s, A_HEADS, A_V_DIM),
                              slopes_a, lam, lam_init, a_subln_g[l])

        c_kv = _rms_norm(b_kv, kv_norm_g[l])
        q_abs = jnp.einsum('bshe,rhe->bshr', b_q.reshape(b, s, B_HEADS, B_HEAD_DIM), w_uk[l])
        o_lat = _sparse_indexed_attention(q_abs, c_kv, i_q.reshape(b, s, IDX_HEADS, IDX_DIM),
                                          i_k, i_w, slopes_b)
        y_b = jnp.einsum('bshr,rhe->bshe', o_lat, w_uv[l]).reshape(b, s, B_WIDTH)

        g = jax.nn.sigmoid(gates).reshape(b, s, 2, d)
        mixed = g[:, :, 0] * (y_a @ w_a_proj[l]) + g[:, :, 1] * (y_b @ w_b_proj[l])
        x = _layer_norm(DEEPNORM_ALPHA * x + gate1 * (mixed @ w_o[l]), ln1_g[l], ln1_b[l])

        u2 = x * (1.0 + scale2) + shift2
        y = _hier_moe(u2, w_group[l], b_group[l], w_expert_router[l], b_expert_router[l],
                      w_exp_gate[l], w_exp_up[l], w_exp_down[l])
        x = _layer_norm(DEEPNORM_ALPHA * x + gate2 * y, ln2_g[l], ln2_b[l])
    return x
```

```python
import functools
import math

import numpy as np
import jax
import jax.numpy as jnp
from jax import lax
from jax.experimental import pallas as pl
from jax.experimental.pallas import tpu as pltpu

D_MODEL = 1024
SEQ = 2048
CHUNK = 64
A_QK_DIM = 64
A_V_DIM = 128
A_HEADS = 4
A_WIDTH = A_HEADS * A_V_DIM
B_HEAD_DIM = 64
B_HEADS = 8
B_WIDTH = B_HEADS * B_HEAD_DIM
B_KV_RANK = 128
IDX_HEADS = 8
IDX_DIM = 64
TOPK = 256
N_ALIBI_HEADS = A_HEADS + B_HEADS
N_GROUPS = 4
EXPERTS_PER_GROUP = 4
N_EXPERTS = 16
D_FF_EXPERT = 256
LN_EPS = 1e-5
RMS_EPS = 1e-5
DEPTH = 1
DEEPNORM_ALPHA = (2.0 * DEPTH) ** 0.25
LAM_INIT = 0.8 - 0.6 * math.exp(-0.3 * 0)

LANES = 128
VMEM_LIMIT_BYTES = 56 * 1024 * 1024

F32 = jnp.float32
BF16 = jnp.bfloat16
NEG_INF = float("-inf")
INT_MIN = -(2 ** 31)

_NT = (((1,), (1,)), ((), ()))


def _dot(a, b, **kw):
    return jnp.dot(a, b, preferred_element_type=F32, **kw)


def _dot_nt(a, b, **kw):
    return lax.dot_general(a, b, _NT, preferred_element_type=F32, **kw)


def _alibi_slopes():
    n = N_ALIBI_HEADS
    slopes = (2.0 ** (-8.0 * np.arange(1, n + 1) / n)).astype(np.float32)
    a_idx = np.arange(A_HEADS) * (n // A_HEADS)
    b_idx = np.setdiff1d(np.arange(n), a_idx)
    return slopes[a_idx], slopes[b_idx]


def _layer_norm(h, g, b):
    mu = jnp.mean(h, axis=-1, keepdims=True)
    d = h - mu
    var = jnp.mean(d * d, axis=-1, keepdims=True)
    return d * lax.rsqrt(var + LN_EPS) * g + b


def _params(sem):
    return pltpu.CompilerParams(dimension_semantics=sem, vmem_limit_bytes=VMEM_LIMIT_BYTES)


MOD_TN = 1536


def _mod_kernel(c_ref, w_ref, b_ref, o_ref):
    c = c_ref[...]
    cond = c * jax.nn.sigmoid(c)
    o_ref[...] = _dot(cond, w_ref[...], precision=lax.Precision.HIGHEST) + b_ref[...]


def _modulation(c, ada_w, ada_b):
    b, d = c.shape
    n = ada_w.shape[1]
    return pl.pallas_call(
        _mod_kernel,
        out_shape=jax.ShapeDtypeStruct((b, n), F32),
        grid=(n // MOD_TN,),
        in_specs=[
            pl.BlockSpec((b, d), lambda j: (0, 0)),
            pl.BlockSpec((d, MOD_TN), lambda j: (0, j)),
            pl.BlockSpec((1, MOD_TN), lambda j: (0, j)),
        ],
        out_specs=pl.BlockSpec((b, MOD_TN), lambda j: (0, j)),
        compiler_params=_params(("arbitrary",)),
        name="modulation",
    )(c, ada_w, ada_b.reshape(1, n))


PROJ_TM = 512
W_QKV = 3 * A_WIDTH
W_SMALL = 3 * LANES
PROJ_COLS = (W_QKV, B_WIDTH, IDX_HEADS * IDX_DIM, 2 * D_MODEL, W_SMALL)


def _proj_kernel(x_ref, mod_ref, w_ref, qkv_ref, bq_ref, iq_ref, gates_ref, small_ref):
    m = mod_ref[0]
    u = (x_ref[...] * (1.0 + m[1:2]) + m[0:1]).astype(BF16)
    off = 0
    for o_ref, n in zip((qkv_ref, bq_ref, iq_ref, gates_ref, small_ref), PROJ_COLS):
        o_ref[...] = _dot(u, w_ref[:, off:off + n]).astype(o_ref.dtype)
        off += n


def _projection(x2, mod3, w_cat):
    n_tok, d = x2.shape
    tiles_per_batch = SEQ // PROJ_TM
    out_dtypes = (BF16, BF16, BF16, BF16, F32)
    return pl.pallas_call(
        _proj_kernel,
        out_shape=[jax.ShapeDtypeStruct((n_tok, n), dt) for n, dt in zip(PROJ_COLS, out_dtypes)],
        grid=(n_tok // PROJ_TM,),
        in_specs=[
            pl.BlockSpec((PROJ_TM, d), lambda i: (i, 0)),
            pl.BlockSpec((1, 6, d), lambda i: (i // tiles_per_batch, 0, 0)),
            pl.BlockSpec(w_cat.shape, lambda i: (0, 0)),
        ],
        out_specs=[pl.BlockSpec((PROJ_TM, n), lambda i: (i, 0)) for n in PROJ_COLS],
        compiler_params=_params(("arbitrary",)),
        name="projection",
    )(x2, mod3, w_cat)


A_TQ = 256


def _diff_attn_kernel(q_ref, k_ref, v_ref, lam_ref, g_ref, slope_ref, o_ref, bias_ref):
    slope = slope_ref[0][:, 0:1]
    r = lax.broadcasted_iota(jnp.int32, (A_TQ, SEQ), 0)
    j = lax.broadcasted_iota(jnp.int32, (A_TQ, SEQ), 1)
    dist = jnp.abs(r + (SEQ - A_TQ) - j).astype(F32)
    visible = (j - (SEQ - A_TQ)) // CHUNK <= r // CHUNK
    bias_ref[...] = jnp.where(visible, -slope * dist, NEG_INF)

    lv = lam_ref[...]
    lam = (jnp.exp(jnp.sum(lv[0:1] * lv[1:2], axis=1, keepdims=True))
           - jnp.exp(jnp.sum(lv[2:3] * lv[3:4], axis=1, keepdims=True)) + LAM_INIT)
    lane = lax.broadcasted_iota(jnp.int32, (A_TQ, 2 * A_QK_DIM), 1)
    scale = A_QK_DIM ** -0.5
    for i in range(SEQ // A_TQ):
        q0 = i * A_TQ
        kv = q0 + A_TQ
        q = q_ref[0, q0:q0 + A_TQ, :] * scale
        k = k_ref[0, 0:kv, :]
        bias = bias_ref[:, SEQ - kv:SEQ]
        ps, inv_ls = [], []
        for mth in range(2):
            qm = jnp.where((lane // A_QK_DIM) == mth, q, jnp.zeros_like(q))
            s = _dot_nt(qm, k) + bias
            mx = jnp.max(s, axis=-1, keepdims=True)
            p = jnp.exp(s - mx)
            ps.append(p)
            inv_ls.append(1.0 / jnp.sum(p, axis=-1, keepdims=True))
        attn = ps[0] * inv_ls[0] - ps[1] * (lam * inv_ls[1])
        o = _dot(attn.astype(BF16), v_ref[0, 0:kv, :])
        y = o * lax.rsqrt(jnp.mean(o * o, axis=-1, keepdims=True) + RMS_EPS) * g_ref[...]
        o_ref[0, q0:q0 + A_TQ, :] = (y * (1.0 - LAM_INIT)).astype(o_ref.dtype)


def _diff_attention(qkv3, lam_vecs, subln_g, slopes):
    b, s, _ = qkv3.shape
    blk = (1, s, A_V_DIM)
    return pl.pallas_call(
        _diff_attn_kernel,
        out_shape=jax.ShapeDtypeStruct((b, s, A_WIDTH), BF16),
        grid=(b, A_HEADS),
        in_specs=[
            pl.BlockSpec(blk, lambda bi, h: (bi, 0, h)),
            pl.BlockSpec(blk, lambda bi, h: (bi, 0, A_HEADS + h)),
            pl.BlockSpec(blk, lambda bi, h: (bi, 0, 2 * A_HEADS + h)),
            pl.BlockSpec(lam_vecs.shape, lambda bi, h: (0, 0)),
            pl.BlockSpec((1, A_V_DIM), lambda bi, h: (0, 0)),
            pl.BlockSpec((1, 1, LANES), lambda bi, h: (h, 0, 0)),
        ],
        out_specs=pl.BlockSpec(blk, lambda bi, h: (bi, 0, h)),
        scratch_shapes=[pltpu.VMEM((A_TQ, SEQ), F32)],
        compiler_params=_params(("arbitrary", "arbitrary")),
        name="diff_attention",
    )(qkv3, qkv3, qkv3, lam_vecs, subln_g, slopes)


B_TQ = 128
HEAD_GROUP = 4
SMALL_KV = slice(0, 128)
SMALL_IK = slice(128, 256)
SMALL_IW = slice(256, 384)


def _row_count(mask_f32):
    return jnp.sum(mask_f32, axis=1, keepdims=True)


def _topk_mask(score_ref, key_ref, selb_ref, kv):
    s = score_ref[:, 0:kv]
    bits = pltpu.bitcast(s, jnp.int32)
    bits = jnp.where(bits == INT_MIN, 0, bits)
    key_ref[:, 0:kv] = bits ^ ((bits >> 31) & 0x7FFFFFFF)

    def bit_step(b, t_u):
        c_u = t_u | lax.shift_left(jnp.int32(1), 31 - b)
        cnt = _row_count(jnp.where(key_ref[:, 0:kv] >= (c_u ^ INT_MIN), 1.0, 0.0))
        return jnp.where(cnt >= TOPK, c_u, t_u)

    t_u = lax.fori_loop(0, 32, bit_step, jnp.zeros((B_TQ, 1), jnp.int32))
    t_key = t_u ^ INT_MIN
    t_val = jnp.min(jnp.where(key_ref[:, 0:kv] >= t_key, s, jnp.inf), axis=1, keepdims=True)
    gt = s > t_val
    eqf = jnp.where(s == t_val, 1.0, 0.0)
    need = TOPK - _row_count(jnp.where(gt, 1.0, 0.0))
    selb_ref[:, 0:kv] = jnp.where(s >= t_val, 0.0, NEG_INF)
    tie_overflow = jnp.max(jnp.abs(_row_count(eqf) - need))

    @pl.when(tie_overflow > 0.0)
    def _():
        idx = lax.broadcasted_iota(jnp.int32, (B_TQ, kv), 1)

        def idx_step(b, j):
            c = j | lax.shift_left(jnp.int32(1), 10 - b)
            before = _row_count(jnp.where(idx < c, eqf, 0.0))
            return jnp.where(before < need, c, j)

        j_max = lax.fori_loop(0, 11, idx_step, jnp.zeros((B_TQ, 1), jnp.int32))
        keep = gt | ((eqf > 0.0) & (idx <= j_max))
        selb_ref[:, 0:kv] = jnp.where(keep, 0.0, NEG_INF)


def _sparse_attn_kernel(bq_ref, iq_ref, small_ref, kvg_ref, wuk_ref, wuv_ref, slope_ref, o_ref,
                        bias_ref, ckv_ref, ik_ref, score_ref, key_ref, selb_ref):
    @pl.when(pl.program_id(0) == 0)
    def _():
        r = lax.broadcasted_iota(jnp.int32, (B_TQ, SEQ), 0)
        j = lax.broadcasted_iota(jnp.int32, (B_TQ, SEQ), 1)
        dist = jnp.abs(r + (SEQ - B_TQ) - j).astype(F32)
        for h in range(B_HEADS):
            bias_ref[h] = -slope_ref[h][:, 0:1] * dist

    kv_lat = small_ref[0, :, SMALL_KV]
    ckv = kv_lat * lax.rsqrt(jnp.mean(kv_lat * kv_lat, axis=-1, keepdims=True) + RMS_EPS) * kvg_ref[...]
    ckv_ref[...] = ckv.astype(BF16)
    ik_ref[...] = small_ref[0, :, SMALL_IK].astype(BF16)

    lane = lax.broadcasted_iota(jnp.int32, (B_TQ, LANES), 1)
    rr = lax.broadcasted_iota(jnp.int32, (B_TQ, B_TQ), 0)
    cc = lax.broadcasted_iota(jnp.int32, (B_TQ, B_TQ), 1)
    diag_visible = cc // CHUNK <= rr // CHUNK
    scale = B_HEAD_DIM ** -0.5
    n_groups = B_HEADS // HEAD_GROUP

    for i in range(SEQ // B_TQ):
        q0 = i * B_TQ
        kv = q0 + B_TQ
        rows = slice(q0, q0 + B_TQ)
        if kv <= TOPK:
            if q0:
                selb_ref[:, 0:q0] = jnp.zeros((B_TQ, q0), F32)
            selb_ref[:, q0:kv] = jnp.where(diag_visible, 0.0, NEG_INF)
        else:
            iq = iq_ref[0, rows, :]
            iw = small_ref[0, rows, SMALL_IW]
            ik = ik_ref[0:kv, :]
            score = jnp.zeros((B_TQ, kv), F32)
            for g in range(n_groups):
                parts = []
                for hh in range(HEAD_GROUP):
                    h = g * HEAD_GROUP + hh
                    pair = iq[:, (h // 2) * LANES:(h // 2 + 1) * LANES]
                    parts.append(jnp.where((lane // IDX_DIM) == (h % 2), pair, jnp.zeros_like(pair)))
                x = _dot_nt(jnp.concatenate(parts, axis=0), ik)
                for hh in range(HEAD_GROUP):
                    h = g * HEAD_GROUP + hh
                    score = score + iw[:, h:h + 1] * jnp.maximum(x[hh * B_TQ:(hh + 1) * B_TQ], 0.0)
            score_ref[:, 0:q0] = score[:, 0:q0]
            score_ref[:, q0:kv] = jnp.where(diag_visible, score[:, q0:kv], NEG_INF)
            _topk_mask(score_ref, key_ref, selb_ref, kv)

        bq = bq_ref[0, rows, :]
        ck = ckv_ref[0:kv, :]
        selb = selb_ref[:, 0:kv]
        lat = []
        for g in range(n_groups):
            parts = []
            for hh in range(HEAD_GROUP):
                h = g * HEAD_GROUP + hh
                pair = bq[:, (h // 2) * LANES:(h // 2 + 1) * LANES]
                parts.append((_dot(pair, wuk_ref[h]) * scale).astype(BF16))
            s = _dot_nt(jnp.concatenate(parts, axis=0), ck)
            probs = []
            for hh in range(HEAD_GROUP):
                h = g * HEAD_GROUP + hh
                sh = s[hh * B_TQ:(hh + 1) * B_TQ] + bias_ref[h, :, SEQ - kv:SEQ] + selb
                mx = jnp.max(sh, axis=-1, keepdims=True)
                p = jnp.exp(sh - mx)
                probs.append((p * (1.0 / jnp.sum(p, axis=-1, keepdims=True))).astype(BF16))
            o = _dot(jnp.concatenate(probs, axis=0), ck)
            lat.extend(o[hh * B_TQ:(hh + 1) * B_TQ] for hh in range(HEAD_GROUP))
        outs = []
        for jp in range(B_HEADS // 2):
            pair = jnp.concatenate([lat[2 * jp], lat[2 * jp + 1]], axis=1).astype(BF16)
            outs.append(_dot(pair, wuv_ref[jp]))
        o_ref[0, rows, :] = jnp.concatenate(outs, axis=1).astype(o_ref.dtype)


def _sparse_attention(bq3, iq3, small3, kv_norm_g, wuk_pad, wuv_pair, slopes):
    b, s, _ = bq3.shape
    return pl.pallas_call(
        _sparse_attn_kernel,
        out_shape=jax.ShapeDtypeStruct((b, s, B_WIDTH), BF16),
        grid=(b,),
        in_specs=[
            pl.BlockSpec((1, s, B_WIDTH), lambda bi: (bi, 0, 0)),
            pl.BlockSpec((1, s, IDX_HEADS * IDX_DIM), lambda bi: (bi, 0, 0)),
            pl.BlockSpec((1, s, W_SMALL), lambda bi: (bi, 0, 0)),
            pl.BlockSpec((1, B_KV_RANK), lambda bi: (0, 0)),
            pl.BlockSpec(wuk_pad.shape, lambda bi: (0, 0, 0)),
            pl.BlockSpec(wuv_pair.shape, lambda bi: (0, 0, 0)),
            pl.BlockSpec(slopes.shape, lambda bi: (0, 0, 0)),
        ],
        out_specs=pl.BlockSpec((1, s, B_WIDTH), lambda bi: (bi, 0, 0)),
        scratch_shapes=[
            pltpu.VMEM((B_HEADS, B_TQ, SEQ), F32),
            pltpu.VMEM((SEQ, B_KV_RANK), BF16),
            pltpu.VMEM((SEQ, LANES), BF16),
            pltpu.VMEM((B_TQ, SEQ), F32),
            pltpu.VMEM((B_TQ, SEQ), jnp.int32),
            pltpu.VMEM((B_TQ, SEQ), F32),
        ],
        compiler_params=_params(("arbitrary",)),
        name="sparse_attention",
    )(bq3, iq3, small3, kv_norm_g, wuk_pad, wuv_pair, slopes)


MERGE_TM = 512
ROUTER_ROWS = 32


def _first_max_onehot(rows):
    mx = rows[0]
    for r in rows[1:]:
        mx = jnp.maximum(mx, r)
    taken = jnp.zeros_like(mx)
    hot = []
    for r in rows:
        h = jnp.where((r == mx) & (taken == 0.0), 1.0, 0.0)
        taken = taken + h
        hot.append(h)
    return hot, mx


def _softmax_rows(rows):
    mx = rows[0]
    for r in rows[1:]:
        mx = jnp.maximum(mx, r)
    e = [jnp.exp(r - mx) for r in rows]
    tot = e[0]
    for r in e[1:]:
        tot = tot + r
    return [r / tot for r in e]


def _merge_kernel(ya_ref, yb_ref, gates_ref, x_ref, mod_ref, wa_ref, wb_ref, wo_ref, g1_ref, b1_ref,
                  wr_ref, br_ref, x1_ref, u2_ref, comb_ref):
    m = mod_ref[0]
    pa = _dot(ya_ref[...], wa_ref[...])
    pb = _dot(yb_ref[...], wb_ref[...])
    gt = jax.nn.sigmoid(gates_ref[...].astype(F32))
    mixed = gt[:, 0:D_MODEL] * pa + gt[:, D_MODEL:2 * D_MODEL] * pb
    z = _dot(mixed.astype(BF16), wo_ref[...])
    x1 = _layer_norm(DEEPNORM_ALPHA * x_ref[...] + m[2:3] * z, g1_ref[...], b1_ref[...])
    x1_ref[...] = x1
    u2 = x1 * (1.0 + m[4:5]) + m[3:4]
    u2_ref[...] = u2.astype(u2_ref.dtype)

    logits = _dot_nt(wr_ref[...], u2, precision=lax.Precision.HIGHEST) + br_ref[...]
    g_prob = _softmax_rows([logits[k:k + 1] for k in range(N_GROUPS)])
    g_hot, g_top = _first_max_onehot(g_prob)
    e_logit = []
    for jx in range(EXPERTS_PER_GROUP):
        acc = jnp.zeros_like(g_top)
        for g in range(N_GROUPS):
            row = N_GROUPS + g * EXPERTS_PER_GROUP + jx
            acc = acc + logits[row:row + 1] * g_hot[g]
        e_logit.append(acc)
    e_prob = _softmax_rows(e_logit)
    hot1, p1 = _first_max_onehot(e_prob)
    rest = [jnp.where(h > 0.0, NEG_INF, p) for h, p in zip(hot1, e_prob)]
    hot2, p2 = _first_max_onehot(rest)
    tot = p1 + p2
    w1 = g_top * (p1 / tot)
    w2 = g_top * (p2 / tot)
    for g in range(N_GROUPS):
        for jx in range(EXPERTS_PER_GROUP):
            e = g * EXPERTS_PER_GROUP + jx
            comb_ref[e:e + 1, :] = g_hot[g] * (w1 * hot1[jx] + w2 * hot2[jx])


def _merge(ya, yb, gates, x2, mod3, wa, wb, wo, ln_g, ln_b, w_router_t, b_router):
    n_tok, d = x2.shape
    tiles_per_batch = SEQ // MERGE_TM
    tok = lambda n: pl.BlockSpec((MERGE_TM, n), lambda i: (i, 0))
    full = lambda a: pl.BlockSpec(a.shape, lambda i: (0,) * a.ndim)
    return pl.pallas_call(
        _merge_kernel,
        out_shape=[
            jax.ShapeDtypeStruct((n_tok, d), F32),
            jax.ShapeDtypeStruct((n_tok, d), BF16),
            jax.ShapeDtypeStruct((N_EXPERTS, n_tok), F32),
        ],
        grid=(n_tok // MERGE_TM,),
        in_specs=[
            tok(A_WIDTH), tok(B_WIDTH), tok(2 * d), tok(d),
            pl.BlockSpec((1, 6, d), lambda i: (i // tiles_per_batch, 0, 0)),
            full(wa), full(wb), full(wo), full(ln_g), full(ln_b), full(w_router_t), full(b_router),
        ],
        out_specs=[tok(d), tok(d), pl.BlockSpec((N_EXPERTS, MERGE_TM), lambda i: (0, i))],
        compiler_params=_params(("arbitrary",)),
        name="merge_router",
    )(ya, yb, gates, x2, mod3, wa, wb, wo, ln_g, ln_b, w_router_t, b_router)


MOE_TM = 512


def _moe_kernel(u_ref, comb_ref, x1_ref, mod_ref, wg_ref, wu_ref, wd_ref, g2_ref, b2_ref, o_ref, acc_ref):
    g = pl.program_id(1)

    @pl.when(g == 0)
    def _():
        acc_ref[...] = jnp.zeros_like(acc_ref)

    u = u_ref[...]
    h = jax.nn.silu(_dot(u, wg_ref[0])) * _dot(u, wu_ref[0])
    comb = comb_ref[...]
    parts = []
    for jx in range(EXPERTS_PER_GROUP):
        c = jnp.zeros((MOE_TM, 1), F32)
        for gg in range(N_GROUPS):
            e = gg * EXPERTS_PER_GROUP + jx
            c = c + jnp.where(g == gg, comb[:, e:e + 1], 0.0)
        parts.append((h[:, jx * D_FF_EXPERT:(jx + 1) * D_FF_EXPERT] * c).astype(BF16))
    acc_ref[...] += _dot(jnp.concatenate(parts, axis=1), wd_ref[0])

    @pl.when(g == N_GROUPS - 1)
    def _():
        m = mod_ref[0]
        o_ref[...] = _layer_norm(DEEPNORM_ALPHA * x1_ref[...] + m[5:6] * acc_ref[...], g2_ref[...], b2_ref[...])


def _moe(u2, comb, x1, mod3, wg, wu, wd, ln_g, ln_b):
    n_tok, d = x1.shape
    tiles_per_batch = SEQ // MOE_TM
    tok = lambda n: pl.BlockSpec((MOE_TM, n), lambda i, g: (i, 0))
    wspec = pl.BlockSpec((1, d, d), lambda i, g: (g, 0, 0))
    vec = pl.BlockSpec((1, d), lambda i, g: (0, 0))
    return pl.pallas_call(
        _moe_kernel,
        out_shape=jax.ShapeDtypeStruct((n_tok, d), F32),
        grid=(n_tok // MOE_TM, N_GROUPS),
        in_specs=[
            tok(d), tok(N_EXPERTS), tok(d),
            pl.BlockSpec((1, 6, d), lambda i, g: (i // tiles_per_batch, 0, 0)),
            wspec, wspec, wspec, vec, vec,
        ],
        out_specs=tok(d),
        scratch_shapes=[pltpu.VMEM((MOE_TM, d), F32)],
        compiler_params=_params(("arbitrary", "arbitrary")),
        name="moe",
    )(u2, comb, x1, mod3, wg, wu, wd, ln_g, ln_b)


def _regroup_w_in(w):
    o = np.cumsum((512, 512, 512, 512, 128, 512, 64, 8, 2048))
    a_qkv, b_q, b_kv = w[:, :o[2]], w[:, o[2]:o[3]], w[:, o[3]:o[4]]
    i_q, i_k, i_w, gates = w[:, o[4]:o[5]], w[:, o[5]:o[6]], w[:, o[6]:o[7]], w[:, o[7]:o[8]]
    pad = jnp.zeros((w.shape[0], LANES - IDX_HEADS), w.dtype)
    return jnp.concatenate([a_qkv, b_q, i_q, gates, b_kv, i_k, i_k, i_w, pad], axis=1).astype(BF16)


def _pad_w_uk(w_uk):
    wt = jnp.transpose(w_uk, (1, 2, 0))
    z = jnp.zeros_like(wt)
    even = jnp.concatenate([wt, z], axis=1)
    odd = jnp.concatenate([z, wt], axis=1)
    sel = (jnp.arange(B_HEADS) % 2 == 0)[:, None, None]
    return jnp.where(sel, even, odd).astype(BF16)


def _pair_w_uv(w_uv):
    wv = jnp.transpose(w_uv, (1, 0, 2))
    z = jnp.zeros_like(wv[0::2])
    top = jnp.concatenate([wv[0::2], z], axis=2)
    bot = jnp.concatenate([z, wv[1::2]], axis=2)
    return jnp.concatenate([top, bot], axis=1).astype(BF16)


def _group_experts(w_gate, w_up, w_down):
    d = w_gate.shape[1]
    cat = lambda w: jnp.transpose(w.reshape(N_GROUPS, EXPERTS_PER_GROUP, d, D_FF_EXPERT), (0, 2, 1, 3)).reshape(
        N_GROUPS, d, EXPERTS_PER_GROUP * D_FF_EXPERT).astype(BF16)
    wd = w_down.reshape(N_GROUPS, EXPERTS_PER_GROUP * D_FF_EXPERT, d).astype(BF16)
    return cat(w_gate), cat(w_up), wd


def kernel(x, c, ada_w, ada_b, w_in, lambda_q1, lambda_k1, lambda_q2, lambda_k2, a_subln_g, kv_norm_g, w_uk, w_uv,
           w_a_proj, w_b_proj, w_o, ln1_g, ln1_b, w_group, b_group, w_expert_router, b_expert_router,
           w_exp_gate, w_exp_up, w_exp_down, ln2_g, ln2_b):
    b, s, d = x.shape
    assert (s, d) == (SEQ, D_MODEL) and ada_w.shape[0] == DEPTH
    slopes_a, slopes_b = _alibi_slopes()
    lane_rep = lambda v: jnp.asarray(np.repeat(v[:, None, None], LANES, axis=2))
    x2 = x.reshape(b * s, d)
    l = 0
    mod3 = _modulation(c, ada_w[l], ada_b[l]).reshape(b, 6, d)

    qkv, bq, iq, gates, small = _projection(x2, mod3, _regroup_w_in(w_in[l]))
    lam_vecs = jnp.stack([lambda_q1[l], lambda_k1[l], lambda_q2[l], lambda_k2[l]]).astype(F32)
    y_a = _diff_attention(qkv.reshape(b, s, W_QKV), lam_vecs, a_subln_g[l].reshape(1, A_V_DIM), lane_rep(slopes_a))
    y_b = _sparse_attention(bq.reshape(b, s, -1), iq.reshape(b, s, -1), small.reshape(b, s, -1),
                            kv_norm_g[l].reshape(1, B_KV_RANK), _pad_w_uk(w_uk[l]), _pair_w_uv(w_uv[l]),
                            lane_rep(slopes_b))

    w_router_t = jnp.zeros((ROUTER_ROWS, d), F32).at[:N_GROUPS].set(w_group[l].T).at[
        N_GROUPS:N_GROUPS + N_EXPERTS].set(w_expert_router[l].T)
    b_router = jnp.zeros((ROUTER_ROWS, 1), F32).at[:N_GROUPS, 0].set(b_group[l]).at[
        N_GROUPS:N_GROUPS + N_EXPERTS, 0].set(b_expert_router[l])
    x1, u2, comb_t = _merge(y_a.reshape(b * s, A_WIDTH), y_b.reshape(b * s, B_WIDTH), gates, x2, mod3,
                            w_a_proj[l].astype(BF16), w_b_proj[l].astype(BF16), w_o[l].astype(BF16),
                            ln1_g[l].reshape(1, d), ln1_b[l].reshape(1, d), w_router_t, b_router)

    wg, wu, wd = _group_experts(w_exp_gate[l], w_exp_up[l], w_exp_down[l])
    out = _moe(u2, comb_t.T, x1, mod3, wg, wu, wd, ln2_g[l].reshape(1, d), ln2_b[l].reshape(1, d))
    return out.reshape(b, s, d)
```

```python
import functools
import math

import numpy as np
import jax
import jax.numpy as jnp
from jax import lax
from jax.experimental import pallas as pl
from jax.experimental.pallas import tpu as pltpu

D_MODEL = 1024
SEQ = 2048
CHUNK = 64
A_QK_DIM = 64
A_V_DIM = 128
A_HEADS = 4
A_WIDTH = A_HEADS * A_V_DIM
B_HEAD_DIM = 64
B_HEADS = 8
B_WIDTH = B_HEADS * B_HEAD_DIM
B_KV_RANK = 128
IDX_HEADS = 8
IDX_DIM = 64
TOPK = 256
N_ALIBI_HEADS = A_HEADS + B_HEADS
N_GROUPS = 4
EXPERTS_PER_GROUP = 4
N_EXPERTS = 16
D_FF_EXPERT = 256
LN_EPS = 1e-5
RMS_EPS = 1e-5
DEPTH = 1
DEEPNORM_ALPHA = (2.0 * DEPTH) ** 0.25
LAM_INIT = 0.8 - 0.6 * math.exp(-0.3 * 0)

LANES = 128
VMEM_LIMIT_BYTES = 56 * 1024 * 1024

F32 = jnp.float32
BF16 = jnp.bfloat16
NEG_INF = float("-inf")

_NT = (((1,), (1,)), ((), ()))


def _dot(a, b, **kw):
    return jnp.dot(a, b, preferred_element_type=F32, **kw)


def _dot_nt(a, b, **kw):
    return lax.dot_general(a, b, _NT, preferred_element_type=F32, **kw)


def _alibi_slopes():
    n = N_ALIBI_HEADS
    slopes = (2.0 ** (-8.0 * np.arange(1, n + 1) / n)).astype(np.float32)
    a_idx = np.arange(A_HEADS) * (n // A_HEADS)
    b_idx = np.setdiff1d(np.arange(n), a_idx)
    return slopes[a_idx], slopes[b_idx]


def _layer_norm(h, g, b):
    mu = jnp.mean(h, axis=-1, keepdims=True)
    d = h - mu
    var = jnp.mean(d * d, axis=-1, keepdims=True)
    return d * lax.rsqrt(var + LN_EPS) * g + b


def _params(sem):
    return pltpu.CompilerParams(dimension_semantics=sem, vmem_limit_bytes=VMEM_LIMIT_BYTES)


MOD_TN = 1536


def _mod_kernel(c_ref, w_ref, b_ref, o_ref):
    c = c_ref[...]
    cond = c * jax.nn.sigmoid(c)
    o_ref[...] = _dot(cond, w_ref[...], precision=lax.Precision.HIGHEST) + b_ref[...]


def _modulation(c, ada_w, ada_b):
    b, d = c.shape
    n = ada_w.shape[1]
    return pl.pallas_call(
        _mod_kernel,
        out_shape=jax.ShapeDtypeStruct((b, n), F32),
        grid=(n // MOD_TN,),
        in_specs=[
            pl.BlockSpec((b, d), lambda j: (0, 0)),
            pl.BlockSpec((d, MOD_TN), lambda j: (0, j)),
            pl.BlockSpec((1, MOD_TN), lambda j: (0, j)),
        ],
        out_specs=pl.BlockSpec((b, MOD_TN), lambda j: (0, j)),
        compiler_params=_params(("arbitrary",)),
        name="modulation",
    )(c, ada_w, ada_b.reshape(1, n))


PROJ_TM = 512
W_QKV = 3 * A_WIDTH
W_SMALL = 3 * LANES
PROJ_COLS = (W_QKV, B_WIDTH, IDX_HEADS * IDX_DIM, 2 * D_MODEL, W_SMALL)


def _proj_kernel(x_ref, mod_ref, w_ref, qkv_ref, bq_ref, iq_ref, gates_ref, small_ref):
    m = mod_ref[0]
    u = (x_ref[...] * (1.0 + m[1:2]) + m[0:1]).astype(BF16)
    off = 0
    for o_ref, n in zip((qkv_ref, bq_ref, iq_ref, gates_ref, small_ref), PROJ_COLS):
        o_ref[...] = _dot(u, w_ref[:, off:off + n]).astype(o_ref.dtype)
        off += n


def _projection(x2, mod3, w_cat):
    n_tok, d = x2.shape
    tiles_per_batch = SEQ // PROJ_TM
    out_dtypes = (BF16, BF16, BF16, BF16, F32)
    return pl.pallas_call(
        _proj_kernel,
        out_shape=[jax.ShapeDtypeStruct((n_tok, n), dt) for n, dt in zip(PROJ_COLS, out_dtypes)],
        grid=(n_tok // PROJ_TM,),
        in_specs=[
            pl.BlockSpec((PROJ_TM, d), lambda i: (i, 0)),
            pl.BlockSpec((1, 6, d), lambda i: (i // tiles_per_batch, 0, 0)),
            pl.BlockSpec(w_cat.shape, lambda i: (0, 0)),
        ],
        out_specs=[pl.BlockSpec((PROJ_TM, n), lambda i: (i, 0)) for n in PROJ_COLS],
        compiler_params=_params(("arbitrary",)),
        name="projection",
    )(x2, mod3, w_cat)


A_TQ = 256


def _diff_attn_kernel(q_ref, k_ref, v_ref, lam_ref, g_ref, slope_ref, o_ref, bias_ref):
    slope = slope_ref[0][:, 0:1]
    r = lax.broadcasted_iota(jnp.int32, (A_TQ, SEQ), 0)
    j = lax.broadcasted_iota(jnp.int32, (A_TQ, SEQ), 1)
    dist = jnp.abs(r + (SEQ - A_TQ) - j).astype(F32)
    visible = (j - (SEQ - A_TQ)) // CHUNK <= r // CHUNK
    bias_ref[...] = jnp.where(visible, -slope * dist, NEG_INF)

    lv = lam_ref[...]
    lam = (jnp.exp(jnp.sum(lv[0:1] * lv[1:2], axis=1, keepdims=True))
           - jnp.exp(jnp.sum(lv[2:3] * lv[3:4], axis=1, keepdims=True)) + LAM_INIT)
    lane = lax.broadcasted_iota(jnp.int32, (A_TQ, 2 * A_QK_DIM), 1)
    scale = A_QK_DIM ** -0.5
    for i in range(SEQ // A_TQ):
        q0 = i * A_TQ
        kv = q0 + A_TQ
        q = q_ref[0, q0:q0 + A_TQ, :] * scale
        k = k_ref[0, 0:kv, :]
        bias = bias_ref[:, SEQ - kv:SEQ]
        ps, inv_ls = [], []
        for mth in range(2):
            qm = jnp.where((lane // A_QK_DIM) == mth, q, jnp.zeros_like(q))
            s = _dot_nt(qm, k) + bias
            mx = jnp.max(s, axis=-1, keepdims=True)
            p = jnp.exp(s - mx)
            ps.append(p)
            inv_ls.append(1.0 / jnp.sum(p, axis=-1, keepdims=True))
        attn = ps[0] * inv_ls[0] - ps[1] * (lam * inv_ls[1])
        o = _dot(attn.astype(BF16), v_ref[0, 0:kv, :])
        y = o * lax.rsqrt(jnp.mean(o * o, axis=-1, keepdims=True) + RMS_EPS) * g_ref[...]
        o_ref[0, q0:q0 + A_TQ, :] = (y * (1.0 - LAM_INIT)).astype(o_ref.dtype)


def _diff_attention(qkv3, lam_vecs, subln_g, slopes):
    b, s, _ = qkv3.shape
    blk = (1, s, A_V_DIM)
    return pl.pallas_call(
        _diff_attn_kernel,
        out_shape=jax.ShapeDtypeStruct((b, s, A_WIDTH), BF16),
        grid=(b, A_HEADS),
        in_specs=[
            pl.BlockSpec(blk, lambda bi, h: (bi, 0, h)),
            pl.BlockSpec(blk, lambda bi, h: (bi, 0, A_HEADS + h)),
            pl.BlockSpec(blk, lambda bi, h: (bi, 0, 2 * A_HEADS + h)),
            pl.BlockSpec(lam_vecs.shape, lambda bi, h: (0, 0)),
            pl.BlockSpec((1, A_V_DIM), lambda bi, h: (0, 0)),
            pl.BlockSpec((1, 1, LANES), lambda bi, h: (h, 0, 0)),
        ],
        out_specs=pl.BlockSpec(blk, lambda bi, h: (bi, 0, h)),
        scratch_shapes=[pltpu.VMEM((A_TQ, SEQ), F32)],
        compiler_params=_params(("arbitrary", "arbitrary")),
        name="diff_attention",
    )(qkv3, qkv3, qkv3, lam_vecs, subln_g, slopes)


B_TQ = 128
HEAD_GROUP = 4
SMALL_KV = slice(0, 128)
SMALL_IK = slice(128, 256)
SMALL_IW = slice(256, 384)
BISECT_FIRST = 14
BISECT_MORE = 4
BISECT_MAX_ROUNDS = 80


def _row_count(mask_f32):
    return jnp.sum(mask_f32, axis=1, keepdims=True)


def _topk_mask(score_ref, selb_ref, kv):
    s = score_ref[:, 0:kv]

    def bisect(_, st):
        lo, hi = st
        probe = 0.5 * lo + 0.5 * hi
        ge = _row_count(jnp.where(score_ref[:, 0:kv] >= probe, 1.0, 0.0)) >= TOPK
        return jnp.where(ge, probe, lo), jnp.where(ge, hi, probe)

    def candidate(lo):
        sc = score_ref[:, 0:kv]
        t_val = jnp.min(jnp.where(sc >= lo, sc, jnp.inf), axis=1, keepdims=True)
        n_gt = _row_count(jnp.where(sc > t_val, 1.0, 0.0))
        return t_val, n_gt, jnp.sum(jnp.where(n_gt >= TOPK, 1.0, 0.0))

    lo = jnp.min(jnp.where(s == NEG_INF, jnp.inf, s), axis=1, keepdims=True)
    hi = jnp.max(s, axis=1, keepdims=True)
    lo, hi = lax.fori_loop(0, BISECT_FIRST, bisect, (lo, hi))

    def unresolved(st):
        return (st[5] > 0.0) & (st[0] < BISECT_MAX_ROUNDS)

    def refine(st):
        lo, hi = lax.fori_loop(0, BISECT_MORE, bisect, (st[1], st[2]))
        return (st[0] + 1, lo, hi) + candidate(lo)

    _, _, _, t_val, n_gt, _ = lax.while_loop(unresolved, refine, (jnp.int32(0), lo, hi) + candidate(lo))
    gt = s > t_val
    eqf = jnp.where(s == t_val, 1.0, 0.0)
    need = TOPK - n_gt
    selb_ref[:, 0:kv] = jnp.where(s >= t_val, 0.0, NEG_INF)
    tie_overflow = jnp.max(jnp.abs(_row_count(eqf) - need))

    @pl.when(tie_overflow > 0.0)
    def _():
        idx = lax.broadcasted_iota(jnp.int32, (B_TQ, kv), 1)

        def idx_step(b, j):
            c = j | lax.shift_left(jnp.int32(1), 10 - b)
            before = _row_count(jnp.where(idx < c, eqf, 0.0))
            return jnp.where(before < need, c, j)

        j_max = lax.fori_loop(0, 11, idx_step, jnp.zeros((B_TQ, 1), jnp.int32))
        keep = gt | ((eqf > 0.0) & (idx <= j_max))
        selb_ref[:, 0:kv] = jnp.where(keep, 0.0, NEG_INF)


def _sparse_attn_kernel(bq_ref, iq_ref, small_ref, kvg_ref, wuk_ref, wuv_ref, slope_ref, o_ref,
                        bias_ref, ckv_ref, ik_ref, score_ref, selb_ref):
    @pl.when(pl.program_id(0) == 0)
    def _():
        r = lax.broadcasted_iota(jnp.int32, (B_TQ, SEQ), 0)
        j = lax.broadcasted_iota(jnp.int32, (B_TQ, SEQ), 1)
        dist = jnp.abs(r + (SEQ - B_TQ) - j).astype(F32)
        for h in range(B_HEADS):
            bias_ref[h] = -slope_ref[h][:, 0:1] * dist

    kv_lat = small_ref[0, :, SMALL_KV]
    ckv = kv_lat * lax.rsqrt(jnp.mean(kv_lat * kv_lat, axis=-1, keepdims=True) + RMS_EPS) * kvg_ref[...]
    ckv_ref[...] = ckv.astype(BF16)
    ik_ref[...] = small_ref[0, :, SMALL_IK].astype(BF16)

    lane = lax.broadcasted_iota(jnp.int32, (B_TQ, LANES), 1)
    rr = lax.broadcasted_iota(jnp.int32, (B_TQ, B_TQ), 0)
    cc = lax.broadcasted_iota(jnp.int32, (B_TQ, B_TQ), 1)
    diag_visible = cc // CHUNK <= rr // CHUNK
    scale = B_HEAD_DIM ** -0.5
    n_groups = B_HEADS // HEAD_GROUP

    for i in range(SEQ // B_TQ):
        q0 = i * B_TQ
        kv = q0 + B_TQ
        rows = slice(q0, q0 + B_TQ)
        if kv <= TOPK:
            if q0:
                selb_ref[:, 0:q0] = jnp.zeros((B_TQ, q0), F32)
            selb_ref[:, q0:kv] = jnp.where(diag_visible, 0.0, NEG_INF)
        else:
            iq = iq_ref[0, rows, :]
            iw = small_ref[0, rows, SMALL_IW]
            ik = ik_ref[0:kv, :]
            score = jnp.zeros((B_TQ, kv), F32)
            for g in range(n_groups):
                parts = []
                for hh in range(HEAD_GROUP):
                    h = g * HEAD_GROUP + hh
                    pair = iq[:, (h // 2) * LANES:(h // 2 + 1) * LANES]
                    parts.append(jnp.where((lane // IDX_DIM) == (h % 2), pair, jnp.zeros_like(pair)))
                x = _dot_nt(jnp.concatenate(parts, axis=0), ik)
                for hh in range(HEAD_GROUP):
                    h = g * HEAD_GROUP + hh
                    score = score + iw[:, h:h + 1] * jnp.maximum(x[hh * B_TQ:(hh + 1) * B_TQ], 0.0)
            score_ref[:, 0:q0] = score[:, 0:q0]
            score_ref[:, q0:kv] = jnp.where(diag_visible, score[:, q0:kv], NEG_INF)
            _topk_mask(score_ref, selb_ref, kv)

        bq = bq_ref[0, rows, :]
        ck = ckv_ref[0:kv, :]
        selb = selb_ref[:, 0:kv]
        lat = []
        for g in range(n_groups):
            parts = []
            for hh in range(HEAD_GROUP):
                h = g * HEAD_GROUP + hh
                pair = bq[:, (h // 2) * LANES:(h // 2 + 1) * LANES]
                parts.append((_dot(pair, wuk_ref[h]) * scale).astype(BF16))
            s = _dot_nt(jnp.concatenate(parts, axis=0), ck)
            probs = []
            for hh in range(HEAD_GROUP):
                h = g * HEAD_GROUP + hh
                sh = s[hh * B_TQ:(hh + 1) * B_TQ] + bias_ref[h, :, SEQ - kv:SEQ] + selb
                mx = jnp.max(sh, axis=-1, keepdims=True)
                p = jnp.exp(sh - mx)
                probs.append((p * (1.0 / jnp.sum(p, axis=-1, keepdims=True))).astype(BF16))
            o = _dot(jnp.concatenate(probs, axis=0), ck)
            lat.extend(o[hh * B_TQ:(hh + 1) * B_TQ] for hh in range(HEAD_GROUP))
        outs = []
        for jp in range(B_HEADS // 2):
            pair = jnp.concatenate([lat[2 * jp], lat[2 * jp + 1]], axis=1).astype(BF16)
            outs.append(_dot(pair, wuv_ref[jp]))
        o_ref[0, rows, :] = jnp.concatenate(outs, axis=1).astype(o_ref.dtype)


def _sparse_attention(bq3, iq3, small3, kv_norm_g, wuk_pad, wuv_pair, slopes):
    b, s, _ = bq3.shape
    return pl.pallas_call(
        _sparse_attn_kernel,
        out_shape=jax.ShapeDtypeStruct((b, s, B_WIDTH), BF16),
        grid=(b,),
        in_specs=[
            pl.BlockSpec((1, s, B_WIDTH), lambda bi: (bi, 0, 0)),
            pl.BlockSpec((1, s, IDX_HEADS * IDX_DIM), lambda bi: (bi, 0, 0)),
            pl.BlockSpec((1, s, W_SMALL), lambda bi: (bi, 0, 0)),
            pl.BlockSpec((1, B_KV_RANK), lambda bi: (0, 0)),
            pl.BlockSpec(wuk_pad.shape, lambda bi: (0, 0, 0)),
            pl.BlockSpec(wuv_pair.shape, lambda bi: (0, 0, 0)),
            pl.BlockSpec(slopes.shape, lambda bi: (0, 0, 0)),
        ],
        out_specs=pl.BlockSpec((1, s, B_WIDTH), lambda bi: (bi, 0, 0)),
        scratch_shapes=[
            pltpu.VMEM((B_HEADS, B_TQ, SEQ), F32),
            pltpu.VMEM((SEQ, B_KV_RANK), BF16),
            pltpu.VMEM((SEQ, LANES), BF16),
            pltpu.VMEM((B_TQ, SEQ), F32),
            pltpu.VMEM((B_TQ, SEQ), F32),
        ],
        compiler_params=_params(("arbitrary",)),
        name="sparse_attention",
    )(bq3, iq3, small3, kv_norm_g, wuk_pad, wuv_pair, slopes)


MERGE_TM = 512
ROUTER_ROWS = 32


def _first_max_onehot(rows):
    mx = rows[0]
    for r in rows[1:]:
        mx = jnp.maximum(mx, r)
    taken = jnp.zeros_like(mx)
    hot = []
    for r in rows:
        h = jnp.where((r == mx) & (taken == 0.0), 1.0, 0.0)
        taken = taken + h
        hot.append(h)
    return hot, mx


def _softmax_rows(rows):
    mx = rows[0]
    for r in rows[1:]:
        mx = jnp.maximum(mx, r)
    e = [jnp.exp(r - mx) for r in rows]
    tot = e[0]
    for r in e[1:]:
        tot = tot + r
    return [r / tot for r in e]


def _merge_kernel(ya_ref, yb_ref, gates_ref, x_ref, mod_ref, wa_ref, wb_ref, wo_ref, g1_ref, b1_ref,
                  wr_ref, br_ref, x1_ref, u2_ref, comb_ref):
    m = mod_ref[0]
    pa = _dot(ya_ref[...], wa_ref[...])
    pb = _dot(yb_ref[...], wb_ref[...])
    gt = jax.nn.sigmoid(gates_ref[...].astype(F32))
    mixed = gt[:, 0:D_MODEL] * pa + gt[:, D_MODEL:2 * D_MODEL] * pb
    z = _dot(mixed.astype(BF16), wo_ref[...])
    x1 = _layer_norm(DEEPNORM_ALPHA * x_ref[...] + m[2:3] * z, g1_ref[...], b1_ref[...])
    x1_ref[...] = x1
    u2 = x1 * (1.0 + m[4:5]) + m[3:4]
    u2_ref[...] = u2.astype(u2_ref.dtype)

    logits = _dot_nt(wr_ref[...], u2, precision=lax.Precision.HIGHEST) + br_ref[...]
    g_prob = _softmax_rows([logits[k:k + 1] for k in range(N_GROUPS)])
    g_hot, g_top = _first_max_onehot(g_prob)
    e_logit = []
    for jx in range(EXPERTS_PER_GROUP):
        acc = jnp.zeros_like(g_top)
        for g in range(N_GROUPS):
            row = N_GROUPS + g * EXPERTS_PER_GROUP + jx
            acc = acc + logits[row:row + 1] * g_hot[g]
        e_logit.append(acc)
    e_prob = _softmax_rows(e_logit)
    hot1, p1 = _first_max_onehot(e_prob)
    rest = [jnp.where(h > 0.0, NEG_INF, p) for h, p in zip(hot1, e_prob)]
    hot2, p2 = _first_max_onehot(rest)
    tot = p1 + p2
    w1 = g_top * (p1 / tot)
    w2 = g_top * (p2 / tot)
    for g in range(N_GROUPS):
        for jx in range(EXPERTS_PER_GROUP):
            e = g * EXPERTS_PER_GROUP + jx
            comb_ref[e:e + 1, :] = g_hot[g] * (w1 * hot1[jx] + w2 * hot2[jx])


def _merge(ya, yb, gates, x2, mod3, wa, wb, wo, ln_g, ln_b, w_router_t, b_router):
    n_tok, d = x2.shape
    tiles_per_batch = SEQ // MERGE_TM
    tok = lambda n: pl.BlockSpec((MERGE_TM, n), lambda i: (i, 0))
    full = lambda a: pl.BlockSpec(a.shape, lambda i: (0,) * a.ndim)
    return pl.pallas_call(
        _merge_kernel,
        out_shape=[
            jax.ShapeDtypeStruct((n_tok, d), F32),
            jax.ShapeDtypeStruct((n_tok, d), BF16),
            jax.ShapeDtypeStruct((N_EXPERTS, n_tok), F32),
        ],
        grid=(n_tok // MERGE_TM,),
        in_specs=[
            tok(A_WIDTH), tok(B_WIDTH), tok(2 * d), tok(d),
            pl.BlockSpec((1, 6, d), lambda i: (i // tiles_per_batch, 0, 0)),
            full(wa), full(wb), full(wo), full(ln_g), full(ln_b), full(w_router_t), full(b_router),
        ],
        out_specs=[tok(d), tok(d), pl.BlockSpec((N_EXPERTS, MERGE_TM), lambda i: (0, i))],
        compiler_params=_params(("arbitrary",)),
        name="merge_router",
    )(ya, yb, gates, x2, mod3, wa, wb, wo, ln_g, ln_b, w_router_t, b_router)


MOE_TM = 512


def _moe_kernel(u_ref, comb_ref, x1_ref, mod_ref, wg_ref, wu_ref, wd_ref, g2_ref, b2_ref, o_ref, acc_ref):
    g = pl.program_id(1)

    @pl.when(g == 0)
    def _():
        acc_ref[...] = jnp.zeros_like(acc_ref)

    u = u_ref[...]
    h = jax.nn.silu(_dot(u, wg_ref[0])) * _dot(u, wu_ref[0])
    comb = comb_ref[...]
    parts = []
    for jx in range(EXPERTS_PER_GROUP):
        c = jnp.zeros((MOE_TM, 1), F32)
        for gg in range(N_GROUPS):
            e = gg * EXPERTS_PER_GROUP + jx
            c = c + jnp.where(g == gg, comb[:, e:e + 1], 0.0)
        parts.append((h[:, jx * D_FF_EXPERT:(jx + 1) * D_FF_EXPERT] * c).astype(BF16))
    acc_ref[...] += _dot(jnp.concatenate(parts, axis=1), wd_ref[0])

    @pl.when(g == N_GROUPS - 1)
    def _():
        m = mod_ref[0]
        o_ref[...] = _layer_norm(DEEPNORM_ALPHA * x1_ref[...] + m[5:6] * acc_ref[...], g2_ref[...], b2_ref[...])


def _moe(u2, comb, x1, mod3, wg, wu, wd, ln_g, ln_b):
    n_tok, d = x1.shape
    tiles_per_batch = SEQ // MOE_TM
    tok = lambda n: pl.BlockSpec((MOE_TM, n), lambda i, g: (i, 0))
    wspec = pl.BlockSpec((1, d, d), lambda i, g: (g, 0, 0))
    vec = pl.BlockSpec((1, d), lambda i, g: (0, 0))
    return pl.pallas_call(
        _moe_kernel,
        out_shape=jax.ShapeDtypeStruct((n_tok, d), F32),
        grid=(n_tok // MOE_TM, N_GROUPS),
        in_specs=[
            tok(d), tok(N_EXPERTS), tok(d),
            pl.BlockSpec((1, 6, d), lambda i, g: (i // tiles_per_batch, 0, 0)),
            wspec, wspec, wspec, vec, vec,
        ],
        out_specs=tok(d),
        scratch_shapes=[pltpu.VMEM((MOE_TM, d), F32)],
        compiler_params=_params(("arbitrary", "arbitrary")),
        name="moe",
    )(u2, comb, x1, mod3, wg, wu, wd, ln_g, ln_b)


def _regroup_w_in(w):
    o = np.cumsum((512, 512, 512, 512, 128, 512, 64, 8, 2048))
    a_qkv, b_q, b_kv = w[:, :o[2]], w[:, o[2]:o[3]], w[:, o[3]:o[4]]
    i_q, i_k, i_w, gates = w[:, o[4]:o[5]], w[:, o[5]:o[6]], w[:, o[6]:o[7]], w[:, o[7]:o[8]]
    pad = jnp.zeros((w.shape[0], LANES - IDX_HEADS), w.dtype)
    return jnp.concatenate([a_qkv, b_q, i_q, gates, b_kv, i_k, i_k, i_w, pad], axis=1).astype(BF16)


def _pad_w_uk(w_uk):
    wt = jnp.transpose(w_uk, (1, 2, 0))
    z = jnp.zeros_like(wt)
    even = jnp.concatenate([wt, z], axis=1)
    odd = jnp.concatenate([z, wt], axis=1)
    sel = (jnp.arange(B_HEADS) % 2 == 0)[:, None, None]
    return jnp.where(sel, even, odd).astype(BF16)


def _pair_w_uv(w_uv):
    wv = jnp.transpose(w_uv, (1, 0, 2))
    z = jnp.zeros_like(wv[0::2])
    top = jnp.concatenate([wv[0::2], z], axis=2)
    bot = jnp.concatenate([z, wv[1::2]], axis=2)
    return jnp.concatenate([top, bot], axis=1).astype(BF16)


def _group_experts(w_gate, w_up, w_down):
    d = w_gate.shape[1]
    cat = lambda w: jnp.transpose(w.reshape(N_GROUPS, EXPERTS_PER_GROUP, d, D_FF_EXPERT), (0, 2, 1, 3)).reshape(
        N_GROUPS, d, EXPERTS_PER_GROUP * D_FF_EXPERT).astype(BF16)
    wd = w_down.reshape(N_GROUPS, EXPERTS_PER_GROUP * D_FF_EXPERT, d).astype(BF16)
    return cat(w_gate), cat(w_up), wd


def kernel(x, c, ada_w, ada_b, w_in, lambda_q1, lambda_k1, lambda_q2, lambda_k2, a_subln_g, kv_norm_g, w_uk, w_uv,
           w_a_proj, w_b_proj, w_o, ln1_g, ln1_b, w_group, b_group, w_expert_router, b_expert_router,
           w_exp_gate, w_exp_up, w_exp_down, ln2_g, ln2_b):
    b, s, d = x.shape
    assert (s, d) == (SEQ, D_MODEL) and ada_w.shape[0] == DEPTH
    slopes_a, slopes_b = _alibi_slopes()
    lane_rep = lambda v: jnp.asarray(np.repeat(v[:, None, None], LANES, axis=2))
    x2 = x.reshape(b * s, d)
    l = 0
    mod3 = _modulation(c, ada_w[l], ada_b[l]).reshape(b, 6, d)

    qkv, bq, iq, gates, small = _projection(x2, mod3, _regroup_w_in(w_in[l]))
    lam_vecs = jnp.stack([lambda_q1[l], lambda_k1[l], lambda_q2[l], lambda_k2[l]]).astype(F32)
    y_a = _diff_attention(qkv.reshape(b, s, W_QKV), lam_vecs, a_subln_g[l].reshape(1, A_V_DIM), lane_rep(slopes_a))
    y_b = _sparse_attention(bq.reshape(b, s, -1), iq.reshape(b, s, -1), small.reshape(b, s, -1),
                            kv_norm_g[l].reshape(1, B_KV_RANK), _pad_w_uk(w_uk[l]), _pair_w_uv(w_uv[l]),
                            lane_rep(slopes_b))

    w_router_t = jnp.zeros((ROUTER_ROWS, d), F32).at[:N_GROUPS].set(w_group[l].T).at[
        N_GROUPS:N_GROUPS + N_EXPERTS].set(w_expert_router[l].T)
    b_router = jnp.zeros((ROUTER_ROWS, 1), F32).at[:N_GROUPS, 0].set(b_group[l]).at[
        N_GROUPS:N_GROUPS + N_EXPERTS, 0].set(b_expert_router[l])
    x1, u2, comb_t = _merge(y_a.reshape(b * s, A_WIDTH), y_b.reshape(b * s, B_WIDTH), gates, x2, mod3,
                            w_a_proj[l].astype(BF16), w_b_proj[l].astype(BF16), w_o[l].astype(BF16),
                            ln1_g[l].reshape(1, d), ln1_b[l].reshape(1, d), w_router_t, b_router)

    wg, wu, wd = _group_experts(w_exp_gate[l], w_exp_up[l], w_exp_down[l])
    out = _moe(u2, comb_t.T, x1, mod3, wg, wu, wd, ln2_g[l].reshape(1, d), ln2_b[l].reshape(1, d))
    return out.reshape(b, s, d)
```

```python
import functools
import math

import numpy as np
import jax
import jax.numpy as jnp
from jax import lax
from jax.experimental import pallas as pl
from jax.experimental.pallas import tpu as pltpu

D_MODEL = 1024
SEQ = 2048
CHUNK = 64
A_QK_DIM = 64
A_V_DIM = 128
A_HEADS = 4
A_WIDTH = A_HEADS * A_V_DIM
B_HEAD_DIM = 64
B_HEADS = 8
B_WIDTH = B_HEADS * B_HEAD_DIM
B_KV_RANK = 128
IDX_HEADS = 8
IDX_DIM = 64
TOPK = 256
N_ALIBI_HEADS = A_HEADS + B_HEADS
N_GROUPS = 4
EXPERTS_PER_GROUP = 4
N_EXPERTS = 16
D_FF_EXPERT = 256
LN_EPS = 1e-5
RMS_EPS = 1e-5
DEPTH = 1
DEEPNORM_ALPHA = (2.0 * DEPTH) ** 0.25
LAM_INIT = 0.8 - 0.6 * math.exp(-0.3 * 0)

LANES = 128
VMEM_LIMIT_BYTES = 56 * 1024 * 1024

F32 = jnp.float32
BF16 = jnp.bfloat16
NEG_INF = float("-inf")

_NT = (((1,), (1,)), ((), ()))


def _dot(a, b, **kw):
    return jnp.dot(a, b, preferred_element_type=F32, **kw)


def _dot_nt(a, b, **kw):
    return lax.dot_general(a, b, _NT, preferred_element_type=F32, **kw)


def _alibi_slopes():
    n = N_ALIBI_HEADS
    slopes = (2.0 ** (-8.0 * np.arange(1, n + 1) / n)).astype(np.float32)
    a_idx = np.arange(A_HEADS) * (n // A_HEADS)
    b_idx = np.setdiff1d(np.arange(n), a_idx)
    return slopes[a_idx], slopes[b_idx]


def _layer_norm(h, g, b):
    mu = jnp.mean(h, axis=-1, keepdims=True)
    d = h - mu
    var = jnp.mean(d * d, axis=-1, keepdims=True)
    return d * lax.rsqrt(var + LN_EPS) * g + b


def _params(sem):
    return pltpu.CompilerParams(dimension_semantics=sem, vmem_limit_bytes=VMEM_LIMIT_BYTES)


MOD_TN = 1536


def _mod_kernel(c_ref, w_ref, b_ref, o_ref):
    c = c_ref[...]
    cond = c * jax.nn.sigmoid(c)
    o_ref[...] = _dot(cond, w_ref[...], precision=lax.Precision.HIGHEST) + b_ref[...]


def _modulation(c, ada_w, ada_b):
    b, d = c.shape
    n = ada_w.shape[1]
    return pl.pallas_call(
        _mod_kernel,
        out_shape=jax.ShapeDtypeStruct((b, n), F32),
        grid=(n // MOD_TN,),
        in_specs=[
            pl.BlockSpec((b, d), lambda j: (0, 0)),
            pl.BlockSpec((d, MOD_TN), lambda j: (0, j)),
            pl.BlockSpec((1, MOD_TN), lambda j: (0, j)),
        ],
        out_specs=pl.BlockSpec((b, MOD_TN), lambda j: (0, j)),
        compiler_params=_params(("arbitrary",)),
        name="modulation",
    )(c, ada_w, ada_b.reshape(1, n))


PROJ_TM = 512
W_QKV = 3 * A_WIDTH
W_SMALL = 3 * LANES
PROJ_COLS = (W_QKV, B_WIDTH, IDX_HEADS * IDX_DIM, 2 * D_MODEL, W_SMALL)


def _proj_kernel(x_ref, mod_ref, w_ref, qkv_ref, bq_ref, iq_ref, gates_ref, small_ref):
    m = mod_ref[0]
    u = (x_ref[...] * (1.0 + m[1:2]) + m[0:1]).astype(BF16)
    off = 0
    for o_ref, n in zip((qkv_ref, bq_ref, iq_ref, gates_ref, small_ref), PROJ_COLS):
        o_ref[...] = _dot(u, w_ref[:, off:off + n]).astype(o_ref.dtype)
        off += n


def _projection(x2, mod3, w_cat):
    n_tok, d = x2.shape
    tiles_per_batch = SEQ // PROJ_TM
    out_dtypes = (BF16, BF16, BF16, BF16, F32)
    return pl.pallas_call(
        _proj_kernel,
        out_shape=[jax.ShapeDtypeStruct((n_tok, n), dt) for n, dt in zip(PROJ_COLS, out_dtypes)],
        grid=(n_tok // PROJ_TM,),
        in_specs=[
            pl.BlockSpec((PROJ_TM, d), lambda i: (i, 0)),
            pl.BlockSpec((1, 6, d), lambda i: (i // tiles_per_batch, 0, 0)),
            pl.BlockSpec(w_cat.shape, lambda i: (0, 0)),
        ],
        out_specs=[pl.BlockSpec((PROJ_TM, n), lambda i: (i, 0)) for n in PROJ_COLS],
        compiler_params=_params(("arbitrary",)),
        name="projection",
    )(x2, mod3, w_cat)


A_TQ = 256


def _diff_attn_kernel(q_ref, k_ref, v_ref, lam_ref, g_ref, slope_ref, o_ref, bias_ref):
    slope = slope_ref[0][:, 0:1]
    r = lax.broadcasted_iota(jnp.int32, (A_TQ, SEQ), 0)
    j = lax.broadcasted_iota(jnp.int32, (A_TQ, SEQ), 1)
    dist = jnp.abs(r + (SEQ - A_TQ) - j).astype(F32)
    visible = (j - (SEQ - A_TQ)) // CHUNK <= r // CHUNK
    bias_ref[...] = jnp.where(visible, -slope * dist, NEG_INF)

    lv = lam_ref[...]
    lam = (jnp.exp(jnp.sum(lv[0:1] * lv[1:2], axis=1, keepdims=True))
           - jnp.exp(jnp.sum(lv[2:3] * lv[3:4], axis=1, keepdims=True)) + LAM_INIT)
    lane = lax.broadcasted_iota(jnp.int32, (A_TQ, 2 * A_QK_DIM), 1)
    scale = A_QK_DIM ** -0.5
    for i in range(SEQ // A_TQ):
        q0 = i * A_TQ
        kv = q0 + A_TQ
        q = q_ref[0, q0:q0 + A_TQ, :] * scale
        k = k_ref[0, 0:kv, :]
        bias = bias_ref[:, SEQ - kv:SEQ]
        ps, inv_ls = [], []
        for mth in range(2):
            qm = jnp.where((lane // A_QK_DIM) == mth, q, jnp.zeros_like(q))
            s = _dot_nt(qm, k) + bias
            mx = jnp.max(s, axis=-1, keepdims=True)
            p = jnp.exp(s - mx)
            ps.append(p)
            inv_ls.append(1.0 / jnp.sum(p, axis=-1, keepdims=True))
        attn = ps[0] * inv_ls[0] - ps[1] * (lam * inv_ls[1])
        o = _dot(attn.astype(BF16), v_ref[0, 0:kv, :])
        y = o * lax.rsqrt(jnp.mean(o * o, axis=-1, keepdims=True) + RMS_EPS) * g_ref[...]
        o_ref[0, q0:q0 + A_TQ, :] = (y * (1.0 - LAM_INIT)).astype(o_ref.dtype)


def _diff_attention(qkv3, lam_vecs, subln_g, slopes):
    b, s, _ = qkv3.shape
    blk = (1, s, A_V_DIM)
    return pl.pallas_call(
        _diff_attn_kernel,
        out_shape=jax.ShapeDtypeStruct((b, s, A_WIDTH), BF16),
        grid=(b, A_HEADS),
        in_specs=[
            pl.BlockSpec(blk, lambda bi, h: (bi, 0, h)),
            pl.BlockSpec(blk, lambda bi, h: (bi, 0, A_HEADS + h)),
            pl.BlockSpec(blk, lambda bi, h: (bi, 0, 2 * A_HEADS + h)),
            pl.BlockSpec(lam_vecs.shape, lambda bi, h: (0, 0)),
            pl.BlockSpec((1, A_V_DIM), lambda bi, h: (0, 0)),
            pl.BlockSpec((1, 1, LANES), lambda bi, h: (h, 0, 0)),
        ],
        out_specs=pl.BlockSpec(blk, lambda bi, h: (bi, 0, h)),
        scratch_shapes=[pltpu.VMEM((A_TQ, SEQ), F32)],
        compiler_params=_params(("arbitrary", "arbitrary")),
        name="diff_attention",
    )(qkv3, qkv3, qkv3, lam_vecs, subln_g, slopes)


B_TQ = 128
HEAD_GROUP = 4
SMALL_KV = slice(0, 128)
SMALL_IK = slice(128, 256)
SMALL_IW = slice(256, 384)
BISECT_FIRST = 14
BISECT_MORE = 4
BISECT_MAX_ROUNDS = 80


def _row_count(mask_f32):
    return jnp.sum(mask_f32, axis=1, keepdims=True)


def _topk_mask(score_ref, selb_ref, kv):
    s = score_ref[:, 0:kv]

    def bisect(_, st):
        lo, hi = st
        probe = 0.5 * lo + 0.5 * hi
        ge = _row_count(jnp.where(score_ref[:, 0:kv] >= probe, 1.0, 0.0)) >= TOPK
        return jnp.where(ge, probe, lo), jnp.where(ge, hi, probe)

    def candidate(lo):
        sc = score_ref[:, 0:kv]
        t_val = jnp.min(jnp.where(sc >= lo, sc, jnp.inf), axis=1, keepdims=True)
        n_gt = _row_count(jnp.where(sc > t_val, 1.0, 0.0))
        return t_val, n_gt, jnp.sum(jnp.where(n_gt >= TOPK, 1.0, 0.0))

    lo = jnp.min(jnp.where(s == NEG_INF, jnp.inf, s), axis=1, keepdims=True)
    hi = jnp.max(s, axis=1, keepdims=True)
    lo, hi = lax.fori_loop(0, BISECT_FIRST, bisect, (lo, hi))

    def unresolved(st):
        return (st[5] > 0.0) & (st[0] < BISECT_MAX_ROUNDS)

    def refine(st):
        lo, hi = lax.fori_loop(0, BISECT_MORE, bisect, (st[1], st[2]))
        return (st[0] + 1, lo, hi) + candidate(lo)

    _, _, _, t_val, n_gt, _ = lax.while_loop(unresolved, refine, (jnp.int32(0), lo, hi) + candidate(lo))
    gt = s > t_val
    eqf = jnp.where(s == t_val, 1.0, 0.0)
    need = TOPK - n_gt
    selb_ref[:, 0:kv] = jnp.where(s >= t_val, 0.0, NEG_INF)
    tie_overflow = jnp.max(jnp.abs(_row_count(eqf) - need))

    @pl.when(tie_overflow > 0.0)
    def _():
        idx = lax.broadcasted_iota(jnp.int32, (B_TQ, kv), 1)

        def idx_step(b, j):
            c = j | lax.shift_left(jnp.int32(1), 10 - b)
            before = _row_count(jnp.where(idx < c, eqf, 0.0))
            return jnp.where(before < need, c, j)

        j_max = lax.fori_loop(0, 11, idx_step, jnp.zeros((B_TQ, 1), jnp.int32))
        keep = gt | ((eqf > 0.0) & (idx <= j_max))
        selb_ref[:, 0:kv] = jnp.where(keep, 0.0, NEG_INF)


def _sparse_attn_kernel(bq_ref, iq_ref, small_ref, kvg_ref, wuk_ref, wuv_ref, slope_ref, o_ref,
                        bias_ref, ckv_ref, ik_ref, score_ref, selb_ref):
    @pl.when(pl.program_id(0) == 0)
    def _():
        r = lax.broadcasted_iota(jnp.int32, (B_TQ, SEQ), 0)
        j = lax.broadcasted_iota(jnp.int32, (B_TQ, SEQ), 1)
        dist = jnp.abs(r + (SEQ - B_TQ) - j).astype(F32)
        for h in range(B_HEADS):
            bias_ref[h] = -slope_ref[h][:, 0:1] * dist

    kv_lat = small_ref[0, :, SMALL_KV]
    ckv = kv_lat * lax.rsqrt(jnp.mean(kv_lat * kv_lat, axis=-1, keepdims=True) + RMS_EPS) * kvg_ref[...]
    ckv_ref[...] = ckv.astype(BF16)
    ik_ref[...] = small_ref[0, :, SMALL_IK].astype(BF16)

    lane = lax.broadcasted_iota(jnp.int32, (B_TQ, LANES), 1)
    rr = lax.broadcasted_iota(jnp.int32, (B_TQ, B_TQ), 0)
    cc = lax.broadcasted_iota(jnp.int32, (B_TQ, B_TQ), 1)
    diag_visible = cc // CHUNK <= rr // CHUNK
    scale = B_HEAD_DIM ** -0.5
    n_groups = B_HEADS // HEAD_GROUP

    for i in range(SEQ // B_TQ):
        q0 = i * B_TQ
        kv = q0 + B_TQ
        rows = slice(q0, q0 + B_TQ)
        if kv <= TOPK:
            if q0:
                selb_ref[:, 0:q0] = jnp.zeros((B_TQ, q0), F32)
            selb_ref[:, q0:kv] = jnp.where(diag_visible, 0.0, NEG_INF)
        else:
            iq = iq_ref[0, rows, :]
            iw = small_ref[0, rows, SMALL_IW]
            ik = ik_ref[0:kv, :]
            score = jnp.zeros((B_TQ, kv), F32)
            for g in range(n_groups):
                parts = []
                for hh in range(HEAD_GROUP):
                    h = g * HEAD_GROUP + hh
                    pair = iq[:, (h // 2) * LANES:(h // 2 + 1) * LANES]
                    parts.append(jnp.where((lane // IDX_DIM) == (h % 2), pair, jnp.zeros_like(pair)))
                x = _dot_nt(jnp.concatenate(parts, axis=0), ik)
                for hh in range(HEAD_GROUP):
                    h = g * HEAD_GROUP + hh
                    score = score + iw[:, h:h + 1] * jnp.maximum(x[hh * B_TQ:(hh + 1) * B_TQ], 0.0)
            score_ref[:, 0:q0] = score[:, 0:q0]
            score_ref[:, q0:kv] = jnp.where(diag_visible, score[:, q0:kv], NEG_INF)
            _topk_mask(score_ref, selb_ref, kv)

        bq = bq_ref[0, rows, :]
        ck = ckv_ref[0:kv, :]
        selb = selb_ref[:, 0:kv]
        lat = []
        for g in range(n_groups):
            parts = []
            for hh in range(HEAD_GROUP):
                h = g * HEAD_GROUP + hh
                pair = bq[:, (h // 2) * LANES:(h // 2 + 1) * LANES]
                parts.append((_dot(pair, wuk_ref[h]) * scale).astype(BF16))
            s = _dot_nt(jnp.concatenate(parts, axis=0), ck)
            probs = []
            for hh in range(HEAD_GROUP):
                h = g * HEAD_GROUP + hh
                sh = s[hh * B_TQ:(hh + 1) * B_TQ] + bias_ref[h, :, SEQ - kv:SEQ] + selb
                mx = jnp.max(sh, axis=-1, keepdims=True)
                p = jnp.exp(sh - mx)
                probs.append((p * (1.0 / jnp.sum(p, axis=-1, keepdims=True))).astype(BF16))
            o = _dot(jnp.concatenate(probs, axis=0), ck)
            lat.extend(o[hh * B_TQ:(hh + 1) * B_TQ] for hh in range(HEAD_GROUP))
        outs = []
        for jp in range(B_HEADS // 2):
            pair = jnp.concatenate([lat[2 * jp], lat[2 * jp + 1]], axis=1).astype(BF16)
            outs.append(_dot(pair, wuv_ref[jp]))
        o_ref[0, rows, :] = jnp.concatenate(outs, axis=1).astype(o_ref.dtype)


def _sparse_attention(bq3, iq3, small3, kv_norm_g, wuk_pad, wuv_pair, slopes):
    b, s, _ = bq3.shape
    return pl.pallas_call(
        _sparse_attn_kernel,
        out_shape=jax.ShapeDtypeStruct((b, s, B_WIDTH), BF16),
        grid=(b,),
        in_specs=[
            pl.BlockSpec((1, s, B_WIDTH), lambda bi: (bi, 0, 0)),
            pl.BlockSpec((1, s, IDX_HEADS * IDX_DIM), lambda bi: (bi, 0, 0)),
            pl.BlockSpec((1, s, W_SMALL), lambda bi: (bi, 0, 0)),
            pl.BlockSpec((1, B_KV_RANK), lambda bi: (0, 0)),
            pl.BlockSpec(wuk_pad.shape, lambda bi: (0, 0, 0)),
            pl.BlockSpec(wuv_pair.shape, lambda bi: (0, 0, 0)),
            pl.BlockSpec(slopes.shape, lambda bi: (0, 0, 0)),
        ],
        out_specs=pl.BlockSpec((1, s, B_WIDTH), lambda bi: (bi, 0, 0)),
        scratch_shapes=[
            pltpu.VMEM((B_HEADS, B_TQ, SEQ), F32),
            pltpu.VMEM((SEQ, B_KV_RANK), BF16),
            pltpu.VMEM((SEQ, LANES), BF16),
            pltpu.VMEM((B_TQ, SEQ), F32),
            pltpu.VMEM((B_TQ, SEQ), F32),
        ],
        compiler_params=_params(("arbitrary",)),
        name="sparse_attention",
    )(bq3, iq3, small3, kv_norm_g, wuk_pad, wuv_pair, slopes)


MERGE_TM = 512
ROUTER_ROWS = 32
ROUTE_ROWS = 8


def _first_max_onehot(rows):
    mx = rows[0]
    for r in rows[1:]:
        mx = jnp.maximum(mx, r)
    taken = jnp.zeros_like(mx)
    hot = []
    for r in rows:
        h = jnp.where((r == mx) & (taken == 0.0), 1.0, 0.0)
        taken = taken + h
        hot.append(h)
    return hot, mx


def _softmax_rows(rows):
    mx = rows[0]
    for r in rows[1:]:
        mx = jnp.maximum(mx, r)
    e = [jnp.exp(r - mx) for r in rows]
    tot = e[0]
    for r in e[1:]:
        tot = tot + r
    return [r / tot for r in e]


def _merge_kernel(ya_ref, yb_ref, gates_ref, x_ref, mod_ref, wa_ref, wb_ref, wo_ref, g1_ref, b1_ref,
                  wr_ref, br_ref, tri_ref, x1_ref, u2_ref, route_ref, cnt_ref, run_ref):
    m = mod_ref[0]
    pa = _dot(ya_ref[...], wa_ref[...])
    pb = _dot(yb_ref[...], wb_ref[...])
    gt = jax.nn.sigmoid(gates_ref[...].astype(F32))
    mixed = gt[:, 0:D_MODEL] * pa + gt[:, D_MODEL:2 * D_MODEL] * pb
    z = _dot(mixed.astype(BF16), wo_ref[...])
    x1 = _layer_norm(DEEPNORM_ALPHA * x_ref[...] + m[2:3] * z, g1_ref[...], b1_ref[...])
    x1_ref[...] = x1
    u2 = x1 * (1.0 + m[4:5]) + m[3:4]
    u2_ref[...] = u2

    logits = _dot_nt(wr_ref[...], u2, precision=lax.Precision.HIGHEST) + br_ref[...]
    g_prob = _softmax_rows([logits[k:k + 1] for k in range(N_GROUPS)])
    g_hot, g_top = _first_max_onehot(g_prob)
    e_logit = []
    for jx in range(EXPERTS_PER_GROUP):
        acc = jnp.zeros_like(g_top)
        for g in range(N_GROUPS):
            row = N_GROUPS + g * EXPERTS_PER_GROUP + jx
            acc = acc + logits[row:row + 1] * g_hot[g]
        e_logit.append(acc)
    e_prob = _softmax_rows(e_logit)
    hot1, p1 = _first_max_onehot(e_prob)
    rest = [jnp.where(h > 0.0, NEG_INF, p) for h, p in zip(hot1, e_prob)]
    hot2, p2 = _first_max_onehot(rest)
    tot = p1 + p2
    w1 = g_top * (p1 / tot)
    w2 = g_top * (p2 / tot)

    @pl.when(pl.program_id(0) == 0)
    def _():
        run_ref[...] = jnp.zeros_like(run_ref)

    gid = g_hot[1] + 2.0 * g_hot[2] + 3.0 * g_hot[3]
    grp = lax.broadcasted_iota(jnp.int32, (ROUTE_ROWS, MERGE_TM), 0).astype(F32)
    hot8 = jnp.where(grp == gid, 1.0, 0.0)
    before = _dot(hot8.astype(BF16), tri_ref[...])
    rank = jnp.sum(hot8 * (run_ref[:, 0:1] + before), axis=0, keepdims=True)
    route_ref[0:1, :] = gid
    for jx in range(EXPERTS_PER_GROUP):
        route_ref[1 + jx:2 + jx, :] = w1 * hot1[jx] + w2 * hot2[jx]
    route_ref[5:6, :] = rank
    route_ref[6:8, :] = jnp.zeros((2, MERGE_TM), F32)
    run_ref[...] = run_ref[...] + jnp.sum(hot8, axis=1, keepdims=True)
    cnt_ref[...] = run_ref[...]


def _merge(ya, yb, gates, x2, mod3, wa, wb, wo, ln_g, ln_b, w_router_t, b_router):
    n_tok, d = x2.shape
    tiles_per_batch = SEQ // MERGE_TM
    tok = lambda n: pl.BlockSpec((MERGE_TM, n), lambda i: (i, 0))
    full = lambda a: pl.BlockSpec(a.shape, lambda i: (0,) * a.ndim)
    tri = jnp.asarray(np.triu(np.ones((MERGE_TM, MERGE_TM), np.float32), k=1), BF16)
    return pl.pallas_call(
        _merge_kernel,
        out_shape=[
            jax.ShapeDtypeStruct((n_tok, d), F32),
            jax.ShapeDtypeStruct((n_tok, d), F32),
            jax.ShapeDtypeStruct((ROUTE_ROWS, n_tok), F32),
            jax.ShapeDtypeStruct((ROUTE_ROWS, LANES), F32),
        ],
        grid=(n_tok // MERGE_TM,),
        in_specs=[
            tok(A_WIDTH), tok(B_WIDTH), tok(2 * d), tok(d),
            pl.BlockSpec((1, 6, d), lambda i: (i // tiles_per_batch, 0, 0)),
            full(wa), full(wb), full(wo), full(ln_g), full(ln_b), full(w_router_t), full(b_router), full(tri),
        ],
        out_specs=[tok(d), tok(d), pl.BlockSpec((ROUTE_ROWS, MERGE_TM), lambda i: (0, i)),
                   pl.BlockSpec((ROUTE_ROWS, LANES), lambda i: (0, 0))],
        scratch_shapes=[pltpu.VMEM((ROUTE_ROWS, LANES), F32)],
        compiler_params=_params(("arbitrary",)),
        name="merge_router",
    )(ya, yb, gates, x2, mod3, wa, wb, wo, ln_g, ln_b, w_router_t, b_router, tri)


def _start_row_gather(src_hbm, idx_ref, buf, slot, sem, n_rows):
    def issue(r, carry):
        pltpu.make_async_copy(src_hbm.at[pl.ds(idx_ref[0, 0, r], 1)], buf.at[slot, pl.ds(r, 1)], sem.at[slot]).start()
        return carry

    lax.fori_loop(0, n_rows, issue, 0, unroll=8)


def _wait_row_gather(buf, slot, sem):
    pltpu.make_async_copy(buf.at[slot], buf.at[slot], sem.at[slot]).wait()


MOE_TM = 512


def _moe_kernel(tg_ref, nu_ref, idx_ref, idx_next_ref, u_hbm, comb_ref, wg_ref, wu_ref, wd_ref, y_ref, xbuf, sem):
    j = pl.program_id(0)
    slot = j % 2
    n_used = nu_ref[0]

    @pl.when(j == 0)
    def _():
        _start_row_gather(u_hbm, idx_ref, xbuf, 0, sem, MOE_TM)

    @pl.when(j + 1 < n_used)
    def _():
        _start_row_gather(u_hbm, idx_next_ref, xbuf, 1 - slot, sem, MOE_TM)

    @pl.when(j < n_used)
    def _():
        _wait_row_gather(xbuf, slot, sem)
        u = xbuf[slot].astype(BF16)
        h = jax.nn.silu(_dot(u, wg_ref[0])) * _dot(u, wu_ref[0])
        comb = comb_ref[...]
        parts = [(h[:, jx * D_FF_EXPERT:(jx + 1) * D_FF_EXPERT] * comb[:, jx:jx + 1]).astype(BF16)
                 for jx in range(EXPERTS_PER_GROUP)]
        y_ref[...] = _dot(jnp.concatenate(parts, axis=1), wd_ref[0])

    @pl.when(j >= n_used)
    def _():
        y_ref[...] = jnp.zeros_like(y_ref)


def _moe(tile_group, n_used, src3, u2, comb_sorted, wg, wu, wd):
    n_tiles = src3.shape[0]
    d = u2.shape[1]
    wspec = pl.BlockSpec((1, d, d), lambda j, tg, nu: (tg[j], 0, 0))
    idx_spec = lambda f: pl.BlockSpec((1, 1, MOE_TM), f, memory_space=pltpu.SMEM)
    return pl.pallas_call(
        _moe_kernel,
        out_shape=jax.ShapeDtypeStruct((n_tiles * MOE_TM, d), F32),
        grid_spec=pltpu.PrefetchScalarGridSpec(
            num_scalar_prefetch=2,
            grid=(n_tiles,),
            in_specs=[
                idx_spec(lambda j, tg, nu: (j, 0, 0)),
                idx_spec(lambda j, tg, nu: (jnp.minimum(j + 1, n_tiles - 1), 0, 0)),
                pl.BlockSpec(memory_space=pl.ANY),
                pl.BlockSpec((MOE_TM, EXPERTS_PER_GROUP), lambda j, tg, nu: (j, 0)),
                wspec, wspec, wspec,
            ],
            out_specs=pl.BlockSpec((MOE_TM, d), lambda j, tg, nu: (j, 0)),
            scratch_shapes=[pltpu.VMEM((2, MOE_TM, d), F32), pltpu.SemaphoreType.DMA((2,))],
        ),
        compiler_params=_params(("arbitrary",)),
        name="moe",
    )(tile_group, n_used, src3, src3, u2, comb_sorted, wg, wu, wd)


FINAL_TM = 512


def _final_kernel(idx_ref, idx_next_ref, y_hbm, x1_ref, mod_ref, g2_ref, b2_ref, o_ref, ybuf, sem):
    i = pl.program_id(0)
    slot = i % 2

    @pl.when(i == 0)
    def _():
        _start_row_gather(y_hbm, idx_ref, ybuf, 0, sem, FINAL_TM)

    @pl.when(i + 1 < pl.num_programs(0))
    def _():
        _start_row_gather(y_hbm, idx_next_ref, ybuf, 1 - slot, sem, FINAL_TM)

    _wait_row_gather(ybuf, slot, sem)
    m = mod_ref[0]
    o_ref[...] = _layer_norm(DEEPNORM_ALPHA * x1_ref[...] + m[5:6] * ybuf[slot], g2_ref[...], b2_ref[...])


def _final(dest3, y_sorted, x1, mod3, ln_g, ln_b):
    n_tok, d = x1.shape
    n_tiles = n_tok // FINAL_TM
    tiles_per_batch = SEQ // FINAL_TM
    idx_spec = lambda f: pl.BlockSpec((1, 1, FINAL_TM), f, memory_space=pltpu.SMEM)
    vec = pl.BlockSpec((1, d), lambda i: (0, 0))
    return pl.pallas_call(
        _final_kernel,
        out_shape=jax.ShapeDtypeStruct((n_tok, d), F32),
        grid=(n_tiles,),
        in_specs=[
            idx_spec(lambda i: (i, 0, 0)),
            idx_spec(lambda i: (jnp.minimum(i + 1, n_tiles - 1), 0, 0)),
            pl.BlockSpec(memory_space=pl.ANY),
            pl.BlockSpec((FINAL_TM, d), lambda i: (i, 0)),
            pl.BlockSpec((1, 6, d), lambda i: (i // tiles_per_batch, 0, 0)),
            vec, vec,
        ],
        out_specs=pl.BlockSpec((FINAL_TM, d), lambda i: (i, 0)),
        scratch_shapes=[pltpu.VMEM((2, FINAL_TM, d), F32), pltpu.SemaphoreType.DMA((2,))],
        compiler_params=_params(("arbitrary",)),
        name="combine_norm",
    )(dest3, dest3, y_sorted, x1, mod3, ln_g, ln_b)


def _routing_tables(route_t, cnt):
    n_tok = route_t.shape[1]
    n_tiles = n_tok // MOE_TM + N_GROUPS
    counts = cnt[:N_GROUPS, 0].astype(jnp.int32)
    padded = (counts + MOE_TM - 1) // MOE_TM * MOE_TM
    ends = jnp.cumsum(padded)
    gid = route_t[0].astype(jnp.int32)
    dest = (ends - padded)[gid] + route_t[5].astype(jnp.int32)
    src = jnp.zeros((n_tiles * MOE_TM,), jnp.int32).at[dest].set(jnp.arange(n_tok, dtype=jnp.int32))
    comb_sorted = route_t[1:1 + EXPERTS_PER_GROUP].T[src]
    tile_start = jnp.arange(n_tiles, dtype=jnp.int32) * MOE_TM
    tile_group = jnp.minimum(jnp.sum((tile_start[:, None] >= ends[None, :]).astype(jnp.int32), axis=1), N_GROUPS - 1)
    n_used = (ends[-1:] // MOE_TM).astype(jnp.int32)
    return tile_group, n_used, src.reshape(n_tiles, 1, MOE_TM), comb_sorted, dest.reshape(-1, 1, FINAL_TM)


def _regroup_w_in(w):
    o = np.cumsum((512, 512, 512, 512, 128, 512, 64, 8, 2048))
    a_qkv, b_q, b_kv = w[:, :o[2]], w[:, o[2]:o[3]], w[:, o[3]:o[4]]
    i_q, i_k, i_w, gates = w[:, o[4]:o[5]], w[:, o[5]:o[6]], w[:, o[6]:o[7]], w[:, o[7]:o[8]]
    pad = jnp.zeros((w.shape[0], LANES - IDX_HEADS), w.dtype)
    return jnp.concatenate([a_qkv, b_q, i_q, gates, b_kv, i_k, i_k, i_w, pad], axis=1).astype(BF16)


def _pad_w_uk(w_uk):
    wt = jnp.transpose(w_uk, (1, 2, 0))
    z = jnp.zeros_like(wt)
    even = jnp.concatenate([wt, z], axis=1)
    odd = jnp.concatenate([z, wt], axis=1)
    sel = (jnp.arange(B_HEADS) % 2 == 0)[:, None, None]
    return jnp.where(sel, even, odd).astype(BF16)


def _pair_w_uv(w_uv):
    wv = jnp.transpose(w_uv, (1, 0, 2))
    z = jnp.zeros_like(wv[0::2])
    top = jnp.concatenate([wv[0::2], z], axis=2)
    bot = jnp.concatenate([z, wv[1::2]], axis=2)
    return jnp.concatenate([top, bot], axis=1).astype(BF16)


def _group_experts(w_gate, w_up, w_down):
    d = w_gate.shape[1]
    cat = lambda w: jnp.transpose(w.reshape(N_GROUPS, EXPERTS_PER_GROUP, d, D_FF_EXPERT), (0, 2, 1, 3)).reshape(
        N_GROUPS, d, EXPERTS_PER_GROUP * D_FF_EXPERT).astype(BF16)
    wd = w_down.reshape(N_GROUPS, EXPERTS_PER_GROUP * D_FF_EXPERT, d).astype(BF16)
    return cat(w_gate), cat(w_up), wd


def kernel(x, c, ada_w, ada_b, w_in, lambda_q1, lambda_k1, lambda_q2, lambda_k2, a_subln_g, kv_norm_g, w_uk, w_uv,
           w_a_proj, w_b_proj, w_o, ln1_g, ln1_b, w_group, b_group, w_expert_router, b_expert_router,
           w_exp_gate, w_exp_up, w_exp_down, ln2_g, ln2_b):
    b, s, d = x.shape
    assert (s, d) == (SEQ, D_MODEL) and ada_w.shape[0] == DEPTH
    slopes_a, slopes_b = _alibi_slopes()
    lane_rep = lambda v: jnp.asarray(np.repeat(v[:, None, None], LANES, axis=2))
    x2 = x.reshape(b * s, d)
    l = 0
    mod3 = _modulation(c, ada_w[l], ada_b[l]).reshape(b, 6, d)

    qkv, bq, iq, gates, small = _projection(x2, mod3, _regroup_w_in(w_in[l]))
    lam_vecs = jnp.stack([lambda_q1[l], lambda_k1[l], lambda_q2[l], lambda_k2[l]]).astype(F32)
    y_a = _diff_attention(qkv.reshape(b, s, W_QKV), lam_vecs, a_subln_g[l].reshape(1, A_V_DIM), lane_rep(slopes_a))
    y_b = _sparse_attention(bq.reshape(b, s, -1), iq.reshape(b, s, -1), small.reshape(b, s, -1),
                            kv_norm_g[l].reshape(1, B_KV_RANK), _pad_w_uk(w_uk[l]), _pair_w_uv(w_uv[l]),
                            lane_rep(slopes_b))

    w_router_t = jnp.zeros((ROUTER_ROWS, d), F32).at[:N_GROUPS].set(w_group[l].T).at[
        N_GROUPS:N_GROUPS + N_EXPERTS].set(w_expert_router[l].T)
    b_router = jnp.zeros((ROUTER_ROWS, 1), F32).at[:N_GROUPS, 0].set(b_group[l]).at[
        N_GROUPS:N_GROUPS + N_EXPERTS, 0].set(b_expert_router[l])
    x1, u2, route_t, cnt = _merge(y_a.reshape(b * s, A_WIDTH), y_b.reshape(b * s, B_WIDTH), gates, x2, mod3,
                                  w_a_proj[l].astype(BF16), w_b_proj[l].astype(BF16), w_o[l].astype(BF16),
                                  ln1_g[l].reshape(1, d), ln1_b[l].reshape(1, d), w_router_t, b_router)

    tile_group, n_used, src3, comb_sorted, dest3 = _routing_tables(route_t, cnt)
    wg, wu, wd = _group_experts(w_exp_gate[l], w_exp_up[l], w_exp_down[l])
    y_sorted = _moe(tile_group, n_used, src3, u2, comb_sorted, wg, wu, wd)
    out = _final(dest3, y_sorted, x1, mod3, ln2_g[l].reshape(1, d), ln2_b[l].reshape(1, d))
    return out.reshape(b, s, d)
```

```python
import functools
import math

import numpy as np
import jax
import jax.numpy as jnp
from jax import lax
from jax.experimental import pallas as pl
from jax.experimental.pallas import tpu as pltpu

D_MODEL = 1024
SEQ = 2048
CHUNK = 64
A_QK_DIM = 64
A_V_DIM = 128
A_HEADS = 4
A_WIDTH = A_HEADS * A_V_DIM
B_HEAD_DIM = 64
B_HEADS = 8
B_WIDTH = B_HEADS * B_HEAD_DIM
B_KV_RANK = 128
IDX_HEADS = 8
IDX_DIM = 64
TOPK = 256
N_ALIBI_HEADS = A_HEADS + B_HEADS
N_GROUPS = 4
EXPERTS_PER_GROUP = 4
N_EXPERTS = 16
D_FF_EXPERT = 256
LN_EPS = 1e-5
RMS_EPS = 1e-5
DEPTH = 1
DEEPNORM_ALPHA = (2.0 * DEPTH) ** 0.25
LAM_INIT = 0.8 - 0.6 * math.exp(-0.3 * 0)

LANES = 128
VMEM_LIMIT_BYTES = 56 * 1024 * 1024

F32 = jnp.float32
BF16 = jnp.bfloat16
NEG_INF = float("-inf")

_NT = (((1,), (1,)), ((), ()))


def _dot(a, b, **kw):
    return jnp.dot(a, b, preferred_element_type=F32, **kw)


def _dot_nt(a, b, **kw):
    return lax.dot_general(a, b, _NT, preferred_element_type=F32, **kw)


def _alibi_slopes():
    n = N_ALIBI_HEADS
    slopes = (2.0 ** (-8.0 * np.arange(1, n + 1) / n)).astype(np.float32)
    a_idx = np.arange(A_HEADS) * (n // A_HEADS)
    b_idx = np.setdiff1d(np.arange(n), a_idx)
    return slopes[a_idx], slopes[b_idx]


def _layer_norm(h, g, b):
    mu = jnp.mean(h, axis=-1, keepdims=True)
    d = h - mu
    var = jnp.mean(d * d, axis=-1, keepdims=True)
    return d * lax.rsqrt(var + LN_EPS) * g + b


def _params(sem):
    return pltpu.CompilerParams(dimension_semantics=sem, vmem_limit_bytes=VMEM_LIMIT_BYTES)


MOD_TN = 1536


def _mod_kernel(c_ref, w_ref, b_ref, o_ref):
    c = c_ref[...]
    cond = c * jax.nn.sigmoid(c)
    o_ref[...] = _dot(cond, w_ref[...], precision=lax.Precision.HIGHEST) + b_ref[...]


def _modulation(c, ada_w, ada_b):
    b, d = c.shape
    n = ada_w.shape[1]
    return pl.pallas_call(
        _mod_kernel,
        out_shape=jax.ShapeDtypeStruct((b, n), F32),
        grid=(n // MOD_TN,),
        in_specs=[
            pl.BlockSpec((b, d), lambda j: (0, 0)),
            pl.BlockSpec((d, MOD_TN), lambda j: (0, j)),
            pl.BlockSpec((1, MOD_TN), lambda j: (0, j)),
        ],
        out_specs=pl.BlockSpec((b, MOD_TN), lambda j: (0, j)),
        compiler_params=_params(("arbitrary",)),
        name="modulation",
    )(c, ada_w, ada_b.reshape(1, n))


PROJ_TM = 512
W_QKV = 3 * A_WIDTH
W_SMALL = 3 * LANES
PROJ_COLS = (W_QKV, B_WIDTH, IDX_HEADS * IDX_DIM, 2 * D_MODEL, W_SMALL)


def _proj_kernel(x_ref, mod_ref, w_ref, qkv_ref, bq_ref, iq_ref, gates_ref, small_ref):
    m = mod_ref[0]
    u = (x_ref[...] * (1.0 + m[1:2]) + m[0:1]).astype(BF16)
    off = 0
    for o_ref, n in zip((qkv_ref, bq_ref, iq_ref, gates_ref, small_ref), PROJ_COLS):
        o_ref[...] = _dot(u, w_ref[:, off:off + n]).astype(o_ref.dtype)
        off += n


def _projection(x2, mod3, w_cat):
    n_tok, d = x2.shape
    tiles_per_batch = SEQ // PROJ_TM
    out_dtypes = (BF16, BF16, BF16, BF16, F32)
    return pl.pallas_call(
        _proj_kernel,
        out_shape=[jax.ShapeDtypeStruct((n_tok, n), dt) for n, dt in zip(PROJ_COLS, out_dtypes)],
        grid=(n_tok // PROJ_TM,),
        in_specs=[
            pl.BlockSpec((PROJ_TM, d), lambda i: (i, 0)),
            pl.BlockSpec((1, 6, d), lambda i: (i // tiles_per_batch, 0, 0)),
            pl.BlockSpec(w_cat.shape, lambda i: (0, 0)),
        ],
        out_specs=[pl.BlockSpec((PROJ_TM, n), lambda i: (i, 0)) for n in PROJ_COLS],
        compiler_params=_params(("arbitrary",)),
        name="projection",
    )(x2, mod3, w_cat)


A_TQ = 256


def _diff_attn_kernel(q_ref, k_ref, v_ref, lam_ref, g_ref, slope_ref, o_ref, bias_ref):
    slope = slope_ref[0][:, 0:1]
    r = lax.broadcasted_iota(jnp.int32, (A_TQ, SEQ), 0)
    j = lax.broadcasted_iota(jnp.int32, (A_TQ, SEQ), 1)
    dist = jnp.abs(r + (SEQ - A_TQ) - j).astype(F32)
    visible = (j - (SEQ - A_TQ)) // CHUNK <= r // CHUNK
    bias_ref[...] = jnp.where(visible, -slope * dist, NEG_INF)

    lv = lam_ref[...]
    lam = (jnp.exp(jnp.sum(lv[0:1] * lv[1:2], axis=1, keepdims=True))
           - jnp.exp(jnp.sum(lv[2:3] * lv[3:4], axis=1, keepdims=True)) + LAM_INIT)
    lane = lax.broadcasted_iota(jnp.int32, (A_TQ, 2 * A_QK_DIM), 1)
    scale = A_QK_DIM ** -0.5
    for i in range(SEQ // A_TQ):
        q0 = i * A_TQ
        kv = q0 + A_TQ
        q = q_ref[0, q0:q0 + A_TQ, :] * scale
        k = k_ref[0, 0:kv, :]
        bias = bias_ref[:, SEQ - kv:SEQ]
        ps, inv_ls = [], []
        for mth in range(2):
            qm = jnp.where((lane // A_QK_DIM) == mth, q, jnp.zeros_like(q))
            s = _dot_nt(qm, k) + bias
            mx = jnp.max(s, axis=-1, keepdims=True)
            p = jnp.exp(s - mx)
            ps.append(p)
            inv_ls.append(1.0 / jnp.sum(p, axis=-1, keepdims=True))
        attn = ps[0] * inv_ls[0] - ps[1] * (lam * inv_ls[1])
        o = _dot(attn.astype(BF16), v_ref[0, 0:kv, :])
        y = o * lax.rsqrt(jnp.mean(o * o, axis=-1, keepdims=True) + RMS_EPS) * g_ref[...]
        o_ref[0, q0:q0 + A_TQ, :] = (y * (1.0 - LAM_INIT)).astype(o_ref.dtype)


def _diff_attention(qkv3, lam_vecs, subln_g, slopes):
    b, s, _ = qkv3.shape
    blk = (1, s, A_V_DIM)
    return pl.pallas_call(
        _diff_attn_kernel,
        out_shape=jax.ShapeDtypeStruct((b, s, A_WIDTH), BF16),
        grid=(b, A_HEADS),
        in_specs=[
            pl.BlockSpec(blk, lambda bi, h: (bi, 0, h)),
            pl.BlockSpec(blk, lambda bi, h: (bi, 0, A_HEADS + h)),
            pl.BlockSpec(blk, lambda bi, h: (bi, 0, 2 * A_HEADS + h)),
            pl.BlockSpec(lam_vecs.shape, lambda bi, h: (0, 0)),
            pl.BlockSpec((1, A_V_DIM), lambda bi, h: (0, 0)),
            pl.BlockSpec((1, 1, LANES), lambda bi, h: (h, 0, 0)),
        ],
        out_specs=pl.BlockSpec(blk, lambda bi, h: (bi, 0, h)),
        scratch_shapes=[pltpu.VMEM((A_TQ, SEQ), F32)],
        compiler_params=_params(("arbitrary", "arbitrary")),
        name="diff_attention",
    )(qkv3, qkv3, qkv3, lam_vecs, subln_g, slopes)


B_TQ = 128
SMALL_KV = slice(0, 128)
SMALL_IK = slice(128, 256)
SMALL_IW = slice(256, 384)
BISECT_FIRST = 14
BISECT_MORE = 4
BISECT_MAX_ROUNDS = 80


KEY_SLAB = 64


def _over_keys(op, x):
    part = op(x.reshape(x.shape[0] // KEY_SLAB, KEY_SLAB, x.shape[1]), axis=0)
    return op(part, axis=0, keepdims=True)


def _key_count(mask_f32):
    return _over_keys(jnp.sum, mask_f32)


def _topk_mask(score_ref, selb_ref, kv):
    s = score_ref[0:kv, :]

    def bisect(_, st):
        lo, hi = st
        probe = 0.5 * lo + 0.5 * hi
        ge = _key_count(jnp.where(score_ref[0:kv, :] >= probe, 1.0, 0.0)) >= TOPK
        return jnp.where(ge, probe, lo), jnp.where(ge, hi, probe)

    def candidate(lo):
        sc = score_ref[0:kv, :]
        t_val = _over_keys(jnp.min, jnp.where(sc >= lo, sc, jnp.inf))
        n_gt = _key_count(jnp.where(sc > t_val, 1.0, 0.0))
        return t_val, n_gt, jnp.sum(jnp.where(n_gt >= TOPK, 1.0, 0.0))

    lo = _over_keys(jnp.min, jnp.where(s == NEG_INF, jnp.inf, s))
    hi = _over_keys(jnp.max, s)
    lo, hi = lax.fori_loop(0, BISECT_FIRST, bisect, (lo, hi))

    def unresolved(st):
        return (st[5] > 0.0) & (st[0] < BISECT_MAX_ROUNDS)

    def refine(st):
        lo, hi = lax.fori_loop(0, BISECT_MORE, bisect, (st[1], st[2]))
        return (st[0] + 1, lo, hi) + candidate(lo)

    _, _, _, t_val, n_gt, _ = lax.while_loop(unresolved, refine, (jnp.int32(0), lo, hi) + candidate(lo))
    gt = s > t_val
    eqf = jnp.where(s == t_val, 1.0, 0.0)
    need = TOPK - n_gt
    selb_ref[0:kv, :] = jnp.where(s >= t_val, 0.0, NEG_INF)
    tie_overflow = jnp.max(jnp.abs(_key_count(eqf) - need))

    @pl.when(tie_overflow > 0.0)
    def _():
        idx = lax.broadcasted_iota(jnp.int32, (kv, B_TQ), 0)

        def idx_step(b, j):
            c = j | lax.shift_left(jnp.int32(1), 10 - b)
            before = _key_count(jnp.where(idx < c, eqf, 0.0))
            return jnp.where(before < need, c, j)

        j_max = lax.fori_loop(0, 11, idx_step, jnp.zeros((1, B_TQ), jnp.int32))
        keep = gt | ((eqf > 0.0) & (idx <= j_max))
        selb_ref[0:kv, :] = jnp.where(keep, 0.0, NEG_INF)


def _sparse_attn_kernel(bq_ref, iq_ref, small_ref, kvg_ref, wuk_ref, wuvt_ref, slope_ref, o_ref,
                        bias_ref, ckv_ref, ckvt_ref, ik_ref, score_ref, selb_ref):
    @pl.when(pl.program_id(0) == 0)
    def _():
        j = lax.broadcasted_iota(jnp.int32, (SEQ, B_TQ), 0)
        r = lax.broadcasted_iota(jnp.int32, (SEQ, B_TQ), 1)
        dist = jnp.abs(r + (SEQ - B_TQ) - j).astype(F32)
        for h in range(B_HEADS):
            bias_ref[h] = -slope_ref[h][:, 0:1] * dist

    kv_lat = small_ref[0, :, SMALL_KV]
    ckv = kv_lat * lax.rsqrt(jnp.mean(kv_lat * kv_lat, axis=-1, keepdims=True) + RMS_EPS) * kvg_ref[...]
    ckv_ref[...] = ckv.astype(BF16)
    ckvt_ref[...] = ckv.T.astype(BF16)
    ik_ref[...] = small_ref[0, :, SMALL_IK].astype(BF16)

    lane = lax.broadcasted_iota(jnp.int32, (B_TQ, LANES), 1)
    kk = lax.broadcasted_iota(jnp.int32, (B_TQ, B_TQ), 0)
    qq = lax.broadcasted_iota(jnp.int32, (B_TQ, B_TQ), 1)
    diag_visible = kk // CHUNK <= qq // CHUNK
    scale = B_HEAD_DIM ** -0.5
    n_pairs = B_HEADS // 2

    def half_masked(pair):
        z = jnp.zeros_like(pair)
        return jnp.concatenate([jnp.where(lane < IDX_DIM, pair, z), jnp.where(lane >= IDX_DIM, pair, z)], axis=0)

    for i in range(SEQ // B_TQ):
        q0 = i * B_TQ
        kv = q0 + B_TQ
        rows = slice(q0, q0 + B_TQ)
        if kv <= TOPK:
            if q0:
                selb_ref[0:q0, :] = jnp.zeros((q0, B_TQ), F32)
            selb_ref[q0:kv, :] = jnp.where(diag_visible, 0.0, NEG_INF)
        else:
            iq = iq_ref[0, rows, :]
            iw_t = small_ref[0, rows, SMALL_IW].T
            ik = ik_ref[0:kv, :]
            score = jnp.zeros((kv, B_TQ), F32)
            for jp in range(n_pairs):
                x = _dot_nt(ik, half_masked(iq[:, jp * LANES:(jp + 1) * LANES]))
                score = (score + iw_t[2 * jp:2 * jp + 1] * jnp.maximum(x[:, 0:B_TQ], 0.0)
                         + iw_t[2 * jp + 1:2 * jp + 2] * jnp.maximum(x[:, B_TQ:2 * B_TQ], 0.0))
            score_ref[0:q0, :] = score[0:q0]
            score_ref[q0:kv, :] = jnp.where(diag_visible, score[q0:kv], NEG_INF)
            _topk_mask(score_ref, selb_ref, kv)

        bq = bq_ref[0, rows, :]
        q_abs = [(_dot(bq[:, (h // 2) * LANES:(h // 2 + 1) * LANES], wuk_ref[h]) * scale).astype(BF16)
                 for h in range(B_HEADS)]
        s_all = _dot_nt(ckv_ref[0:kv, :], jnp.concatenate(q_abs, axis=0))
        selb = selb_ref[0:kv, :]
        probs, inv_l = [], []
        for h in range(B_HEADS):
            sh = s_all[:, h * B_TQ:(h + 1) * B_TQ] + bias_ref[h, SEQ - kv:SEQ, :] + selb
            p = jnp.exp(sh - _over_keys(jnp.max, sh))
            inv_l.append(1.0 / _over_keys(jnp.sum, p))
            probs.append(p.astype(BF16))
        lat_t = _dot(ckvt_ref[:, 0:kv], jnp.concatenate(probs, axis=1))
        outs = []
        for jp in range(n_pairs):
            pair = jnp.concatenate([lat_t[:, h * B_TQ:(h + 1) * B_TQ] * inv_l[h] for h in (2 * jp, 2 * jp + 1)],
                                   axis=0).astype(BF16)
            outs.append(_dot(wuvt_ref[jp], pair))
        o_ref[0, rows, :] = jnp.concatenate(outs, axis=0).T.astype(o_ref.dtype)


def _sparse_attention(bq3, iq3, small3, kv_norm_g, wuk_pad, wuvt_pair, slopes):
    b, s, _ = bq3.shape
    return pl.pallas_call(
        _sparse_attn_kernel,
        out_shape=jax.ShapeDtypeStruct((b, s, B_WIDTH), BF16),
        grid=(b,),
        in_specs=[
            pl.BlockSpec((1, s, B_WIDTH), lambda bi: (bi, 0, 0)),
            pl.BlockSpec((1, s, IDX_HEADS * IDX_DIM), lambda bi: (bi, 0, 0)),
            pl.BlockSpec((1, s, W_SMALL), lambda bi: (bi, 0, 0)),
            pl.BlockSpec((1, B_KV_RANK), lambda bi: (0, 0)),
            pl.BlockSpec(wuk_pad.shape, lambda bi: (0, 0, 0)),
            pl.BlockSpec(wuvt_pair.shape, lambda bi: (0, 0, 0)),
            pl.BlockSpec(slopes.shape, lambda bi: (0, 0, 0)),
        ],
        out_specs=pl.BlockSpec((1, s, B_WIDTH), lambda bi: (bi, 0, 0)),
        scratch_shapes=[
            pltpu.VMEM((B_HEADS, SEQ, B_TQ), F32),
            pltpu.VMEM((SEQ, B_KV_RANK), BF16),
            pltpu.VMEM((B_KV_RANK, SEQ), BF16),
            pltpu.VMEM((SEQ, LANES), BF16),
            pltpu.VMEM((SEQ, B_TQ), F32),
            pltpu.VMEM((SEQ, B_TQ), F32),
        ],
        compiler_params=_params(("arbitrary",)),
        name="sparse_attention",
    )(bq3, iq3, small3, kv_norm_g, wuk_pad, wuvt_pair, slopes)


MERGE_TM = 512
ROUTER_ROWS = 32
ROUTE_ROWS = 8


def _first_max_onehot(rows):
    mx = rows[0]
    for r in rows[1:]:
        mx = jnp.maximum(mx, r)
    taken = jnp.zeros_like(mx)
    hot = []
    for r in rows:
        h = jnp.where((r == mx) & (taken == 0.0), 1.0, 0.0)
        taken = taken + h
        hot.append(h)
    return hot, mx


def _softmax_rows(rows):
    mx = rows[0]
    for r in rows[1:]:
        mx = jnp.maximum(mx, r)
    e = [jnp.exp(r - mx) for r in rows]
    tot = e[0]
    for r in e[1:]:
        tot = tot + r
    return [r / tot for r in e]


def _merge_kernel(ya_ref, yb_ref, gates_ref, x_ref, mod_ref, wa_ref, wb_ref, wo_ref, g1_ref, b1_ref,
                  wr_ref, br_ref, tri_ref, x1_ref, u2_ref, route_ref, cnt_ref, run_ref):
    m = mod_ref[0]
    pa = _dot(ya_ref[...], wa_ref[...])
    pb = _dot(yb_ref[...], wb_ref[...])
    gt = jax.nn.sigmoid(gates_ref[...].astype(F32))
    mixed = gt[:, 0:D_MODEL] * pa + gt[:, D_MODEL:2 * D_MODEL] * pb
    z = _dot(mixed.astype(BF16), wo_ref[...])
    x1 = _layer_norm(DEEPNORM_ALPHA * x_ref[...] + m[2:3] * z, g1_ref[...], b1_ref[...])
    x1_ref[...] = x1
    u2 = x1 * (1.0 + m[4:5]) + m[3:4]
    u2_ref[...] = u2

    logits = _dot_nt(wr_ref[...], u2, precision=lax.Precision.HIGHEST) + br_ref[...]
    g_prob = _softmax_rows([logits[k:k + 1] for k in range(N_GROUPS)])
    g_hot, g_top = _first_max_onehot(g_prob)
    e_logit = []
    for jx in range(EXPERTS_PER_GROUP):
        acc = jnp.zeros_like(g_top)
        for g in range(N_GROUPS):
            row = N_GROUPS + g * EXPERTS_PER_GROUP + jx
            acc = acc + logits[row:row + 1] * g_hot[g]
        e_logit.append(acc)
    e_prob = _softmax_rows(e_logit)
    hot1, p1 = _first_max_onehot(e_prob)
    rest = [jnp.where(h > 0.0, NEG_INF, p) for h, p in zip(hot1, e_prob)]
    hot2, p2 = _first_max_onehot(rest)
    tot = p1 + p2
    w1 = g_top * (p1 / tot)
    w2 = g_top * (p2 / tot)

    @pl.when(pl.program_id(0) == 0)
    def _():
        run_ref[...] = jnp.zeros_like(run_ref)

    gid = g_hot[1] + 2.0 * g_hot[2] + 3.0 * g_hot[3]
    grp = lax.broadcasted_iota(jnp.int32, (ROUTE_ROWS, MERGE_TM), 0).astype(F32)
    hot8 = jnp.where(grp == gid, 1.0, 0.0)
    before = _dot(hot8.astype(BF16), tri_ref[...])
    rank = jnp.sum(hot8 * (run_ref[:, 0:1] + before), axis=0, keepdims=True)
    route_ref[0:1, :] = gid
    for jx in range(EXPERTS_PER_GROUP):
        route_ref[1 + jx:2 + jx, :] = w1 * hot1[jx] + w2 * hot2[jx]
    route_ref[5:6, :] = rank
    route_ref[6:8, :] = jnp.zeros((2, MERGE_TM), F32)
    run_ref[...] = run_ref[...] + jnp.sum(hot8, axis=1, keepdims=True)
    cnt_ref[...] = run_ref[...]


def _merge(ya, yb, gates, x2, mod3, wa, wb, wo, ln_g, ln_b, w_router_t, b_router):
    n_tok, d = x2.shape
    tiles_per_batch = SEQ // MERGE_TM
    tok = lambda n: pl.BlockSpec((MERGE_TM, n), lambda i: (i, 0))
    full = lambda a: pl.BlockSpec(a.shape, lambda i: (0,) * a.ndim)
    tri = jnp.asarray(np.triu(np.ones((MERGE_TM, MERGE_TM), np.float32), k=1), BF16)
    return pl.pallas_call(
        _merge_kernel,
        out_shape=[
            jax.ShapeDtypeStruct((n_tok, d), F32),
            jax.ShapeDtypeStruct((n_tok, d), F32),
            jax.ShapeDtypeStruct((ROUTE_ROWS, n_tok), F32),
            jax.ShapeDtypeStruct((ROUTE_ROWS, LANES), F32),
        ],
        grid=(n_tok // MERGE_TM,),
        in_specs=[
            tok(A_WIDTH), tok(B_WIDTH), tok(2 * d), tok(d),
            pl.BlockSpec((1, 6, d), lambda i: (i // tiles_per_batch, 0, 0)),
            full(wa), full(wb), full(wo), full(ln_g), full(ln_b), full(w_router_t), full(b_router), full(tri),
        ],
        out_specs=[tok(d), tok(d), pl.BlockSpec((ROUTE_ROWS, MERGE_TM), lambda i: (0, i)),
                   pl.BlockSpec((ROUTE_ROWS, LANES), lambda i: (0, 0))],
        scratch_shapes=[pltpu.VMEM((ROUTE_ROWS, LANES), F32)],
        compiler_params=_params(("arbitrary",)),
        name="merge_router",
    )(ya, yb, gates, x2, mod3, wa, wb, wo, ln_g, ln_b, w_router_t, b_router, tri)


def _start_row_gather(src_hbm, idx_ref, buf, slot, sem, n_rows):
    def issue(r, carry):
        pltpu.make_async_copy(src_hbm.at[pl.ds(idx_ref[0, 0, r], 1)], buf.at[slot, pl.ds(r, 1)], sem.at[slot]).start()
        return carry

    lax.fori_loop(0, n_rows, issue, 0, unroll=8)


def _wait_row_gather(buf, slot, sem):
    pltpu.make_async_copy(buf.at[slot], buf.at[slot], sem.at[slot]).wait()


MOE_TM = 512


def _moe_kernel(tg_ref, nu_ref, idx_ref, idx_next_ref, u_hbm, comb_ref, wg_ref, wu_ref, wd_ref, y_ref, xbuf, sem):
    j = pl.program_id(0)
    slot = j % 2
    n_used = nu_ref[0]

    @pl.when((j == 0) & (n_used > 0))
    def _():
        _start_row_gather(u_hbm, idx_ref, xbuf, 0, sem, MOE_TM)

    @pl.when(j + 1 < n_used)
    def _():
        _start_row_gather(u_hbm, idx_next_ref, xbuf, 1 - slot, sem, MOE_TM)

    @pl.when(j < n_used)
    def _():
        _wait_row_gather(xbuf, slot, sem)
        u = xbuf[slot].astype(BF16)
        h = jax.nn.silu(_dot(u, wg_ref[0])) * _dot(u, wu_ref[0])
        comb = comb_ref[...]
        parts = [(h[:, jx * D_FF_EXPERT:(jx + 1) * D_FF_EXPERT] * comb[:, jx:jx + 1]).astype(BF16)
                 for jx in range(EXPERTS_PER_GROUP)]
        y_ref[...] = _dot(jnp.concatenate(parts, axis=1), wd_ref[0])

    @pl.when(j >= n_used)
    def _():
        y_ref[...] = jnp.zeros_like(y_ref)


def _moe(tile_group, n_used, src3, u2, comb_sorted, wg, wu, wd):
    n_tiles = src3.shape[0]
    d = u2.shape[1]
    wspec = pl.BlockSpec((1, d, d), lambda j, tg, nu: (tg[j], 0, 0))
    idx_spec = lambda f: pl.BlockSpec((1, 1, MOE_TM), f, memory_space=pltpu.SMEM)
    return pl.pallas_call(
        _moe_kernel,
        out_shape=jax.ShapeDtypeStruct((n_tiles * MOE_TM, d), F32),
        grid_spec=pltpu.PrefetchScalarGridSpec(
            num_scalar_prefetch=2,
            grid=(n_tiles,),
            in_specs=[
                idx_spec(lambda j, tg, nu: (j, 0, 0)),
                idx_spec(lambda j, tg, nu: (jnp.minimum(j + 1, n_tiles - 1), 0, 0)),
                pl.BlockSpec(memory_space=pl.ANY),
                pl.BlockSpec((MOE_TM, EXPERTS_PER_GROUP), lambda j, tg, nu: (j, 0)),
                wspec, wspec, wspec,
            ],
            out_specs=pl.BlockSpec((MOE_TM, d), lambda j, tg, nu: (j, 0)),
            scratch_shapes=[pltpu.VMEM((2, MOE_TM, d), F32), pltpu.SemaphoreType.DMA((2,))],
        ),
        compiler_params=_params(("arbitrary",)),
        name="moe",
    )(tile_group, n_used, src3, src3, u2, comb_sorted, wg, wu, wd)


FINAL_TM = 512


def _final_kernel(idx_ref, idx_next_ref, y_hbm, x1_ref, mod_ref, g2_ref, b2_ref, o_ref, ybuf, sem):
    i = pl.program_id(0)
    slot = i % 2

    @pl.when(i == 0)
    def _():
        _start_row_gather(y_hbm, idx_ref, ybuf, 0, sem, FINAL_TM)

    @pl.when(i + 1 < pl.num_programs(0))
    def _():
        _start_row_gather(y_hbm, idx_next_ref, ybuf, 1 - slot, sem, FINAL_TM)

    _wait_row_gather(ybuf, slot, sem)
    m = mod_ref[0]
    o_ref[...] = _layer_norm(DEEPNORM_ALPHA * x1_ref[...] + m[5:6] * ybuf[slot], g2_ref[...], b2_ref[...])


def _final(dest3, y_sorted, x1, mod3, ln_g, ln_b):
    n_tok, d = x1.shape
    n_tiles = n_tok // FINAL_TM
    tiles_per_batch = SEQ // FINAL_TM
    idx_spec = lambda f: pl.BlockSpec((1, 1, FINAL_TM), f, memory_space=pltpu.SMEM)
    vec = pl.BlockSpec((1, d), lambda i: (0, 0))
    return pl.pallas_call(
        _final_kernel,
        out_shape=jax.ShapeDtypeStruct((n_tok, d), F32),
        grid=(n_tiles,),
        in_specs=[
            idx_spec(lambda i: (i, 0, 0)),
            idx_spec(lambda i: (jnp.minimum(i + 1, n_tiles - 1), 0, 0)),
            pl.BlockSpec(memory_space=pl.ANY),
            pl.BlockSpec((FINAL_TM, d), lambda i: (i, 0)),
            pl.BlockSpec((1, 6, d), lambda i: (i // tiles_per_batch, 0, 0)),
            vec, vec,
        ],
        out_specs=pl.BlockSpec((FINAL_TM, d), lambda i: (i, 0)),
        scratch_shapes=[pltpu.VMEM((2, FINAL_TM, d), F32), pltpu.SemaphoreType.DMA((2,))],
        compiler_params=_params(("arbitrary",)),
        name="combine_norm",
    )(dest3, dest3, y_sorted, x1, mod3, ln_g, ln_b)


def _routing_tables(route_t, cnt):
    n_tok = route_t.shape[1]
    n_tiles = n_tok // MOE_TM + N_GROUPS
    counts = cnt[:N_GROUPS, 0].astype(jnp.int32)
    padded = (counts + MOE_TM - 1) // MOE_TM * MOE_TM
    ends = jnp.cumsum(padded)
    gid = route_t[0].astype(jnp.int32)
    dest = (ends - padded)[gid] + route_t[5].astype(jnp.int32)
    src = jnp.zeros((n_tiles * MOE_TM,), jnp.int32).at[dest].set(jnp.arange(n_tok, dtype=jnp.int32))
    comb_sorted = route_t[1:1 + EXPERTS_PER_GROUP].T[src]
    tile_start = jnp.arange(n_tiles, dtype=jnp.int32) * MOE_TM
    tile_group = jnp.minimum(jnp.sum((tile_start[:, None] >= ends[None, :]).astype(jnp.int32), axis=1), N_GROUPS - 1)
    n_used = (ends[-1:] // MOE_TM).astype(jnp.int32)
    return tile_group, n_used, src.reshape(n_tiles, 1, MOE_TM), comb_sorted, dest.reshape(-1, 1, FINAL_TM)


def _regroup_w_in(w):
    o = np.cumsum((512, 512, 512, 512, 128, 512, 64, 8, 2048))
    a_qkv, b_q, b_kv = w[:, :o[2]], w[:, o[2]:o[3]], w[:, o[3]:o[4]]
    i_q, i_k, i_w, gates = w[:, o[4]:o[5]], w[:, o[5]:o[6]], w[:, o[6]:o[7]], w[:, o[7]:o[8]]
    pad = jnp.zeros((w.shape[0], LANES - IDX_HEADS), w.dtype)
    return jnp.concatenate([a_qkv, b_q, i_q, gates, b_kv, i_k, i_k, i_w, pad], axis=1).astype(BF16)


def _pad_w_uk(w_uk):
    wt = jnp.transpose(w_uk, (1, 2, 0))
    z = jnp.zeros_like(wt)
    even = jnp.concatenate([wt, z], axis=1)
    odd = jnp.concatenate([z, wt], axis=1)
    sel = (jnp.arange(B_HEADS) % 2 == 0)[:, None, None]
    return jnp.where(sel, even, odd).astype(BF16)


def _pair_w_uv(w_uv):
    wv = jnp.transpose(w_uv, (1, 0, 2))
    z = jnp.zeros_like(wv[0::2])
    top = jnp.concatenate([wv[0::2], z], axis=2)
    bot = jnp.concatenate([z, wv[1::2]], axis=2)
    return jnp.transpose(jnp.concatenate([top, bot], axis=1), (0, 2, 1)).astype(BF16)


def _group_experts(w_gate, w_up, w_down):
    d = w_gate.shape[1]
    cat = lambda w: jnp.transpose(w.reshape(N_GROUPS, EXPERTS_PER_GROUP, d, D_FF_EXPERT), (0, 2, 1, 3)).reshape(
        N_GROUPS, d, EXPERTS_PER_GROUP * D_FF_EXPERT).astype(BF16)
    wd = w_down.reshape(N_GROUPS, EXPERTS_PER_GROUP * D_FF_EXPERT, d).astype(BF16)
    return cat(w_gate), cat(w_up), wd


def kernel(x, c, ada_w, ada_b, w_in, lambda_q1, lambda_k1, lambda_q2, lambda_k2, a_subln_g, kv_norm_g, w_uk, w_uv,
           w_a_proj, w_b_proj, w_o, ln1_g, ln1_b, w_group, b_group, w_expert_router, b_expert_router,
           w_exp_gate, w_exp_up, w_exp_down, ln2_g, ln2_b):
    b, s, d = x.shape
    assert (s, d) == (SEQ, D_MODEL) and ada_w.shape[0] == DEPTH
    slopes_a, slopes_b = _alibi_slopes()
    lane_rep = lambda v: jnp.asarray(np.repeat(v[:, None, None], LANES, axis=2))
    x2 = x.reshape(b * s, d)
    l = 0
    mod3 = _modulation(c, ada_w[l], ada_b[l]).reshape(b, 6, d)

    qkv, bq, iq, gates, small = _projection(x2, mod3, _regroup_w_in(w_in[l]))
    lam_vecs = jnp.stack([lambda_q1[l], lambda_k1[l], lambda_q2[l], lambda_k2[l]]).astype(F32)
    y_a = _diff_attention(qkv.reshape(b, s, W_QKV), lam_vecs, a_subln_g[l].reshape(1, A_V_DIM), lane_rep(slopes_a))
    y_b = _sparse_attention(bq.reshape(b, s, -1), iq.reshape(b, s, -1), small.reshape(b, s, -1),
                            kv_norm_g[l].reshape(1, B_KV_RANK), _pad_w_uk(w_uk[l]), _pair_w_uv(w_uv[l]),
                            lane_rep(slopes_b))

    w_router_t = jnp.zeros((ROUTER_ROWS, d), F32).at[:N_GROUPS].set(w_group[l].T).at[
        N_GROUPS:N_GROUPS + N_EXPERTS].set(w_expert_router[l].T)
    b_router = jnp.zeros((ROUTER_ROWS, 1), F32).at[:N_GROUPS, 0].set(b_group[l]).at[
        N_GROUPS:N_GROUPS + N_EXPERTS, 0].set(b_expert_router[l])
    x1, u2, route_t, cnt = _merge(y_a.reshape(b * s, A_WIDTH), y_b.reshape(b * s, B_WIDTH), gates, x2, mod3,
                                  w_a_proj[l].astype(BF16), w_b_proj[l].astype(BF16), w_o[l].astype(BF16),
                                  ln1_g[l].reshape(1, d), ln1_b[l].reshape(1, d), w_router_t, b_router)

    tile_group, n_used, src3, comb_sorted, dest3 = _routing_tables(route_t, cnt)
    wg, wu, wd = _group_experts(w_exp_gate[l], w_exp_up[l], w_exp_down[l])
    y_sorted = _moe(tile_group, n_used, src3, u2, comb_sorted, wg, wu, wd)
    out = _final(dest3, y_sorted, x1, mod3, ln2_g[l].reshape(1, d), ln2_b[l].reshape(1, d))
    return out.reshape(b, s, d)
```

```python
import functools
import math

import numpy as np
import jax
import jax.numpy as jnp
from jax import lax
from jax.experimental import pallas as pl
from jax.experimental.pallas import tpu as pltpu

D_MODEL = 1024
SEQ = 2048
CHUNK = 64
A_QK_DIM = 64
A_V_DIM = 128
A_HEADS = 4
A_WIDTH = A_HEADS * A_V_DIM
B_HEAD_DIM = 64
B_HEADS = 8
B_WIDTH = B_HEADS * B_HEAD_DIM
B_KV_RANK = 128
IDX_HEADS = 8
IDX_DIM = 64
TOPK = 256
N_ALIBI_HEADS = A_HEADS + B_HEADS
N_GROUPS = 4
EXPERTS_PER_GROUP = 4
N_EXPERTS = 16
D_FF_EXPERT = 256
LN_EPS = 1e-5
RMS_EPS = 1e-5
DEPTH = 1
DEEPNORM_ALPHA = (2.0 * DEPTH) ** 0.25
LAM_INIT = 0.8 - 0.6 * math.exp(-0.3 * 0)

LANES = 128
VMEM_LIMIT_BYTES = 56 * 1024 * 1024

F32 = jnp.float32
BF16 = jnp.bfloat16
NEG_INF = float("-inf")

_NT = (((1,), (1,)), ((), ()))


def _dot(a, b, **kw):
    return jnp.dot(a, b, preferred_element_type=F32, **kw)


def _dot_nt(a, b, **kw):
    return lax.dot_general(a, b, _NT, preferred_element_type=F32, **kw)


def _alibi_slopes():
    n = N_ALIBI_HEADS
    slopes = (2.0 ** (-8.0 * np.arange(1, n + 1) / n)).astype(np.float32)
    a_idx = np.arange(A_HEADS) * (n // A_HEADS)
    b_idx = np.setdiff1d(np.arange(n), a_idx)
    return slopes[a_idx], slopes[b_idx]


def _layer_norm(h, g, b):
    mu = jnp.mean(h, axis=-1, keepdims=True)
    d = h - mu
    var = jnp.mean(d * d, axis=-1, keepdims=True)
    return d * lax.rsqrt(var + LN_EPS) * g + b


KEY_SLAB = 64


def _over_keys(op, x):
    part = op(x.reshape(x.shape[0] // KEY_SLAB, KEY_SLAB, x.shape[1]), axis=0)
    return op(part, axis=0, keepdims=True)


def _params(sem):
    return pltpu.CompilerParams(dimension_semantics=sem, vmem_limit_bytes=VMEM_LIMIT_BYTES)


MOD_TN = 1536


def _mod_kernel(c_ref, w_ref, b_ref, o_ref):
    c = c_ref[...]
    cond = c * jax.nn.sigmoid(c)
    o_ref[...] = _dot(cond, w_ref[...], precision=lax.Precision.HIGHEST) + b_ref[...]


def _modulation(c, ada_w, ada_b):
    b, d = c.shape
    n = ada_w.shape[1]
    return pl.pallas_call(
        _mod_kernel,
        out_shape=jax.ShapeDtypeStruct((b, n), F32),
        grid=(n // MOD_TN,),
        in_specs=[
            pl.BlockSpec((b, d), lambda j: (0, 0)),
            pl.BlockSpec((d, MOD_TN), lambda j: (0, j)),
            pl.BlockSpec((1, MOD_TN), lambda j: (0, j)),
        ],
        out_specs=pl.BlockSpec((b, MOD_TN), lambda j: (0, j)),
        compiler_params=_params(("arbitrary",)),
        name="modulation",
    )(c, ada_w, ada_b.reshape(1, n))


PROJ_TM = 512
W_QKV = 3 * A_WIDTH
W_SMALL = 3 * LANES
PROJ_COLS = (W_QKV, B_WIDTH, IDX_HEADS * IDX_DIM, 2 * D_MODEL, W_SMALL)


def _proj_kernel(x_ref, mod_ref, w_ref, qkv_ref, bq_ref, iq_ref, gates_ref, small_ref):
    m = mod_ref[0]
    u = (x_ref[...] * (1.0 + m[1:2]) + m[0:1]).astype(BF16)
    off = 0
    for o_ref, n in zip((qkv_ref, bq_ref, iq_ref, gates_ref, small_ref), PROJ_COLS):
        o_ref[...] = _dot(u, w_ref[:, off:off + n]).astype(o_ref.dtype)
        off += n


def _projection(x2, mod3, w_cat):
    n_tok, d = x2.shape
    tiles_per_batch = SEQ // PROJ_TM
    out_dtypes = (BF16, BF16, BF16, BF16, F32)
    return pl.pallas_call(
        _proj_kernel,
        out_shape=[jax.ShapeDtypeStruct((n_tok, n), dt) for n, dt in zip(PROJ_COLS, out_dtypes)],
        grid=(n_tok // PROJ_TM,),
        in_specs=[
            pl.BlockSpec((PROJ_TM, d), lambda i: (i, 0)),
            pl.BlockSpec((1, 6, d), lambda i: (i // tiles_per_batch, 0, 0)),
            pl.BlockSpec(w_cat.shape, lambda i: (0, 0)),
        ],
        out_specs=[pl.BlockSpec((PROJ_TM, n), lambda i: (i, 0)) for n in PROJ_COLS],
        compiler_params=_params(("arbitrary",)),
        name="projection",
    )(x2, mod3, w_cat)


A_TQ = 256


def _diff_attn_kernel(q_ref, k_ref, v_ref, lam_ref, g_ref, slope_ref, o_ref, bias_ref):
    slope = slope_ref[0][:, 0:1]
    r = lax.broadcasted_iota(jnp.int32, (A_TQ, SEQ), 0)
    j = lax.broadcasted_iota(jnp.int32, (A_TQ, SEQ), 1)
    dist = jnp.abs(r + (SEQ - A_TQ) - j).astype(F32)
    visible = (j - (SEQ - A_TQ)) // CHUNK <= r // CHUNK
    bias_ref[...] = jnp.where(visible, -slope * dist, NEG_INF)

    lv = lam_ref[...]
    lam = (jnp.exp(jnp.sum(lv[0:1] * lv[1:2], axis=1, keepdims=True))
           - jnp.exp(jnp.sum(lv[2:3] * lv[3:4], axis=1, keepdims=True)) + LAM_INIT)
    lane = lax.broadcasted_iota(jnp.int32, (A_TQ, 2 * A_QK_DIM), 1)
    scale = A_QK_DIM ** -0.5
    for i in range(SEQ // A_TQ):
        q0 = i * A_TQ
        kv = q0 + A_TQ
        q = q_ref[0, q0:q0 + A_TQ, :] * scale
        k = k_ref[0, 0:kv, :]
        bias = bias_ref[:, SEQ - kv:SEQ]
        ps, inv_ls = [], []
        for mth in range(2):
            qm = jnp.where((lane // A_QK_DIM) == mth, q, jnp.zeros_like(q))
            s = _dot_nt(qm, k) + bias
            mx = jnp.max(s, axis=-1, keepdims=True)
            p = jnp.exp(s - mx)
            ps.append(p)
            inv_ls.append(1.0 / jnp.sum(p, axis=-1, keepdims=True))
        attn = ps[0] * inv_ls[0] - ps[1] * (lam * inv_ls[1])
        o = _dot(attn.astype(BF16), v_ref[0, 0:kv, :])
        y = o * lax.rsqrt(jnp.mean(o * o, axis=-1, keepdims=True) + RMS_EPS) * g_ref[...]
        o_ref[0, q0:q0 + A_TQ, :] = (y * (1.0 - LAM_INIT)).astype(o_ref.dtype)


def _diff_attention(qkv3, lam_vecs, subln_g, slopes):
    b, s, _ = qkv3.shape
    blk = (1, s, A_V_DIM)
    return pl.pallas_call(
        _diff_attn_kernel,
        out_shape=jax.ShapeDtypeStruct((b, s, A_WIDTH), BF16),
        grid=(b, A_HEADS),
        in_specs=[
            pl.BlockSpec(blk, lambda bi, h: (bi, 0, h)),
            pl.BlockSpec(blk, lambda bi, h: (bi, 0, A_HEADS + h)),
            pl.BlockSpec(blk, lambda bi, h: (bi, 0, 2 * A_HEADS + h)),
            pl.BlockSpec(lam_vecs.shape, lambda bi, h: (0, 0)),
            pl.BlockSpec((1, A_V_DIM), lambda bi, h: (0, 0)),
            pl.BlockSpec((1, 1, LANES), lambda bi, h: (h, 0, 0)),
        ],
        out_specs=pl.BlockSpec(blk, lambda bi, h: (bi, 0, h)),
        scratch_shapes=[pltpu.VMEM((A_TQ, SEQ), F32)],
        compiler_params=_params(("arbitrary", "arbitrary")),
        name="diff_attention",
    )(qkv3, qkv3, qkv3, lam_vecs, subln_g, slopes)


B_TQ = 128
SMALL_KV = slice(0, 128)
SMALL_IK = slice(128, 256)
SMALL_IW = slice(256, 384)
BISECT_FIRST = 14
BISECT_MORE = 4
BISECT_MAX_ROUNDS = 80


def _key_count(mask_f32):
    return _over_keys(jnp.sum, mask_f32)


def _topk_mask(score_ref, selb_ref, kv):
    s = score_ref[0:kv, :]

    def bisect(_, st):
        lo, hi = st
        probe = 0.5 * lo + 0.5 * hi
        ge = _key_count(jnp.where(score_ref[0:kv, :] >= probe, 1.0, 0.0)) >= TOPK
        return jnp.where(ge, probe, lo), jnp.where(ge, hi, probe)

    def candidate(lo):
        sc = score_ref[0:kv, :]
        t_val = _over_keys(jnp.min, jnp.where(sc >= lo, sc, jnp.inf))
        n_gt = _key_count(jnp.where(sc > t_val, 1.0, 0.0))
        return t_val, n_gt, jnp.sum(jnp.where(n_gt >= TOPK, 1.0, 0.0))

    lo = _over_keys(jnp.min, jnp.where(s == NEG_INF, jnp.inf, s))
    hi = _over_keys(jnp.max, s)
    lo, hi = lax.fori_loop(0, BISECT_FIRST, bisect, (lo, hi))

    def unresolved(st):
        return (st[5] > 0.0) & (st[0] < BISECT_MAX_ROUNDS)

    def refine(st):
        lo, hi = lax.fori_loop(0, BISECT_MORE, bisect, (st[1], st[2]))
        return (st[0] + 1, lo, hi) + candidate(lo)

    _, _, _, t_val, n_gt, _ = lax.while_loop(unresolved, refine, (jnp.int32(0), lo, hi) + candidate(lo))
    gt = s > t_val
    eqf = jnp.where(s == t_val, 1.0, 0.0)
    need = TOPK - n_gt
    selb_ref[0:kv, :] = jnp.where(s >= t_val, 0.0, NEG_INF)
    tie_overflow = jnp.max(jnp.abs(_key_count(eqf) - need))

    @pl.when(tie_overflow > 0.0)
    def _():
        idx = lax.broadcasted_iota(jnp.int32, (kv, B_TQ), 0)

        def idx_step(b, j):
            c = j | lax.shift_left(jnp.int32(1), 10 - b)
            before = _key_count(jnp.where(idx < c, eqf, 0.0))
            return jnp.where(before < need, c, j)

        j_max = lax.fori_loop(0, 11, idx_step, jnp.zeros((1, B_TQ), jnp.int32))
        keep = gt | ((eqf > 0.0) & (idx <= j_max))
        selb_ref[0:kv, :] = jnp.where(keep, 0.0, NEG_INF)


def _sparse_attn_kernel(bq_ref, iq_ref, small_ref, kvg_ref, wuk_ref, wuvt_ref, slope_ref, o_ref,
                        bias_ref, ckv_ref, ckvt_ref, ik_ref, score_ref, selb_ref):
    @pl.when(pl.program_id(0) == 0)
    def _():
        j = lax.broadcasted_iota(jnp.int32, (SEQ, B_TQ), 0)
        r = lax.broadcasted_iota(jnp.int32, (SEQ, B_TQ), 1)
        dist = jnp.abs(r + (SEQ - B_TQ) - j).astype(F32)
        for h in range(B_HEADS):
            bias_ref[h] = -slope_ref[h][:, 0:1] * dist

    kv_lat = small_ref[0, :, SMALL_KV]
    ckv = kv_lat * lax.rsqrt(jnp.mean(kv_lat * kv_lat, axis=-1, keepdims=True) + RMS_EPS) * kvg_ref[...]
    ckv_ref[...] = ckv.astype(BF16)
    ckvt_ref[...] = ckv.T.astype(BF16)
    ik_ref[...] = small_ref[0, :, SMALL_IK].astype(BF16)

    lane = lax.broadcasted_iota(jnp.int32, (B_TQ, LANES), 1)
    kk = lax.broadcasted_iota(jnp.int32, (B_TQ, B_TQ), 0)
    qq = lax.broadcasted_iota(jnp.int32, (B_TQ, B_TQ), 1)
    diag_visible = kk // CHUNK <= qq // CHUNK
    scale = B_HEAD_DIM ** -0.5
    n_pairs = B_HEADS // 2

    def half_masked(pair):
        z = jnp.zeros_like(pair)
        return jnp.concatenate([jnp.where(lane < IDX_DIM, pair, z), jnp.where(lane >= IDX_DIM, pair, z)], axis=0)

    for i in range(SEQ // B_TQ):
        q0 = i * B_TQ
        kv = q0 + B_TQ
        rows = slice(q0, q0 + B_TQ)
        if kv <= TOPK:
            if q0:
                selb_ref[0:q0, :] = jnp.zeros((q0, B_TQ), F32)
            selb_ref[q0:kv, :] = jnp.where(diag_visible, 0.0, NEG_INF)
        else:
            iq = iq_ref[0, rows, :]
            iw_t = small_ref[0, rows, SMALL_IW].T
            ik = ik_ref[0:kv, :]
            score = jnp.zeros((kv, B_TQ), F32)
            for jp in range(n_pairs):
                x = _dot_nt(ik, half_masked(iq[:, jp * LANES:(jp + 1) * LANES]))
                score = (score + iw_t[2 * jp:2 * jp + 1] * jnp.maximum(x[:, 0:B_TQ], 0.0)
                         + iw_t[2 * jp + 1:2 * jp + 2] * jnp.maximum(x[:, B_TQ:2 * B_TQ], 0.0))
            score_ref[0:q0, :] = score[0:q0]
            score_ref[q0:kv, :] = jnp.where(diag_visible, score[q0:kv], NEG_INF)
            _topk_mask(score_ref, selb_ref, kv)

        bq = bq_ref[0, rows, :]
        q_abs = [(_dot(bq[:, (h // 2) * LANES:(h // 2 + 1) * LANES], wuk_ref[h]) * scale).astype(BF16)
                 for h in range(B_HEADS)]
        s_all = _dot_nt(ckv_ref[0:kv, :], jnp.concatenate(q_abs, axis=0))
        selb = selb_ref[0:kv, :]
        probs, inv_l = [], []
        for h in range(B_HEADS):
            sh = s_all[:, h * B_TQ:(h + 1) * B_TQ] + bias_ref[h, SEQ - kv:SEQ, :] + selb
            p = jnp.exp(sh - _over_keys(jnp.max, sh))
            inv_l.append(1.0 / _over_keys(jnp.sum, p))
            probs.append(p.astype(BF16))
        lat_t = _dot(ckvt_ref[:, 0:kv], jnp.concatenate(probs, axis=1))
        outs = []
        for jp in range(n_pairs):
            pair = jnp.concatenate([lat_t[:, h * B_TQ:(h + 1) * B_TQ] * inv_l[h] for h in (2 * jp, 2 * jp + 1)],
                                   axis=0).astype(BF16)
            outs.append(_dot(wuvt_ref[jp], pair))
        o_ref[0, rows, :] = jnp.concatenate(outs, axis=0).T.astype(o_ref.dtype)


def _sparse_attention(bq3, iq3, small3, kv_norm_g, wuk_pad, wuvt_pair, slopes):
    b, s, _ = bq3.shape
    return pl.pallas_call(
        _sparse_attn_kernel,
        out_shape=jax.ShapeDtypeStruct((b, s, B_WIDTH), BF16),
        grid=(b,),
        in_specs=[
            pl.BlockSpec((1, s, B_WIDTH), lambda bi: (bi, 0, 0)),
            pl.BlockSpec((1, s, IDX_HEADS * IDX_DIM), lambda bi: (bi, 0, 0)),
            pl.BlockSpec((1, s, W_SMALL), lambda bi: (bi, 0, 0)),
            pl.BlockSpec((1, B_KV_RANK), lambda bi: (0, 0)),
            pl.BlockSpec(wuk_pad.shape, lambda bi: (0, 0, 0)),
            pl.BlockSpec(wuvt_pair.shape, lambda bi: (0, 0, 0)),
            pl.BlockSpec(slopes.shape, lambda bi: (0, 0, 0)),
        ],
        out_specs=pl.BlockSpec((1, s, B_WIDTH), lambda bi: (bi, 0, 0)),
        scratch_shapes=[
            pltpu.VMEM((B_HEADS, SEQ, B_TQ), F32),
            pltpu.VMEM((SEQ, B_KV_RANK), BF16),
            pltpu.VMEM((B_KV_RANK, SEQ), BF16),
            pltpu.VMEM((SEQ, LANES), BF16),
            pltpu.VMEM((SEQ, B_TQ), F32),
            pltpu.VMEM((SEQ, B_TQ), F32),
        ],
        compiler_params=_params(("arbitrary",)),
        name="sparse_attention",
    )(bq3, iq3, small3, kv_norm_g, wuk_pad, wuvt_pair, slopes)


MERGE_TM = 512
ROUTER_ROWS = 32
ROUTE_ROWS = 8
ROW_W = D_MODEL + LANES


def _first_max_onehot(rows):
    mx = rows[0]
    for r in rows[1:]:
        mx = jnp.maximum(mx, r)
    taken = jnp.zeros_like(mx)
    hot = []
    for r in rows:
        h = jnp.where((r == mx) & (taken == 0.0), 1.0, 0.0)
        taken = taken + h
        hot.append(h)
    return hot, mx


def _softmax_rows(rows):
    mx = rows[0]
    for r in rows[1:]:
        mx = jnp.maximum(mx, r)
    e = [jnp.exp(r - mx) for r in rows]
    tot = e[0]
    for r in e[1:]:
        tot = tot + r
    return [r / tot for r in e]


def _merge_kernel(ya_ref, yb_ref, gates_ref, x_ref, mod_ref, wa_ref, wb_ref, wo_ref, g1_ref, b1_ref,
                  wr_ref, br_ref, tri_ref, x1_ref, u2_ref, route_ref, cnt_ref, run_ref):
    m = mod_ref[0]
    pa = _dot(ya_ref[...], wa_ref[...])
    pb = _dot(yb_ref[...], wb_ref[...])
    gt = jax.nn.sigmoid(gates_ref[...].astype(F32))
    mixed = gt[:, 0:D_MODEL] * pa + gt[:, D_MODEL:2 * D_MODEL] * pb
    z = _dot(mixed.astype(BF16), wo_ref[...])
    x1 = _layer_norm(DEEPNORM_ALPHA * x_ref[...] + m[2:3] * z, g1_ref[...], b1_ref[...])
    x1_ref[...] = x1
    u2 = x1 * (1.0 + m[4:5]) + m[3:4]
    u2_ref[:, 0:D_MODEL] = u2

    logits = _dot_nt(wr_ref[...], u2, precision=lax.Precision.HIGHEST) + br_ref[...]
    g_prob = _softmax_rows([logits[k:k + 1] for k in range(N_GROUPS)])
    g_hot, g_top = _first_max_onehot(g_prob)
    e_logit = []
    for jx in range(EXPERTS_PER_GROUP):
        acc = jnp.zeros_like(g_top)
        for g in range(N_GROUPS):
            row = N_GROUPS + g * EXPERTS_PER_GROUP + jx
            acc = acc + logits[row:row + 1] * g_hot[g]
        e_logit.append(acc)
    e_prob = _softmax_rows(e_logit)
    hot1, p1 = _first_max_onehot(e_prob)
    rest = [jnp.where(h > 0.0, NEG_INF, p) for h, p in zip(hot1, e_prob)]
    hot2, p2 = _first_max_onehot(rest)
    tot = p1 + p2
    w1 = g_top * (p1 / tot)
    w2 = g_top * (p2 / tot)

    @pl.when(pl.program_id(0) == 0)
    def _():
        run_ref[...] = jnp.zeros_like(run_ref)

    gid = g_hot[1] + 2.0 * g_hot[2] + 3.0 * g_hot[3]
    grp = lax.broadcasted_iota(jnp.int32, (ROUTE_ROWS, MERGE_TM), 0).astype(F32)
    hot8 = jnp.where(grp == gid, 1.0, 0.0)
    before = _dot(hot8.astype(BF16), tri_ref[...])
    rank = jnp.sum(hot8 * (run_ref[:, 0:1] + before), axis=0, keepdims=True)
    record = jnp.concatenate([gid] + [w1 * hot1[jx] + w2 * hot2[jx] for jx in range(EXPERTS_PER_GROUP)]
                             + [rank, jnp.zeros((LANES - 2 - EXPERTS_PER_GROUP, MERGE_TM), F32)], axis=0)
    route_ref[...] = record[0:ROUTE_ROWS]
    u2_ref[:, D_MODEL:D_MODEL + LANES] = record.T
    run_ref[...] = run_ref[...] + jnp.sum(hot8, axis=1, keepdims=True)
    cnt_ref[...] = run_ref[...]


def _merge(ya, yb, gates, x2, mod3, wa, wb, wo, ln_g, ln_b, w_router_t, b_router):
    n_tok, d = x2.shape
    tiles_per_batch = SEQ // MERGE_TM
    tok = lambda n: pl.BlockSpec((MERGE_TM, n), lambda i: (i, 0))
    full = lambda a: pl.BlockSpec(a.shape, lambda i: (0,) * a.ndim)
    tri = jnp.asarray(np.triu(np.ones((MERGE_TM, MERGE_TM), np.float32), k=1), BF16)
    return pl.pallas_call(
        _merge_kernel,
        out_shape=[
            jax.ShapeDtypeStruct((n_tok, d), F32),
            jax.ShapeDtypeStruct((n_tok, ROW_W), F32),
            jax.ShapeDtypeStruct((ROUTE_ROWS, n_tok), F32),
            jax.ShapeDtypeStruct((ROUTE_ROWS, LANES), F32),
        ],
        grid=(n_tok // MERGE_TM,),
        in_specs=[
            tok(A_WIDTH), tok(B_WIDTH), tok(2 * d), tok(d),
            pl.BlockSpec((1, 6, d), lambda i: (i // tiles_per_batch, 0, 0)),
            full(wa), full(wb), full(wo), full(ln_g), full(ln_b), full(w_router_t), full(b_router), full(tri),
        ],
        out_specs=[tok(d), tok(ROW_W), pl.BlockSpec((ROUTE_ROWS, MERGE_TM), lambda i: (0, i)),
                   pl.BlockSpec((ROUTE_ROWS, LANES), lambda i: (0, 0))],
        scratch_shapes=[pltpu.VMEM((ROUTE_ROWS, LANES), F32)],
        compiler_params=_params(("arbitrary",)),
        name="merge_router",
    )(ya, yb, gates, x2, mod3, wa, wb, wo, ln_g, ln_b, w_router_t, b_router, tri)


def _start_row_gather(src_hbm, idx_ref, buf, slot, sem, n_rows):
    def issue(r, carry):
        pltpu.make_async_copy(src_hbm.at[pl.ds(idx_ref[0, 0, r], 1)], buf.at[slot, pl.ds(r, 1)], sem.at[slot]).start()
        return carry

    lax.fori_loop(0, n_rows, issue, 0, unroll=8)


def _wait_row_gather(buf, slot, sem):
    pltpu.make_async_copy(buf.at[slot], buf.at[slot], sem.at[slot]).wait()


MOE_TM = 512


def _dispatch_kernel(ends_ref, idx_ref, rows_hbm, sorted_hbm, zero_buf, sem, zsem):
    i = pl.program_id(0)
    tile_rows = lambda ref: ref.at[pl.ds(0, MOE_TM)]

    @pl.when(i == 0)
    def _():
        zero_buf[...] = jnp.zeros_like(zero_buf)
        for g in range(N_GROUPS):
            start = pl.multiple_of(ends_ref[g] - MOE_TM, MOE_TM)
            nonempty = ends_ref[g] > (ends_ref[g - 1] if g else 0)

            @pl.when(nonempty)
            def _():
                cp = pltpu.make_async_copy(zero_buf, sorted_hbm.at[pl.ds(start, MOE_TM)], zsem)
                cp.start()
                cp.wait()

        for k in range(N_GROUPS):
            spare = sorted_hbm.shape[0] - (k + 1) * MOE_TM

            @pl.when(spare >= ends_ref[N_GROUPS - 1])
            def _():
                cp = pltpu.make_async_copy(zero_buf, sorted_hbm.at[pl.ds(spare, MOE_TM)], zsem)
                cp.start()
                cp.wait()

    def issue(r, carry):
        pltpu.make_async_copy(rows_hbm.at[pl.ds(i * MOE_TM + r, 1)], sorted_hbm.at[pl.ds(idx_ref[0, 0, r], 1)], sem).start()
        return carry

    lax.fori_loop(0, MOE_TM, issue, 0, unroll=8)

    @pl.when(i > 0)
    def _():
        pltpu.make_async_copy(tile_rows(rows_hbm), tile_rows(sorted_hbm), sem).wait()

    @pl.when(i == pl.num_programs(0) - 1)
    def _():
        pltpu.make_async_copy(tile_rows(rows_hbm), tile_rows(sorted_hbm), sem).wait()


def _dispatch(ends, dest3, rows, n_slots):
    n_steps = dest3.shape[0]
    w = rows.shape[1]
    return pl.pallas_call(
        _dispatch_kernel,
        out_shape=jax.ShapeDtypeStruct((n_slots, w), rows.dtype),
        grid_spec=pltpu.PrefetchScalarGridSpec(
            num_scalar_prefetch=1,
            grid=(n_steps,),
            in_specs=[
                pl.BlockSpec((1, 1, MOE_TM), lambda i, ends: (i, 0, 0), memory_space=pltpu.SMEM),
                pl.BlockSpec(memory_space=pl.ANY),
            ],
            out_specs=pl.BlockSpec(memory_space=pl.ANY),
            scratch_shapes=[pltpu.VMEM((MOE_TM, w), rows.dtype), pltpu.SemaphoreType.DMA(()), pltpu.SemaphoreType.DMA(())],
        ),
        compiler_params=_params(("arbitrary",)),
        name="dispatch",
    )(ends, dest3, rows)


def _moe_kernel(tg_ref, nu_ref, x_ref, wg_ref, wu_ref, wd_ref, y_ref):
    j = pl.program_id(0)
    n_used = nu_ref[0]

    @pl.when(j < n_used)
    def _():
        u = x_ref[:, 0:D_MODEL].astype(BF16)
        comb = x_ref[:, D_MODEL + 1:D_MODEL + 1 + EXPERTS_PER_GROUP]
        y = None
        for jx in range(EXPERTS_PER_GROUP):
            h = jax.nn.silu(_dot(u, wg_ref[jx])) * _dot(u, wu_ref[jx])
            yj = _dot((h * comb[:, jx:jx + 1]).astype(BF16), wd_ref[jx])
            y = yj if y is None else y + yj
        y_ref[...] = y

    @pl.when(j >= n_used)
    def _():
        y_ref[...] = jnp.zeros_like(y_ref)


def _moe(tile_group, n_used, rows_sorted, wg, wu, wd):
    n_tiles = rows_sorted.shape[0] // MOE_TM
    group_of = lambda w: pl.BlockSpec((EXPERTS_PER_GROUP,) + w.shape[1:], lambda j, tg, nu: (tg[j], 0, 0))
    return pl.pallas_call(
        _moe_kernel,
        out_shape=jax.ShapeDtypeStruct((n_tiles * MOE_TM, D_MODEL), F32),
        grid_spec=pltpu.PrefetchScalarGridSpec(
            num_scalar_prefetch=2,
            grid=(n_tiles,),
            in_specs=[
                pl.BlockSpec((MOE_TM, ROW_W), lambda j, tg, nu: (jnp.minimum(j, jnp.maximum(nu[0] - 1, 0)), 0)),
                group_of(wg), group_of(wu), group_of(wd),
            ],
            out_specs=pl.BlockSpec((MOE_TM, D_MODEL), lambda j, tg, nu: (j, 0)),
        ),
        compiler_params=_params(("arbitrary",)),
        name="moe",
    )(tile_group, n_used, rows_sorted, wg, wu, wd)


FINAL_TM = 512


def _final_kernel(idx_ref, idx_next_ref, y_hbm, x1_ref, mod_ref, g2_ref, b2_ref, o_ref, ybuf, sem):
    i = pl.program_id(0)
    slot = i % 2

    @pl.when(i == 0)
    def _():
        _start_row_gather(y_hbm, idx_ref, ybuf, 0, sem, FINAL_TM)

    @pl.when(i + 1 < pl.num_programs(0))
    def _():
        _start_row_gather(y_hbm, idx_next_ref, ybuf, 1 - slot, sem, FINAL_TM)

    _wait_row_gather(ybuf, slot, sem)
    m = mod_ref[0]
    o_ref[...] = _layer_norm(DEEPNORM_ALPHA * x1_ref[...] + m[5:6] * ybuf[slot], g2_ref[...], b2_ref[...])


def _final(dest3, y_sorted, x1, mod3, ln_g, ln_b):
    n_tok, d = x1.shape
    n_tiles = n_tok // FINAL_TM
    tiles_per_batch = SEQ // FINAL_TM
    idx_spec = lambda f: pl.BlockSpec((1, 1, FINAL_TM), f, memory_space=pltpu.SMEM)
    vec = pl.BlockSpec((1, d), lambda i: (0, 0))
    return pl.pallas_call(
        _final_kernel,
        out_shape=jax.ShapeDtypeStruct((n_tok, d), F32),
        grid=(n_tiles,),
        in_specs=[
            idx_spec(lambda i: (i, 0, 0)),
            idx_spec(lambda i: (jnp.minimum(i + 1, n_tiles - 1), 0, 0)),
            pl.BlockSpec(memory_space=pl.ANY),
            pl.BlockSpec((FINAL_TM, d), lambda i: (i, 0)),
            pl.BlockSpec((1, 6, d), lambda i: (i // tiles_per_batch, 0, 0)),
            vec, vec,
        ],
        out_specs=pl.BlockSpec((FINAL_TM, d), lambda i: (i, 0)),
        scratch_shapes=[pltpu.VMEM((2, FINAL_TM, d), F32), pltpu.SemaphoreType.DMA((2,))],
        compiler_params=_params(("arbitrary",)),
        name="combine_norm",
    )(dest3, dest3, y_sorted, x1, mod3, ln_g, ln_b)


def _routing_tables(route_t, cnt):
    n_tok = route_t.shape[1]
    n_tiles = n_tok // MOE_TM + N_GROUPS
    counts = cnt[:N_GROUPS, 0].astype(jnp.int32)
    padded = (counts + MOE_TM - 1) // MOE_TM * MOE_TM
    ends = jnp.cumsum(padded)
    gid = route_t[0].astype(jnp.int32)
    dest = (ends - padded)[gid] + route_t[5].astype(jnp.int32)
    tile_start = jnp.arange(n_tiles, dtype=jnp.int32) * MOE_TM
    tile_group = jnp.minimum(jnp.sum((tile_start[:, None] >= ends[None, :]).astype(jnp.int32), axis=1), N_GROUPS - 1)
    n_used = (ends[-1:] // MOE_TM).astype(jnp.int32)
    return tile_group, n_used, ends.astype(jnp.int32), dest.reshape(-1, 1, MOE_TM), n_tiles * MOE_TM


def _regroup_w_in(w):
    o = np.cumsum((512, 512, 512, 512, 128, 512, 64, 8, 2048))
    a_qkv, b_q, b_kv = w[:, :o[2]], w[:, o[2]:o[3]], w[:, o[3]:o[4]]
    i_q, i_k, i_w, gates = w[:, o[4]:o[5]], w[:, o[5]:o[6]], w[:, o[6]:o[7]], w[:, o[7]:o[8]]
    pad = jnp.zeros((w.shape[0], LANES - IDX_HEADS), w.dtype)
    return jnp.concatenate([a_qkv, b_q, i_q, gates, b_kv, i_k, i_k, i_w, pad], axis=1).astype(BF16)


def _pad_w_uk(w_uk):
    wt = jnp.transpose(w_uk, (1, 2, 0))
    z = jnp.zeros_like(wt)
    even = jnp.concatenate([wt, z], axis=1)
    odd = jnp.concatenate([z, wt], axis=1)
    sel = (jnp.arange(B_HEADS) % 2 == 0)[:, None, None]
    return jnp.where(sel, even, odd).astype(BF16)


def _pair_w_uv(w_uv):
    wv = jnp.transpose(w_uv, (1, 0, 2))
    z = jnp.zeros_like(wv[0::2])
    top = jnp.concatenate([wv[0::2], z], axis=2)
    bot = jnp.concatenate([z, wv[1::2]], axis=2)
    return jnp.transpose(jnp.concatenate([top, bot], axis=1), (0, 2, 1)).astype(BF16)


def kernel(x, c, ada_w, ada_b, w_in, lambda_q1, lambda_k1, lambda_q2, lambda_k2, a_subln_g, kv_norm_g, w_uk, w_uv,
           w_a_proj, w_b_proj, w_o, ln1_g, ln1_b, w_group, b_group, w_expert_router, b_expert_router,
           w_exp_gate, w_exp_up, w_exp_down, ln2_g, ln2_b):
    b, s, d = x.shape
    assert (s, d) == (SEQ, D_MODEL) and ada_w.shape[0] == DEPTH
    slopes_a, slopes_b = _alibi_slopes()
    lane_rep = lambda v: jnp.asarray(np.repeat(v[:, None, None], LANES, axis=2))
    x2 = x.reshape(b * s, d)
    l = 0
    mod3 = _modulation(c, ada_w[l], ada_b[l]).reshape(b, 6, d)

    qkv, bq, iq, gates, small = _projection(x2, mod3, _regroup_w_in(w_in[l]))
    lam_vecs = jnp.stack([lambda_q1[l], lambda_k1[l], lambda_q2[l], lambda_k2[l]]).astype(F32)
    y_a = _diff_attention(qkv.reshape(b, s, W_QKV), lam_vecs, a_subln_g[l].reshape(1, A_V_DIM), lane_rep(slopes_a))
    y_b = _sparse_attention(bq.reshape(b, s, -1), iq.reshape(b, s, -1), small.reshape(b, s, -1),
                            kv_norm_g[l].reshape(1, B_KV_RANK), _pad_w_uk(w_uk[l]), _pair_w_uv(w_uv[l]),
                            lane_rep(slopes_b))

    w_router_t = jnp.zeros((ROUTER_ROWS, d), F32).at[:N_GROUPS].set(w_group[l].T).at[
        N_GROUPS:N_GROUPS + N_EXPERTS].set(w_expert_router[l].T)
    b_router = jnp.zeros((ROUTER_ROWS, 1), F32).at[:N_GROUPS, 0].set(b_group[l]).at[
        N_GROUPS:N_GROUPS + N_EXPERTS, 0].set(b_expert_router[l])
    x1, u2, route_t, cnt = _merge(y_a.reshape(b * s, A_WIDTH), y_b.reshape(b * s, B_WIDTH), gates, x2, mod3,
                                  w_a_proj[l].astype(BF16), w_b_proj[l].astype(BF16), w_o[l].astype(BF16),
                                  ln1_g[l].reshape(1, d), ln1_b[l].reshape(1, d), w_router_t, b_router)

    tile_group, n_used, ends, dest3, n_slots = _routing_tables(route_t, cnt)
    wg, wu, wd = w_exp_gate[l].astype(BF16), w_exp_up[l].astype(BF16), w_exp_down[l].astype(BF16)
    y_sorted = _moe(tile_group, n_used, _dispatch(ends, dest3, u2, n_slots), wg, wu, wd)
    out = _final(dest3, y_sorted, x1, mod3, ln2_g[l].reshape(1, d), ln2_b[l].reshape(1, d))
    return out.reshape(b, s, d)
```

```python
import functools
import math

import numpy as np
import jax
import jax.numpy as jnp
from jax import lax
from jax.experimental import pallas as pl
from jax.experimental.pallas import tpu as pltpu

D_MODEL = 1024
SEQ = 2048
CHUNK = 64
A_QK_DIM = 64
A_V_DIM = 128
A_HEADS = 4
A_WIDTH = A_HEADS * A_V_DIM
B_HEAD_DIM = 64
B_HEADS = 8
B_WIDTH = B_HEADS * B_HEAD_DIM
B_KV_RANK = 128
IDX_HEADS = 8
IDX_DIM = 64
TOPK = 256
N_ALIBI_HEADS = A_HEADS + B_HEADS
N_GROUPS = 4
EXPERTS_PER_GROUP = 4
N_EXPERTS = 16
D_FF_EXPERT = 256
LN_EPS = 1e-5
RMS_EPS = 1e-5
DEPTH = 1
DEEPNORM_ALPHA = (2.0 * DEPTH) ** 0.25
LAM_INIT = 0.8 - 0.6 * math.exp(-0.3 * 0)

LANES = 128
VMEM_LIMIT_BYTES = 56 * 1024 * 1024

F32 = jnp.float32
BF16 = jnp.bfloat16
NEG_INF = float("-inf")

_NT = (((1,), (1,)), ((), ()))


def _dot(a, b, **kw):
    return jnp.dot(a, b, preferred_element_type=F32, **kw)


def _dot_nt(a, b, **kw):
    return lax.dot_general(a, b, _NT, preferred_element_type=F32, **kw)


def _alibi_slopes():
    n = N_ALIBI_HEADS
    slopes = (2.0 ** (-8.0 * np.arange(1, n + 1) / n)).astype(np.float32)
    a_idx = np.arange(A_HEADS) * (n // A_HEADS)
    b_idx = np.setdiff1d(np.arange(n), a_idx)
    return slopes[a_idx], slopes[b_idx]


def _layer_norm(h, g, b):
    mu = jnp.mean(h, axis=-1, keepdims=True)
    d = h - mu
    var = jnp.mean(d * d, axis=-1, keepdims=True)
    return d * lax.rsqrt(var + LN_EPS) * g + b


KEY_SLAB = 64


def _over_keys(op, x):
    part = op(x.reshape(x.shape[0] // KEY_SLAB, KEY_SLAB, x.shape[1]), axis=0)
    return op(part, axis=0, keepdims=True)


def _params(sem):
    return pltpu.CompilerParams(dimension_semantics=sem, vmem_limit_bytes=VMEM_LIMIT_BYTES)


MOD_TN = 1536


def _mod_kernel(c_ref, w_ref, b_ref, o_ref):
    c = c_ref[...]
    cond = c * jax.nn.sigmoid(c)
    o_ref[...] = _dot(cond, w_ref[...], precision=lax.Precision.HIGHEST) + b_ref[...]


def _modulation(c, ada_w, ada_b):
    b, d = c.shape
    n = ada_w.shape[1]
    return pl.pallas_call(
        _mod_kernel,
        out_shape=jax.ShapeDtypeStruct((b, n), F32),
        grid=(n // MOD_TN,),
        in_specs=[
            pl.BlockSpec((b, d), lambda j: (0, 0)),
            pl.BlockSpec((d, MOD_TN), lambda j: (0, j)),
            pl.BlockSpec((1, MOD_TN), lambda j: (0, j)),
        ],
        out_specs=pl.BlockSpec((b, MOD_TN), lambda j: (0, j)),
        compiler_params=_params(("arbitrary",)),
        name="modulation",
    )(c, ada_w, ada_b.reshape(1, n))


PROJ_TM = 512
W_QKV = 3 * A_WIDTH
W_SMALL = 3 * LANES
PROJ_COLS = (W_QKV, B_WIDTH, IDX_HEADS * IDX_DIM, 2 * D_MODEL, W_SMALL)


def _proj_kernel(x_ref, mod_ref, w_ref, qkv_ref, bq_ref, iq_ref, gates_ref, small_ref):
    m = mod_ref[0]
    u = (x_ref[...] * (1.0 + m[1:2]) + m[0:1]).astype(BF16)
    off = 0
    for o_ref, n in zip((qkv_ref, bq_ref, iq_ref, gates_ref, small_ref), PROJ_COLS):
        o_ref[...] = _dot(u, w_ref[:, off:off + n]).astype(o_ref.dtype)
        off += n


def _projection(x2, mod3, w_cat):
    n_tok, d = x2.shape
    tiles_per_batch = SEQ // PROJ_TM
    out_dtypes = (BF16, BF16, BF16, BF16, F32)
    return pl.pallas_call(
        _proj_kernel,
        out_shape=[jax.ShapeDtypeStruct((n_tok, n), dt) for n, dt in zip(PROJ_COLS, out_dtypes)],
        grid=(n_tok // PROJ_TM,),
        in_specs=[
            pl.BlockSpec((PROJ_TM, d), lambda i: (i, 0)),
            pl.BlockSpec((1, 6, d), lambda i: (i // tiles_per_batch, 0, 0)),
            pl.BlockSpec(w_cat.shape, lambda i: (0, 0)),
        ],
        out_specs=[pl.BlockSpec((PROJ_TM, n), lambda i: (i, 0)) for n in PROJ_COLS],
        compiler_params=_params(("arbitrary",)),
        name="projection",
    )(x2, mod3, w_cat)


A_TQ = 256


def _diff_attn_kernel(q_ref, k_ref, v_ref, lam_ref, g_ref, slope_ref, o_ref, bias_ref):
    slope = slope_ref[0][:, 0:1]
    r = lax.broadcasted_iota(jnp.int32, (A_TQ, SEQ), 0)
    j = lax.broadcasted_iota(jnp.int32, (A_TQ, SEQ), 1)
    dist = jnp.abs(r + (SEQ - A_TQ) - j).astype(F32)
    visible = (j - (SEQ - A_TQ)) // CHUNK <= r // CHUNK
    bias_ref[...] = jnp.where(visible, -slope * dist, NEG_INF)

    lv = lam_ref[...]
    lam = (jnp.exp(jnp.sum(lv[0:1] * lv[1:2], axis=1, keepdims=True))
           - jnp.exp(jnp.sum(lv[2:3] * lv[3:4], axis=1, keepdims=True)) + LAM_INIT)
    lane = lax.broadcasted_iota(jnp.int32, (A_TQ, 2 * A_QK_DIM), 1)
    scale = A_QK_DIM ** -0.5
    for i in range(SEQ // A_TQ):
        q0 = i * A_TQ
        kv = q0 + A_TQ
        q = q_ref[0, q0:q0 + A_TQ, :] * scale
        k = k_ref[0, 0:kv, :]
        bias = bias_ref[:, SEQ - kv:SEQ]
        ps, inv_ls = [], []
        for mth in range(2):
            qm = jnp.where((lane // A_QK_DIM) == mth, q, jnp.zeros_like(q))
            s = _dot_nt(qm, k) + bias
            mx = jnp.max(s, axis=-1, keepdims=True)
            p = jnp.exp(s - mx)
            ps.append(p)
            inv_ls.append(1.0 / jnp.sum(p, axis=-1, keepdims=True))
        attn = ps[0] * inv_ls[0] - ps[1] * (lam * inv_ls[1])
        o = _dot(attn.astype(BF16), v_ref[0, 0:kv, :])
        y = o * lax.rsqrt(jnp.mean(o * o, axis=-1, keepdims=True) + RMS_EPS) * g_ref[...]
        o_ref[0, q0:q0 + A_TQ, :] = (y * (1.0 - LAM_INIT)).astype(o_ref.dtype)


def _diff_attention(qkv3, lam_vecs, subln_g, slopes):
    b, s, _ = qkv3.shape
    blk = (1, s, A_V_DIM)
    return pl.pallas_call(
        _diff_attn_kernel,
        out_shape=jax.ShapeDtypeStruct((b, s, A_WIDTH), BF16),
        grid=(b, A_HEADS),
        in_specs=[
            pl.BlockSpec(blk, lambda bi, h: (bi, 0, h)),
            pl.BlockSpec(blk, lambda bi, h: (bi, 0, A_HEADS + h)),
            pl.BlockSpec(blk, lambda bi, h: (bi, 0, 2 * A_HEADS + h)),
            pl.BlockSpec(lam_vecs.shape, lambda bi, h: (0, 0)),
            pl.BlockSpec((1, A_V_DIM), lambda bi, h: (0, 0)),
            pl.BlockSpec((1, 1, LANES), lambda bi, h: (h, 0, 0)),
        ],
        out_specs=pl.BlockSpec(blk, lambda bi, h: (bi, 0, h)),
        scratch_shapes=[pltpu.VMEM((A_TQ, SEQ), F32)],
        compiler_params=_params(("arbitrary", "arbitrary")),
        name="diff_attention",
    )(qkv3, qkv3, qkv3, lam_vecs, subln_g, slopes)


B_TQ = 128
SMALL_KV = slice(0, 128)
SMALL_IK = slice(128, 256)
SMALL_IW = slice(256, 384)
BISECT_FIRST = 14
BISECT_MORE = 4
BISECT_MAX_ROUNDS = 80


def _key_count(mask_f32):
    return _over_keys(jnp.sum, mask_f32)


def _topk_mask(score_ref, selb_ref, kv):
    s = score_ref[0:kv, :]

    def bisect(_, st):
        lo, hi = st
        probe = 0.5 * lo + 0.5 * hi
        ge = _key_count(jnp.where(score_ref[0:kv, :] >= probe, 1.0, 0.0)) >= TOPK
        return jnp.where(ge, probe, lo), jnp.where(ge, hi, probe)

    def candidate(lo):
        sc = score_ref[0:kv, :]
        t_val = _over_keys(jnp.min, jnp.where(sc >= lo, sc, jnp.inf))
        n_gt = _key_count(jnp.where(sc > t_val, 1.0, 0.0))
        return t_val, n_gt, jnp.sum(jnp.where(n_gt >= TOPK, 1.0, 0.0))

    lo = _over_keys(jnp.min, jnp.where(s == NEG_INF, jnp.inf, s))
    hi = _over_keys(jnp.max, s)
    lo, hi = lax.fori_loop(0, BISECT_FIRST, bisect, (lo, hi))

    def unresolved(st):
        return (st[5] > 0.0) & (st[0] < BISECT_MAX_ROUNDS)

    def refine(st):
        lo, hi = lax.fori_loop(0, BISECT_MORE, bisect, (st[1], st[2]))
        return (st[0] + 1, lo, hi) + candidate(lo)

    _, _, _, t_val, n_gt, _ = lax.while_loop(unresolved, refine, (jnp.int32(0), lo, hi) + candidate(lo))
    gt = s > t_val
    eqf = jnp.where(s == t_val, 1.0, 0.0)
    need = TOPK - n_gt
    selb_ref[0:kv, :] = jnp.where(s >= t_val, 0.0, NEG_INF)
    tie_overflow = jnp.max(jnp.abs(_key_count(eqf) - need))

    @pl.when(tie_overflow > 0.0)
    def _():
        idx = lax.broadcasted_iota(jnp.int32, (kv, B_TQ), 0)

        def idx_step(b, j):
            c = j | lax.shift_left(jnp.int32(1), 10 - b)
            before = _key_count(jnp.where(idx < c, eqf, 0.0))
            return jnp.where(before < need, c, j)

        j_max = lax.fori_loop(0, 11, idx_step, jnp.zeros((1, B_TQ), jnp.int32))
        keep = gt | ((eqf > 0.0) & (idx <= j_max))
        selb_ref[0:kv, :] = jnp.where(keep, 0.0, NEG_INF)


def _sparse_attn_kernel(bq_ref, iq_ref, small_ref, kvg_ref, wuk_ref, wuvt_ref, slope_ref, o_ref,
                        bias_ref, ckv_ref, ckvt_ref, ik_ref, score_ref, selb_ref):
    @pl.when(pl.program_id(0) == 0)
    def _():
        j = lax.broadcasted_iota(jnp.int32, (SEQ, B_TQ), 0)
        r = lax.broadcasted_iota(jnp.int32, (SEQ, B_TQ), 1)
        dist = jnp.abs(r + (SEQ - B_TQ) - j).astype(F32)
        for h in range(B_HEADS):
            bias_ref[h] = -slope_ref[h][:, 0:1] * dist

    kv_lat = small_ref[0, :, SMALL_KV]
    ckv = kv_lat * lax.rsqrt(jnp.mean(kv_lat * kv_lat, axis=-1, keepdims=True) + RMS_EPS) * kvg_ref[...]
    ckv_ref[...] = ckv.astype(BF16)
    ckvt_ref[...] = ckv.T.astype(BF16)
    ik_ref[...] = small_ref[0, :, SMALL_IK].astype(BF16)

    lane = lax.broadcasted_iota(jnp.int32, (B_TQ, LANES), 1)
    kk = lax.broadcasted_iota(jnp.int32, (B_TQ, B_TQ), 0)
    qq = lax.broadcasted_iota(jnp.int32, (B_TQ, B_TQ), 1)
    diag_visible = kk // CHUNK <= qq // CHUNK
    scale = B_HEAD_DIM ** -0.5
    n_pairs = B_HEADS // 2

    def half_masked(pair):
        z = jnp.zeros_like(pair)
        return jnp.concatenate([jnp.where(lane < IDX_DIM, pair, z), jnp.where(lane >= IDX_DIM, pair, z)], axis=0)

    for i in range(SEQ // B_TQ):
        q0 = i * B_TQ
        kv = q0 + B_TQ
        rows = slice(q0, q0 + B_TQ)
        if kv <= TOPK:
            if q0:
                selb_ref[0:q0, :] = jnp.zeros((q0, B_TQ), F32)
            selb_ref[q0:kv, :] = jnp.where(diag_visible, 0.0, NEG_INF)
        else:
            iq = iq_ref[0, rows, :]
            iw_t = small_ref[0, rows, SMALL_IW].T
            ik = ik_ref[0:kv, :]
            score = jnp.zeros((kv, B_TQ), F32)
            for jp in range(n_pairs):
                x = _dot_nt(ik, half_masked(iq[:, jp * LANES:(jp + 1) * LANES]))
                score = (score + iw_t[2 * jp:2 * jp + 1] * jnp.maximum(x[:, 0:B_TQ], 0.0)
                         + iw_t[2 * jp + 1:2 * jp + 2] * jnp.maximum(x[:, B_TQ:2 * B_TQ], 0.0))
            score_ref[0:q0, :] = score[0:q0]
            score_ref[q0:kv, :] = jnp.where(diag_visible, score[q0:kv], NEG_INF)
            _topk_mask(score_ref, selb_ref, kv)

        bq = bq_ref[0, rows, :]
        q_abs = [(_dot(bq[:, (h // 2) * LANES:(h // 2 + 1) * LANES], wuk_ref[h]) * scale).astype(BF16)
                 for h in range(B_HEADS)]
        s_all = _dot_nt(ckv_ref[0:kv, :], jnp.concatenate(q_abs, axis=0))
        selb = selb_ref[0:kv, :]
        probs, inv_l = [], []
        for h in range(B_HEADS):
            sh = s_all[:, h * B_TQ:(h + 1) * B_TQ] + bias_ref[h, SEQ - kv:SEQ, :] + selb
            p = jnp.exp(sh - _over_keys(jnp.max, sh))
            inv_l.append(1.0 / _over_keys(jnp.sum, p))
            probs.append(p.astype(BF16))
        lat_t = _dot(ckvt_ref[:, 0:kv], jnp.concatenate(probs, axis=1))
        outs = []
        for jp in range(n_pairs):
            pair = jnp.concatenate([lat_t[:, h * B_TQ:(h + 1) * B_TQ] * inv_l[h] for h in (2 * jp, 2 * jp + 1)],
                                   axis=0).astype(BF16)
            outs.append(_dot(wuvt_ref[jp], pair))
        o_ref[0, rows, :] = jnp.concatenate(outs, axis=0).T.astype(o_ref.dtype)


def _sparse_attention(bq3, iq3, small3, kv_norm_g, wuk_pad, wuvt_pair, slopes):
    b, s, _ = bq3.shape
    return pl.pallas_call(
        _sparse_attn_kernel,
        out_shape=jax.ShapeDtypeStruct((b, s, B_WIDTH), BF16),
        grid=(b,),
        in_specs=[
            pl.BlockSpec((1, s, B_WIDTH), lambda bi: (bi, 0, 0)),
            pl.BlockSpec((1, s, IDX_HEADS * IDX_DIM), lambda bi: (bi, 0, 0)),
            pl.BlockSpec((1, s, W_SMALL), lambda bi: (bi, 0, 0)),
            pl.BlockSpec((1, B_KV_RANK), lambda bi: (0, 0)),
            pl.BlockSpec(wuk_pad.shape, lambda bi: (0, 0, 0)),
            pl.BlockSpec(wuvt_pair.shape, lambda bi: (0, 0, 0)),
            pl.BlockSpec(slopes.shape, lambda bi: (0, 0, 0)),
        ],
        out_specs=pl.BlockSpec((1, s, B_WIDTH), lambda bi: (bi, 0, 0)),
        scratch_shapes=[
            pltpu.VMEM((B_HEADS, SEQ, B_TQ), F32),
            pltpu.VMEM((SEQ, B_KV_RANK), BF16),
            pltpu.VMEM((B_KV_RANK, SEQ), BF16),
            pltpu.VMEM((SEQ, LANES), BF16),
            pltpu.VMEM((SEQ, B_TQ), F32),
            pltpu.VMEM((SEQ, B_TQ), F32),
        ],
        compiler_params=_params(("arbitrary",)),
        name="sparse_attention",
    )(bq3, iq3, small3, kv_norm_g, wuk_pad, wuvt_pair, slopes)


MERGE_TM = 512
ROUTER_ROWS = 32
ROUTE_ROWS = 8
ROW_W = D_MODEL + LANES


def _first_max_onehot(rows):
    mx = rows[0]
    for r in rows[1:]:
        mx = jnp.maximum(mx, r)
    taken = jnp.zeros_like(mx)
    hot = []
    for r in rows:
        h = jnp.where((r == mx) & (taken == 0.0), 1.0, 0.0)
        taken = taken + h
        hot.append(h)
    return hot, mx


def _softmax_rows(rows):
    mx = rows[0]
    for r in rows[1:]:
        mx = jnp.maximum(mx, r)
    e = [jnp.exp(r - mx) for r in rows]
    tot = e[0]
    for r in e[1:]:
        tot = tot + r
    return [r / tot for r in e]


def _merge_kernel(ya_ref, yb_ref, gates_ref, x_ref, mod_ref, wa_ref, wb_ref, wo_ref, g1_ref, b1_ref,
                  wr_ref, br_ref, tri_ref, x1_ref, u2_ref, route_ref, cnt_ref, run_ref):
    m = mod_ref[0]
    pa = _dot(ya_ref[...], wa_ref[...])
    pb = _dot(yb_ref[...], wb_ref[...])
    gt = jax.nn.sigmoid(gates_ref[...].astype(F32))
    mixed = gt[:, 0:D_MODEL] * pa + gt[:, D_MODEL:2 * D_MODEL] * pb
    z = _dot(mixed.astype(BF16), wo_ref[...])
    x1 = _layer_norm(DEEPNORM_ALPHA * x_ref[...] + m[2:3] * z, g1_ref[...], b1_ref[...])
    x1_ref[...] = x1
    u2 = x1 * (1.0 + m[4:5]) + m[3:4]
    u2_ref[:, 0:D_MODEL] = u2

    logits = _dot_nt(wr_ref[...], u2, precision=lax.Precision.HIGHEST) + br_ref[...]
    g_prob = _softmax_rows([logits[k:k + 1] for k in range(N_GROUPS)])
    g_hot, g_top = _first_max_onehot(g_prob)
    e_logit = []
    for jx in range(EXPERTS_PER_GROUP):
        acc = jnp.zeros_like(g_top)
        for g in range(N_GROUPS):
            row = N_GROUPS + g * EXPERTS_PER_GROUP + jx
            acc = acc + logits[row:row + 1] * g_hot[g]
        e_logit.append(acc)
    e_prob = _softmax_rows(e_logit)
    hot1, p1 = _first_max_onehot(e_prob)
    rest = [jnp.where(h > 0.0, NEG_INF, p) for h, p in zip(hot1, e_prob)]
    hot2, p2 = _first_max_onehot(rest)
    tot = p1 + p2
    w1 = g_top * (p1 / tot)
    w2 = g_top * (p2 / tot)

    @pl.when(pl.program_id(0) == 0)
    def _():
        run_ref[...] = jnp.zeros_like(run_ref)

    gid = g_hot[1] + 2.0 * g_hot[2] + 3.0 * g_hot[3]
    grp = lax.broadcasted_iota(jnp.int32, (ROUTE_ROWS, MERGE_TM), 0).astype(F32)
    hot8 = jnp.where(grp == gid, 1.0, 0.0)
    before = _dot(hot8.astype(BF16), tri_ref[...])
    rank = jnp.sum(hot8 * (run_ref[:, 0:1] + before), axis=0, keepdims=True)
    record = jnp.concatenate([gid] + [w1 * hot1[jx] + w2 * hot2[jx] for jx in range(EXPERTS_PER_GROUP)]
                             + [rank, jnp.zeros((LANES - 2 - EXPERTS_PER_GROUP, MERGE_TM), F32)], axis=0)
    route_ref[...] = record[0:ROUTE_ROWS]
    u2_ref[:, D_MODEL:D_MODEL + LANES] = record.T
    run_ref[...] = run_ref[...] + jnp.sum(hot8, axis=1, keepdims=True)
    cnt_ref[...] = run_ref[...]


def _merge(ya, yb, gates, x2, mod3, wa, wb, wo, ln_g, ln_b, w_router_t, b_router):
    n_tok, d = x2.shape
    tiles_per_batch = SEQ // MERGE_TM
    tok = lambda n: pl.BlockSpec((MERGE_TM, n), lambda i: (i, 0))
    full = lambda a: pl.BlockSpec(a.shape, lambda i: (0,) * a.ndim)
    tri = jnp.asarray(np.triu(np.ones((MERGE_TM, MERGE_TM), np.float32), k=1), BF16)
    return pl.pallas_call(
        _merge_kernel,
        out_shape=[
            jax.ShapeDtypeStruct((n_tok, d), F32),
            jax.ShapeDtypeStruct((n_tok, ROW_W), F32),
            jax.ShapeDtypeStruct((ROUTE_ROWS, n_tok), F32),
            jax.ShapeDtypeStruct((ROUTE_ROWS, LANES), F32),
        ],
        grid=(n_tok // MERGE_TM,),
        in_specs=[
            tok(A_WIDTH), tok(B_WIDTH), tok(2 * d), tok(d),
            pl.BlockSpec((1, 6, d), lambda i: (i // tiles_per_batch, 0, 0)),
            full(wa), full(wb), full(wo), full(ln_g), full(ln_b), full(w_router_t), full(b_router), full(tri),
        ],
        out_specs=[tok(d), tok(ROW_W), pl.BlockSpec((ROUTE_ROWS, MERGE_TM), lambda i: (0, i)),
                   pl.BlockSpec((ROUTE_ROWS, LANES), lambda i: (0, 0))],
        scratch_shapes=[pltpu.VMEM((ROUTE_ROWS, LANES), F32)],
        compiler_params=_params(("arbitrary",)),
        name="merge_router",
    )(ya, yb, gates, x2, mod3, wa, wb, wo, ln_g, ln_b, w_router_t, b_router, tri)


def _start_row_gather(src_hbm, idx_ref, buf, slot, sem, n_rows):
    def issue(r, carry):
        pltpu.make_async_copy(src_hbm.at[pl.ds(idx_ref[0, 0, r], 1)], buf.at[slot, pl.ds(r, 1)], sem.at[slot]).start()
        return carry

    lax.fori_loop(0, n_rows, issue, 0, unroll=8)


def _wait_row_gather(buf, slot, sem):
    pltpu.make_async_copy(buf.at[slot], buf.at[slot], sem.at[slot]).wait()


MOE_TM = 512


def _dispatch_kernel(ends_ref, idx_ref, rows_ref, sorted_hbm, zero_buf, stage, sem, zsem):
    i = pl.program_id(0)
    slot = i % 2

    @pl.when(i == 0)
    def _():
        zero_buf[...] = jnp.zeros_like(zero_buf)
        for g in range(N_GROUPS):
            start = pl.multiple_of(ends_ref[g] - MOE_TM, MOE_TM)
            nonempty = ends_ref[g] > (ends_ref[g - 1] if g else 0)

            @pl.when(nonempty)
            def _():
                cp = pltpu.make_async_copy(zero_buf, sorted_hbm.at[pl.ds(start, MOE_TM)], zsem)
                cp.start()
                cp.wait()

        for k in range(N_GROUPS):
            spare = sorted_hbm.shape[0] - (k + 1) * MOE_TM

            @pl.when(spare >= ends_ref[N_GROUPS - 1])
            def _():
                cp = pltpu.make_async_copy(zero_buf, sorted_hbm.at[pl.ds(spare, MOE_TM)], zsem)
                cp.start()
                cp.wait()

    stage[slot] = rows_ref[...]

    def issue(r, carry):
        pltpu.make_async_copy(stage.at[slot, pl.ds(r, 1)], sorted_hbm.at[pl.ds(idx_ref[0, 0, r], 1)], sem.at[slot]).start()
        return carry

    lax.fori_loop(0, MOE_TM, issue, 0, unroll=8)

    def wait_slot(s):
        pltpu.make_async_copy(stage.at[s], sorted_hbm.at[pl.ds(0, MOE_TM)], sem.at[s]).wait()

    @pl.when(i > 0)
    def _():
        wait_slot(1 - slot)

    @pl.when(i == pl.num_programs(0) - 1)
    def _():
        wait_slot(slot)


def _dispatch(ends, dest3, rows, n_slots):
    n_steps = dest3.shape[0]
    w = rows.shape[1]
    return pl.pallas_call(
        _dispatch_kernel,
        out_shape=jax.ShapeDtypeStruct((n_slots, w), rows.dtype),
        grid_spec=pltpu.PrefetchScalarGridSpec(
            num_scalar_prefetch=1,
            grid=(n_steps,),
            in_specs=[
                pl.BlockSpec((1, 1, MOE_TM), lambda i, ends: (i, 0, 0), memory_space=pltpu.SMEM),
                pl.BlockSpec((MOE_TM, w), lambda i, ends: (i, 0)),
            ],
            out_specs=pl.BlockSpec(memory_space=pl.ANY),
            scratch_shapes=[pltpu.VMEM((MOE_TM, w), rows.dtype), pltpu.VMEM((2, MOE_TM, w), rows.dtype),
                            pltpu.SemaphoreType.DMA((2,)), pltpu.SemaphoreType.DMA(())],
        ),
        compiler_params=_params(("arbitrary",)),
        name="dispatch",
    )(ends, dest3, rows)


def _moe_kernel(tg_ref, nu_ref, x_ref, wg_ref, wu_ref, wd_ref, y_ref):
    j = pl.program_id(0)
    n_used = nu_ref[0]

    @pl.when(j < n_used)
    def _():
        u = x_ref[:, 0:D_MODEL].astype(BF16)
        comb = x_ref[:, D_MODEL + 1:D_MODEL + 1 + EXPERTS_PER_GROUP]
        y = None
        for jx in range(EXPERTS_PER_GROUP):
            h = jax.nn.silu(_dot(u, wg_ref[jx])) * _dot(u, wu_ref[jx])
            yj = _dot((h * comb[:, jx:jx + 1]).astype(BF16), wd_ref[jx])
            y = yj if y is None else y + yj
        y_ref[...] = y

    @pl.when(j >= n_used)
    def _():
        y_ref[...] = jnp.zeros_like(y_ref)


def _moe(tile_group, n_used, rows_sorted, wg, wu, wd):
    n_tiles = rows_sorted.shape[0] // MOE_TM
    group_of = lambda w: pl.BlockSpec((EXPERTS_PER_GROUP,) + w.shape[1:], lambda j, tg, nu: (tg[j], 0, 0))
    return pl.pallas_call(
        _moe_kernel,
        out_shape=jax.ShapeDtypeStruct((n_tiles * MOE_TM, D_MODEL), F32),
        grid_spec=pltpu.PrefetchScalarGridSpec(
            num_scalar_prefetch=2,
            grid=(n_tiles,),
            in_specs=[
                pl.BlockSpec((MOE_TM, ROW_W), lambda j, tg, nu: (jnp.minimum(j, jnp.maximum(nu[0] - 1, 0)), 0)),
                group_of(wg), group_of(wu), group_of(wd),
            ],
            out_specs=pl.BlockSpec((MOE_TM, D_MODEL), lambda j, tg, nu: (j, 0)),
        ),
        compiler_params=_params(("arbitrary",)),
        name="moe",
    )(tile_group, n_used, rows_sorted, wg, wu, wd)


FINAL_TM = 512


def _final_kernel(idx_ref, idx_next_ref, y_hbm, x1_ref, mod_ref, g2_ref, b2_ref, o_ref, ybuf, sem):
    i = pl.program_id(0)
    slot = i % 2

    @pl.when(i == 0)
    def _():
        _start_row_gather(y_hbm, idx_ref, ybuf, 0, sem, FINAL_TM)

    @pl.when(i + 1 < pl.num_programs(0))
    def _():
        _start_row_gather(y_hbm, idx_next_ref, ybuf, 1 - slot, sem, FINAL_TM)

    _wait_row_gather(ybuf, slot, sem)
    m = mod_ref[0]
    o_ref[...] = _layer_norm(DEEPNORM_ALPHA * x1_ref[...] + m[5:6] * ybuf[slot], g2_ref[...], b2_ref[...])


def _final(dest3, y_sorted, x1, mod3, ln_g, ln_b):
    n_tok, d = x1.shape
    n_tiles = n_tok // FINAL_TM
    tiles_per_batch = SEQ // FINAL_TM
    idx_spec = lambda f: pl.BlockSpec((1, 1, FINAL_TM), f, memory_space=pltpu.SMEM)
    vec = pl.BlockSpec((1, d), lambda i: (0, 0))
    return pl.pallas_call(
        _final_kernel,
        out_shape=jax.ShapeDtypeStruct((n_tok, d), F32),
        grid=(n_tiles,),
        in_specs=[
            idx_spec(lambda i: (i, 0, 0)),
            idx_spec(lambda i: (jnp.minimum(i + 1, n_tiles - 1), 0, 0)),
            pl.BlockSpec(memory_space=pl.ANY),
            pl.BlockSpec((FINAL_TM, d), lambda i: (i, 0)),
            pl.BlockSpec((1, 6, d), lambda i: (i // tiles_per_batch, 0, 0)),
            vec, vec,
        ],
        out_specs=pl.BlockSpec((FINAL_TM, d), lambda i: (i, 0)),
        scratch_shapes=[pltpu.VMEM((2, FINAL_TM, d), F32), pltpu.SemaphoreType.DMA((2,))],
        compiler_params=_params(("arbitrary",)),
        name="combine_norm",
    )(dest3, dest3, y_sorted, x1, mod3, ln_g, ln_b)


def _routing_tables(route_t, cnt):
    n_tok = route_t.shape[1]
    n_tiles = n_tok // MOE_TM + N_GROUPS
    counts = cnt[:N_GROUPS, 0].astype(jnp.int32)
    padded = (counts + MOE_TM - 1) // MOE_TM * MOE_TM
    ends = jnp.cumsum(padded)
    gid = route_t[0].astype(jnp.int32)
    dest = (ends - padded)[gid] + route_t[5].astype(jnp.int32)
    tile_start = jnp.arange(n_tiles, dtype=jnp.int32) * MOE_TM
    tile_group = jnp.minimum(jnp.sum((tile_start[:, None] >= ends[None, :]).astype(jnp.int32), axis=1), N_GROUPS - 1)
    n_used = (ends[-1:] // MOE_TM).astype(jnp.int32)
    return tile_group, n_used, ends.astype(jnp.int32), dest.reshape(-1, 1, MOE_TM), n_tiles * MOE_TM


def _regroup_w_in(w):
    o = np.cumsum((512, 512, 512, 512, 128, 512, 64, 8, 2048))
    a_qkv, b_q, b_kv = w[:, :o[2]], w[:, o[2]:o[3]], w[:, o[3]:o[4]]
    i_q, i_k, i_w, gates = w[:, o[4]:o[5]], w[:, o[5]:o[6]], w[:, o[6]:o[7]], w[:, o[7]:o[8]]
    pad = jnp.zeros((w.shape[0], LANES - IDX_HEADS), w.dtype)
    return jnp.concatenate([a_qkv, b_q, i_q, gates, b_kv, i_k, i_k, i_w, pad], axis=1).astype(BF16)


def _pad_w_uk(w_uk):
    wt = jnp.transpose(w_uk, (1, 2, 0))
    z = jnp.zeros_like(wt)
    even = jnp.concatenate([wt, z], axis=1)
    odd = jnp.concatenate([z, wt], axis=1)
    sel = (jnp.arange(B_HEADS) % 2 == 0)[:, None, None]
    return jnp.where(sel, even, odd).astype(BF16)


def _pair_w_uv(w_uv):
    wv = jnp.transpose(w_uv, (1, 0, 2))
    z = jnp.zeros_like(wv[0::2])
    top = jnp.concatenate([wv[0::2], z], axis=2)
    bot = jnp.concatenate([z, wv[1::2]], axis=2)
    return jnp.transpose(jnp.concatenate([top, bot], axis=1), (0, 2, 1)).astype(BF16)


def kernel(x, c, ada_w, ada_b, w_in, lambda_q1, lambda_k1, lambda_q2, lambda_k2, a_subln_g, kv_norm_g, w_uk, w_uv,
           w_a_proj, w_b_proj, w_o, ln1_g, ln1_b, w_group, b_group, w_expert_router, b_expert_router,
           w_exp_gate, w_exp_up, w_exp_down, ln2_g, ln2_b):
    b, s, d = x.shape
    assert (s, d) == (SEQ, D_MODEL) and ada_w.shape[0] == DEPTH
    slopes_a, slopes_b = _alibi_slopes()
    lane_rep = lambda v: jnp.asarray(np.repeat(v[:, None, None], LANES, axis=2))
    x2 = x.reshape(b * s, d)
    l = 0
    mod3 = _modulation(c, ada_w[l], ada_b[l]).reshape(b, 6, d)

    qkv, bq, iq, gates, small = _projection(x2, mod3, _regroup_w_in(w_in[l]))
    lam_vecs = jnp.stack([lambda_q1[l], lambda_k1[l], lambda_q2[l], lambda_k2[l]]).astype(F32)
    y_a = _diff_attention(qkv.reshape(b, s, W_QKV), lam_vecs, a_subln_g[l].reshape(1, A_V_DIM), lane_rep(slopes_a))
    y_b = _sparse_attention(bq.reshape(b, s, -1), iq.reshape(b, s, -1), small.reshape(b, s, -1),
                            kv_norm_g[l].reshape(1, B_KV_RANK), _pad_w_uk(w_uk[l]), _pair_w_uv(w_uv[l]),
                            lane_rep(slopes_b))

    w_router_t = jnp.zeros((ROUTER_ROWS, d), F32).at[:N_GROUPS].set(w_group[l].T).at[
        N_GROUPS:N_GROUPS + N_EXPERTS].set(w_expert_router[l].T)
    b_router = jnp.zeros((ROUTER_ROWS, 1), F32).at[:N_GROUPS, 0].set(b_group[l]).at[
        N_GROUPS:N_GROUPS + N_EXPERTS, 0].set(b_expert_router[l])
    x1, u2, route_t, cnt = _merge(y_a.reshape(b * s, A_WIDTH), y_b.reshape(b * s, B_WIDTH), gates, x2, mod3,
                                  w_a_proj[l].astype(BF16), w_b_proj[l].astype(BF16), w_o[l].astype(BF16),
                                  ln1_g[l].reshape(1, d), ln1_b[l].reshape(1, d), w_router_t, b_router)

    tile_group, n_used, ends, dest3, n_slots = _routing_tables(route_t, cnt)
    wg, wu, wd = w_exp_gate[l].astype(BF16), w_exp_up[l].astype(BF16), w_exp_down[l].astype(BF16)
    y_sorted = _moe(tile_group, n_used, _dispatch(ends, dest3, u2, n_slots), wg, wu, wd)
    out = _final(dest3, y_sorted, x1, mod3, ln2_g[l].reshape(1, d), ln2_b[l].reshape(1, d))
    return out.reshape(b, s, d)
```

```python
import functools
import math

import numpy as np
import jax
import jax.numpy as jnp
from jax import lax
from jax.experimental import pallas as pl
from jax.experimental.pallas import tpu as pltpu

D_MODEL = 1024
SEQ = 2048
CHUNK = 64
A_QK_DIM = 64
A_V_DIM = 128
A_HEADS = 4
A_WIDTH = A_HEADS * A_V_DIM
B_HEAD_DIM = 64
B_HEADS = 8
B_WIDTH = B_HEADS * B_HEAD_DIM
B_KV_RANK = 128
IDX_HEADS = 8
IDX_DIM = 64
TOPK = 256
N_ALIBI_HEADS = A_HEADS + B_HEADS
N_GROUPS = 4
EXPERTS_PER_GROUP = 4
N_EXPERTS = 16
D_FF_EXPERT = 256
LN_EPS = 1e-5
RMS_EPS = 1e-5
DEPTH = 1
DEEPNORM_ALPHA = (2.0 * DEPTH) ** 0.25
LAM_INIT = 0.8 - 0.6 * math.exp(-0.3 * 0)
LOG2E = math.log2(math.e)

LANES = 128
VMEM_LIMIT_BYTES = 56 * 1024 * 1024

F32 = jnp.float32
BF16 = jnp.bfloat16
NEG_INF = float("-inf")

_NT = (((1,), (1,)), ((), ()))


def _dot(a, b, **kw):
    return jnp.dot(a, b, preferred_element_type=F32, **kw)


def _dot_nt(a, b, **kw):
    return lax.dot_general(a, b, _NT, preferred_element_type=F32, **kw)


def _alibi_slopes():
    n = N_ALIBI_HEADS
    slopes = (2.0 ** (-8.0 * np.arange(1, n + 1) / n)).astype(np.float32)
    a_idx = np.arange(A_HEADS) * (n // A_HEADS)
    b_idx = np.setdiff1d(np.arange(n), a_idx)
    return slopes[a_idx], slopes[b_idx]


def _layer_norm(h, g, b):
    mu = jnp.mean(h, axis=-1, keepdims=True)
    d = h - mu
    var = jnp.mean(d * d, axis=-1, keepdims=True)
    return d * lax.rsqrt(var + LN_EPS) * g + b


KEY_SLAB = 64


def _over_keys(op, x):
    part = op(x.reshape(x.shape[0] // KEY_SLAB, KEY_SLAB, x.shape[1]), axis=0)
    return op(part, axis=0, keepdims=True)


def _params(sem):
    return pltpu.CompilerParams(dimension_semantics=sem, vmem_limit_bytes=VMEM_LIMIT_BYTES)


MOD_TN = 1536


def _mod_kernel(c_ref, w_ref, b_ref, o_ref):
    c = c_ref[...]
    cond = c * jax.nn.sigmoid(c)
    o_ref[...] = _dot(cond, w_ref[...], precision=lax.Precision.HIGHEST) + b_ref[...]


def _modulation(c, ada_w, ada_b):
    b, d = c.shape
    n = ada_w.shape[1]
    return pl.pallas_call(
        _mod_kernel,
        out_shape=jax.ShapeDtypeStruct((b, n), F32),
        grid=(n // MOD_TN,),
        in_specs=[
            pl.BlockSpec((b, d), lambda j: (0, 0)),
            pl.BlockSpec((d, MOD_TN), lambda j: (0, j)),
            pl.BlockSpec((1, MOD_TN), lambda j: (0, j)),
        ],
        out_specs=pl.BlockSpec((b, MOD_TN), lambda j: (0, j)),
        compiler_params=_params(("arbitrary",)),
        name="modulation",
    )(c, ada_w, ada_b.reshape(1, n))


PROJ_TM = 512
W_QKV = 3 * A_WIDTH
W_SMALL = 3 * LANES
PROJ_COLS = (W_QKV, B_WIDTH, IDX_HEADS * IDX_DIM, 2 * D_MODEL, W_SMALL)


def _proj_kernel(x_ref, mod_ref, w_ref, qkv_ref, bq_ref, iq_ref, gates_ref, small_ref):
    m = mod_ref[0]
    u = (x_ref[...] * (1.0 + m[1:2]) + m[0:1]).astype(BF16)
    off = 0
    for o_ref, n in zip((qkv_ref, bq_ref, iq_ref, gates_ref, small_ref), PROJ_COLS):
        o_ref[...] = _dot(u, w_ref[:, off:off + n]).astype(o_ref.dtype)
        off += n


def _projection(x2, mod3, w_cat):
    n_tok, d = x2.shape
    tiles_per_batch = SEQ // PROJ_TM
    out_dtypes = (BF16, BF16, BF16, BF16, F32)
    return pl.pallas_call(
        _proj_kernel,
        out_shape=[jax.ShapeDtypeStruct((n_tok, n), dt) for n, dt in zip(PROJ_COLS, out_dtypes)],
        grid=(n_tok // PROJ_TM,),
        in_specs=[
            pl.BlockSpec((PROJ_TM, d), lambda i: (i, 0)),
            pl.BlockSpec((1, 6, d), lambda i: (i // tiles_per_batch, 0, 0)),
            pl.BlockSpec(w_cat.shape, lambda i: (0, 0)),
        ],
        out_specs=[pl.BlockSpec((PROJ_TM, n), lambda i: (i, 0)) for n in PROJ_COLS],
        compiler_params=_params(("arbitrary",)),
        name="projection",
    )(x2, mod3, w_cat)


A_TQ = 256


def _diff_attn_kernel(q_ref, k_ref, v_ref, lam_ref, g_ref, slope_ref, o_ref, bias_ref):
    slope = slope_ref[0][:, 0:1]
    r = lax.broadcasted_iota(jnp.int32, (A_TQ, SEQ), 0)
    j = lax.broadcasted_iota(jnp.int32, (A_TQ, SEQ), 1)
    dist = jnp.abs(r + (SEQ - A_TQ) - j).astype(F32)
    visible = (j - (SEQ - A_TQ)) // CHUNK <= r // CHUNK
    bias_ref[...] = jnp.where(visible, (-LOG2E) * slope * dist, NEG_INF)

    lv = lam_ref[...]
    lam = (jnp.exp(jnp.sum(lv[0:1] * lv[1:2], axis=1, keepdims=True))
           - jnp.exp(jnp.sum(lv[2:3] * lv[3:4], axis=1, keepdims=True)) + LAM_INIT)
    lane = lax.broadcasted_iota(jnp.int32, (A_TQ, 2 * A_QK_DIM), 1)
    for i in range(SEQ // A_TQ):
        q0 = i * A_TQ
        kv = q0 + A_TQ
        q = q_ref[0, q0:q0 + A_TQ, :]
        k = k_ref[0, 0:kv, :]
        bias = bias_ref[:, SEQ - kv:SEQ]
        ps, inv_ls = [], []
        for mth in range(2):
            qm = jnp.where((lane // A_QK_DIM) == mth, q, jnp.zeros_like(q))
            s = _dot_nt(qm, k) + bias
            mx = jnp.max(s, axis=-1, keepdims=True)
            p = jnp.exp2(s - mx)
            ps.append(p)
            inv_ls.append(1.0 / jnp.sum(p, axis=-1, keepdims=True))
        attn = ps[0] * inv_ls[0] - ps[1] * (lam * inv_ls[1])
        o = _dot(attn.astype(BF16), v_ref[0, 0:kv, :])
        y = o * lax.rsqrt(jnp.mean(o * o, axis=-1, keepdims=True) + RMS_EPS) * g_ref[...]
        o_ref[0, q0:q0 + A_TQ, :] = (y * (1.0 - LAM_INIT)).astype(o_ref.dtype)


def _diff_attention(qkv3, lam_vecs, subln_g, slopes):
    b, s, _ = qkv3.shape
    blk = (1, s, A_V_DIM)
    return pl.pallas_call(
        _diff_attn_kernel,
        out_shape=jax.ShapeDtypeStruct((b, s, A_WIDTH), BF16),
        grid=(b, A_HEADS),
        in_specs=[
            pl.BlockSpec(blk, lambda bi, h: (bi, 0, h)),
            pl.BlockSpec(blk, lambda bi, h: (bi, 0, A_HEADS + h)),
            pl.BlockSpec(blk, lambda bi, h: (bi, 0, 2 * A_HEADS + h)),
            pl.BlockSpec(lam_vecs.shape, lambda bi, h: (0, 0)),
            pl.BlockSpec((1, A_V_DIM), lambda bi, h: (0, 0)),
            pl.BlockSpec((1, 1, LANES), lambda bi, h: (h, 0, 0)),
        ],
        out_specs=pl.BlockSpec(blk, lambda bi, h: (bi, 0, h)),
        scratch_shapes=[pltpu.VMEM((A_TQ, SEQ), F32)],
        compiler_params=_params(("arbitrary", "arbitrary")),
        name="diff_attention",
    )(qkv3, qkv3, qkv3, lam_vecs, subln_g, slopes)


B_TQ = 128
SMALL_KV = slice(0, 128)
SMALL_IK = slice(128, 256)
SMALL_IW = slice(256, 384)
BISECT_FIRST = 18
BISECT_MORE = 3
BISECT_MAX_ROUNDS = 80


def _key_count(mask_f32):
    return _over_keys(jnp.sum, mask_f32)


def _topk_mask(score_ref, selb_ref, kv):
    s = score_ref[0:kv, :]

    def bisect(_, st):
        lo, hi = st
        probe = 0.5 * lo + 0.5 * hi
        ge = _key_count(jnp.where(score_ref[0:kv, :] >= probe, 1.0, 0.0)) >= TOPK
        return jnp.where(ge, probe, lo), jnp.where(ge, hi, probe)

    def candidate(lo):
        sc = score_ref[0:kv, :]
        t_val = _over_keys(jnp.min, jnp.where(sc >= lo, sc, jnp.inf))
        n_gt = _key_count(jnp.where(sc > t_val, 1.0, 0.0))
        return t_val, n_gt, jnp.sum(jnp.where(n_gt >= TOPK, 1.0, 0.0))

    lo = _over_keys(jnp.min, jnp.where(s == NEG_INF, jnp.inf, s))
    hi = _over_keys(jnp.max, s)
    lo, hi = lax.fori_loop(0, BISECT_FIRST, bisect, (lo, hi))

    def unresolved(st):
        return (st[5] > 0.0) & (st[0] < BISECT_MAX_ROUNDS)

    def refine(st):
        lo, hi = lax.fori_loop(0, BISECT_MORE, bisect, (st[1], st[2]))
        return (st[0] + 1, lo, hi) + candidate(lo)

    _, _, _, t_val, n_gt, _ = lax.while_loop(unresolved, refine, (jnp.int32(0), lo, hi) + candidate(lo))
    gt = s > t_val
    eqf = jnp.where(s == t_val, 1.0, 0.0)
    need = TOPK - n_gt
    selb_ref[0:kv, :] = jnp.where(s >= t_val, 0.0, NEG_INF)
    tie_overflow = jnp.max(jnp.abs(_key_count(eqf) - need))

    @pl.when(tie_overflow > 0.0)
    def _():
        idx = lax.broadcasted_iota(jnp.int32, (kv, B_TQ), 0)

        def idx_step(b, j):
            c = j | lax.shift_left(jnp.int32(1), 10 - b)
            before = _key_count(jnp.where(idx < c, eqf, 0.0))
            return jnp.where(before < need, c, j)

        j_max = lax.fori_loop(0, 11, idx_step, jnp.zeros((1, B_TQ), jnp.int32))
        keep = gt | ((eqf > 0.0) & (idx <= j_max))
        selb_ref[0:kv, :] = jnp.where(keep, 0.0, NEG_INF)


def _sparse_attn_kernel(bq_ref, iq_ref, small_ref, kvg_ref, wuk_ref, wuvt_ref, slope_ref, o_ref,
                        bias_ref, ckv_ref, ckvt_ref, ik_ref, score_ref, selb_ref):
    @pl.when(pl.program_id(0) == 0)
    def _():
        j = lax.broadcasted_iota(jnp.int32, (SEQ, B_TQ), 0)
        r = lax.broadcasted_iota(jnp.int32, (SEQ, B_TQ), 1)
        dist = jnp.abs(r + (SEQ - B_TQ) - j).astype(F32)
        for h in range(B_HEADS):
            bias_ref[h] = (-LOG2E) * slope_ref[h][:, 0:1] * dist

    kv_lat = small_ref[0, :, SMALL_KV]
    ckv = kv_lat * lax.rsqrt(jnp.mean(kv_lat * kv_lat, axis=-1, keepdims=True) + RMS_EPS) * kvg_ref[...]
    ckv_ref[...] = ckv.astype(BF16)
    ckvt_ref[...] = ckv.T.astype(BF16)
    ik_ref[...] = small_ref[0, :, SMALL_IK].astype(BF16)

    lane = lax.broadcasted_iota(jnp.int32, (B_TQ, LANES), 1)
    kk = lax.broadcasted_iota(jnp.int32, (B_TQ, B_TQ), 0)
    qq = lax.broadcasted_iota(jnp.int32, (B_TQ, B_TQ), 1)
    diag_visible = kk // CHUNK <= qq // CHUNK
    n_pairs = B_HEADS // 2

    def half_masked(pair):
        z = jnp.zeros_like(pair)
        return jnp.concatenate([jnp.where(lane < IDX_DIM, pair, z), jnp.where(lane >= IDX_DIM, pair, z)], axis=0)

    for i in range(SEQ // B_TQ):
        q0 = i * B_TQ
        kv = q0 + B_TQ
        rows = slice(q0, q0 + B_TQ)
        if kv <= TOPK:
            if q0:
                selb_ref[0:q0, :] = jnp.zeros((q0, B_TQ), F32)
            selb_ref[q0:kv, :] = jnp.where(diag_visible, 0.0, NEG_INF)
        else:
            iq = iq_ref[0, rows, :]
            iw_t = small_ref[0, rows, SMALL_IW].T
            ik = ik_ref[0:kv, :]
            score = jnp.zeros((kv, B_TQ), F32)
            for jp in range(n_pairs):
                x = _dot_nt(ik, half_masked(iq[:, jp * LANES:(jp + 1) * LANES]))
                score = (score + iw_t[2 * jp:2 * jp + 1] * jnp.maximum(x[:, 0:B_TQ], 0.0)
                         + iw_t[2 * jp + 1:2 * jp + 2] * jnp.maximum(x[:, B_TQ:2 * B_TQ], 0.0))
            score_ref[0:q0, :] = score[0:q0]
            score_ref[q0:kv, :] = jnp.where(diag_visible, score[q0:kv], NEG_INF)
            _topk_mask(score_ref, selb_ref, kv)

        bq = bq_ref[0, rows, :]
        q_abs = [_dot(bq[:, (h // 2) * LANES:(h // 2 + 1) * LANES], wuk_ref[h]).astype(BF16)
                 for h in range(B_HEADS)]
        s_all = _dot_nt(ckv_ref[0:kv, :], jnp.concatenate(q_abs, axis=0))
        selb = selb_ref[0:kv, :]
        probs, inv_l = [], []
        for h in range(B_HEADS):
            sh = s_all[:, h * B_TQ:(h + 1) * B_TQ] + bias_ref[h, SEQ - kv:SEQ, :] + selb
            p = jnp.exp2(sh - _over_keys(jnp.max, sh))
            inv_l.append(1.0 / _over_keys(jnp.sum, p))
            probs.append(p.astype(BF16))
        lat_t = _dot(ckvt_ref[:, 0:kv], jnp.concatenate(probs, axis=1))
        outs = []
        for jp in range(n_pairs):
            pair = jnp.concatenate([lat_t[:, h * B_TQ:(h + 1) * B_TQ] * inv_l[h] for h in (2 * jp, 2 * jp + 1)],
                                   axis=0).astype(BF16)
            outs.append(_dot(wuvt_ref[jp], pair))
        o_ref[0, rows, :] = jnp.concatenate(outs, axis=0).T.astype(o_ref.dtype)


def _sparse_attention(bq3, iq3, small3, kv_norm_g, wuk_pad, wuvt_pair, slopes):
    b, s, _ = bq3.shape
    return pl.pallas_call(
        _sparse_attn_kernel,
        out_shape=jax.ShapeDtypeStruct((b, s, B_WIDTH), BF16),
        grid=(b,),
        in_specs=[
            pl.BlockSpec((1, s, B_WIDTH), lambda bi: (bi, 0, 0)),
            pl.BlockSpec((1, s, IDX_HEADS * IDX_DIM), lambda bi: (bi, 0, 0)),
            pl.BlockSpec((1, s, W_SMALL), lambda bi: (bi, 0, 0)),
            pl.BlockSpec((1, B_KV_RANK), lambda bi: (0, 0)),
            pl.BlockSpec(wuk_pad.shape, lambda bi: (0, 0, 0)),
            pl.BlockSpec(wuvt_pair.shape, lambda bi: (0, 0, 0)),
            pl.BlockSpec(slopes.shape, lambda bi: (0, 0, 0)),
        ],
        out_specs=pl.BlockSpec((1, s, B_WIDTH), lambda bi: (bi, 0, 0)),
        scratch_shapes=[
            pltpu.VMEM((B_HEADS, SEQ, B_TQ), F32),
            pltpu.VMEM((SEQ, B_KV_RANK), BF16),
            pltpu.VMEM((B_KV_RANK, SEQ), BF16),
            pltpu.VMEM((SEQ, LANES), BF16),
            pltpu.VMEM((SEQ, B_TQ), F32),
            pltpu.VMEM((SEQ, B_TQ), F32),
        ],
        compiler_params=_params(("arbitrary",)),
        name="sparse_attention",
    )(bq3, iq3, small3, kv_norm_g, wuk_pad, wuvt_pair, slopes)


MERGE_TM = 512
ROUTER_ROWS = 32
ROUTE_ROWS = 8
ROW_W = D_MODEL + LANES


def _first_max_onehot(rows):
    mx = rows[0]
    for r in rows[1:]:
        mx = jnp.maximum(mx, r)
    taken = jnp.zeros_like(mx)
    hot = []
    for r in rows:
        h = jnp.where((r == mx) & (taken == 0.0), 1.0, 0.0)
        taken = taken + h
        hot.append(h)
    return hot, mx


def _softmax_rows(rows):
    mx = rows[0]
    for r in rows[1:]:
        mx = jnp.maximum(mx, r)
    e = [jnp.exp(r - mx) for r in rows]
    tot = e[0]
    for r in e[1:]:
        tot = tot + r
    return [r / tot for r in e]


def _merge_kernel(ya_ref, yb_ref, gates_ref, x_ref, mod_ref, wa_ref, wb_ref, wo_ref, g1_ref, b1_ref,
                  wr_ref, br_ref, tri_ref, x1_ref, u2_ref, route_ref, cnt_ref, run_ref):
    m = mod_ref[0]
    pa = _dot(ya_ref[...], wa_ref[...])
    pb = _dot(yb_ref[...], wb_ref[...])
    gt = jax.nn.sigmoid(gates_ref[...].astype(F32))
    mixed = gt[:, 0:D_MODEL] * pa + gt[:, D_MODEL:2 * D_MODEL] * pb
    z = _dot(mixed.astype(BF16), wo_ref[...])
    x1 = _layer_norm(DEEPNORM_ALPHA * x_ref[...] + m[2:3] * z, g1_ref[...], b1_ref[...])
    x1_ref[...] = x1
    u2 = x1 * (1.0 + m[4:5]) + m[3:4]
    u2_ref[:, 0:D_MODEL] = u2

    logits = _dot_nt(wr_ref[...], u2, precision=lax.Precision.HIGHEST) + br_ref[...]
    g_prob = _softmax_rows([logits[k:k + 1] for k in range(N_GROUPS)])
    g_hot, g_top = _first_max_onehot(g_prob)
    e_logit = []
    for jx in range(EXPERTS_PER_GROUP):
        acc = jnp.zeros_like(g_top)
        for g in range(N_GROUPS):
            row = N_GROUPS + g * EXPERTS_PER_GROUP + jx
            acc = acc + logits[row:row + 1] * g_hot[g]
        e_logit.append(acc)
    e_prob = _softmax_rows(e_logit)
    hot1, p1 = _first_max_onehot(e_prob)
    rest = [jnp.where(h > 0.0, NEG_INF, p) for h, p in zip(hot1, e_prob)]
    hot2, p2 = _first_max_onehot(rest)
    tot = p1 + p2
    w1 = g_top * (p1 / tot)
    w2 = g_top * (p2 / tot)

    @pl.when(pl.program_id(0) == 0)
    def _():
        run_ref[...] = jnp.zeros_like(run_ref)

    gid = g_hot[1] + 2.0 * g_hot[2] + 3.0 * g_hot[3]
    grp = lax.broadcasted_iota(jnp.int32, (ROUTE_ROWS, MERGE_TM), 0).astype(F32)
    hot8 = jnp.where(grp == gid, 1.0, 0.0)
    before = _dot(hot8.astype(BF16), tri_ref[...])
    rank = jnp.sum(hot8 * (run_ref[:, 0:1] + before), axis=0, keepdims=True)
    record = jnp.concatenate([gid] + [w1 * hot1[jx] + w2 * hot2[jx] for jx in range(EXPERTS_PER_GROUP)]
                             + [rank, jnp.zeros((LANES - 2 - EXPERTS_PER_GROUP, MERGE_TM), F32)], axis=0)
    route_ref[...] = record[0:ROUTE_ROWS]
    u2_ref[:, D_MODEL:D_MODEL + LANES] = record.T
    run_ref[...] = run_ref[...] + jnp.sum(hot8, axis=1, keepdims=True)
    cnt_ref[...] = run_ref[...]


def _merge(ya, yb, gates, x2, mod3, wa, wb, wo, ln_g, ln_b, w_router_t, b_router):
    n_tok, d = x2.shape
    tiles_per_batch = SEQ // MERGE_TM
    tok = lambda n: pl.BlockSpec((MERGE_TM, n), lambda i: (i, 0))
    full = lambda a: pl.BlockSpec(a.shape, lambda i: (0,) * a.ndim)
    tri = jnp.asarray(np.triu(np.ones((MERGE_TM, MERGE_TM), np.float32), k=1), BF16)
    return pl.pallas_call(
        _merge_kernel,
        out_shape=[
            jax.ShapeDtypeStruct((n_tok, d), F32),
            jax.ShapeDtypeStruct((n_tok, ROW_W), F32),
            jax.ShapeDtypeStruct((ROUTE_ROWS, n_tok), F32),
            jax.ShapeDtypeStruct((ROUTE_ROWS, LANES), F32),
        ],
        grid=(n_tok // MERGE_TM,),
        in_specs=[
            tok(A_WIDTH), tok(B_WIDTH), tok(2 * d), tok(d),
            pl.BlockSpec((1, 6, d), lambda i: (i // tiles_per_batch, 0, 0)),
            full(wa), full(wb), full(wo), full(ln_g), full(ln_b), full(w_router_t), full(b_router), full(tri),
        ],
        out_specs=[tok(d), tok(ROW_W), pl.BlockSpec((ROUTE_ROWS, MERGE_TM), lambda i: (0, i)),
                   pl.BlockSpec((ROUTE_ROWS, LANES), lambda i: (0, 0))],
        scratch_shapes=[pltpu.VMEM((ROUTE_ROWS, LANES), F32)],
        compiler_params=_params(("arbitrary",)),
        name="merge_router",
    )(ya, yb, gates, x2, mod3, wa, wb, wo, ln_g, ln_b, w_router_t, b_router, tri)


def _start_row_gather(src_hbm, idx_ref, buf, slot, sem, n_rows):
    def issue(r, carry):
        pltpu.make_async_copy(src_hbm.at[pl.ds(idx_ref[0, 0, r], 1)], buf.at[slot, pl.ds(r, 1)], sem.at[slot]).start()
        return carry

    lax.fori_loop(0, n_rows, issue, 0, unroll=8)


def _wait_row_gather(buf, slot, sem):
    pltpu.make_async_copy(buf.at[slot], buf.at[slot], sem.at[slot]).wait()


MOE_TM = 512


def _dispatch_kernel(ends_ref, idx_ref, rows_ref, sorted_hbm, zero_buf, stage, sem, zsem):
    i = pl.program_id(0)
    slot = i % 2

    @pl.when(i == 0)
    def _():
        zero_buf[...] = jnp.zeros_like(zero_buf)
        for g in range(N_GROUPS):
            start = pl.multiple_of(ends_ref[g] - MOE_TM, MOE_TM)
            nonempty = ends_ref[g] > (ends_ref[g - 1] if g else 0)

            @pl.when(nonempty)
            def _():
                cp = pltpu.make_async_copy(zero_buf, sorted_hbm.at[pl.ds(start, MOE_TM)], zsem)
                cp.start()
                cp.wait()

        for k in range(N_GROUPS):
            spare = sorted_hbm.shape[0] - (k + 1) * MOE_TM

            @pl.when(spare >= ends_ref[N_GROUPS - 1])
            def _():
                cp = pltpu.make_async_copy(zero_buf, sorted_hbm.at[pl.ds(spare, MOE_TM)], zsem)
                cp.start()
                cp.wait()

    stage[slot] = rows_ref[...]

    def issue(r, carry):
        pltpu.make_async_copy(stage.at[slot, pl.ds(r, 1)], sorted_hbm.at[pl.ds(idx_ref[0, 0, r], 1)], sem.at[slot]).start()
        return carry

    lax.fori_loop(0, MOE_TM, issue, 0, unroll=8)

    def wait_slot(s):
        pltpu.make_async_copy(stage.at[s], sorted_hbm.at[pl.ds(0, MOE_TM)], sem.at[s]).wait()

    @pl.when(i > 0)
    def _():
        wait_slot(1 - slot)

    @pl.when(i == pl.num_programs(0) - 1)
    def _():
        wait_slot(slot)


def _dispatch(ends, dest3, rows, n_slots):
    n_steps = dest3.shape[0]
    w = rows.shape[1]
    return pl.pallas_call(
        _dispatch_kernel,
        out_shape=jax.ShapeDtypeStruct((n_slots, w), rows.dtype),
        grid_spec=pltpu.PrefetchScalarGridSpec(
            num_scalar_prefetch=1,
            grid=(n_steps,),
            in_specs=[
                pl.BlockSpec((1, 1, MOE_TM), lambda i, ends: (i, 0, 0), memory_space=pltpu.SMEM),
                pl.BlockSpec((MOE_TM, w), lambda i, ends: (i, 0)),
            ],
            out_specs=pl.BlockSpec(memory_space=pl.ANY),
            scratch_shapes=[pltpu.VMEM((MOE_TM, w), rows.dtype), pltpu.VMEM((2, MOE_TM, w), rows.dtype),
                            pltpu.SemaphoreType.DMA((2,)), pltpu.SemaphoreType.DMA(())],
        ),
        compiler_params=_params(("arbitrary",)),
        name="dispatch",
    )(ends, dest3, rows)


def _moe_kernel(tg_ref, nu_ref, x_ref, wg_ref, wu_ref, wd_ref, y_ref):
    j = pl.program_id(0)
    n_used = nu_ref[0]

    @pl.when(j < n_used)
    def _():
        u = x_ref[:, 0:D_MODEL].astype(BF16)
        comb = x_ref[:, D_MODEL + 1:D_MODEL + 1 + EXPERTS_PER_GROUP]
        y = None
        for jx in range(EXPERTS_PER_GROUP):
            h = jax.nn.silu(_dot(u, wg_ref[jx])) * _dot(u, wu_ref[jx])
            yj = _dot((h * comb[:, jx:jx + 1]).astype(BF16), wd_ref[jx])
            y = yj if y is None else y + yj
        y_ref[...] = y

    @pl.when(j >= n_used)
    def _():
        y_ref[...] = jnp.zeros_like(y_ref)


def _moe(tile_group, n_used, rows_sorted, wg, wu, wd):
    n_tiles = rows_sorted.shape[0] // MOE_TM
    group_of = lambda w: pl.BlockSpec((EXPERTS_PER_GROUP,) + w.shape[1:], lambda j, tg, nu: (tg[j], 0, 0))
    return pl.pallas_call(
        _moe_kernel,
        out_shape=jax.ShapeDtypeStruct((n_tiles * MOE_TM, D_MODEL), F32),
        grid_spec=pltpu.PrefetchScalarGridSpec(
            num_scalar_prefetch=2,
            grid=(n_tiles,),
            in_specs=[
                pl.BlockSpec((MOE_TM, ROW_W), lambda j, tg, nu: (jnp.minimum(j, jnp.maximum(nu[0] - 1, 0)), 0)),
                group_of(wg), group_of(wu), group_of(wd),
            ],
            out_specs=pl.BlockSpec((MOE_TM, D_MODEL), lambda j, tg, nu: (j, 0)),
        ),
        compiler_params=_params(("arbitrary",)),
        name="moe",
    )(tile_group, n_used, rows_sorted, wg, wu, wd)


FINAL_TM = 512


def _final_kernel(idx_ref, idx_next_ref, y_hbm, x1_ref, mod_ref, g2_ref, b2_ref, o_ref, ybuf, sem):
    i = pl.program_id(0)
    slot = i % 2

    @pl.when(i == 0)
    def _():
        _start_row_gather(y_hbm, idx_ref, ybuf, 0, sem, FINAL_TM)

    @pl.when(i + 1 < pl.num_programs(0))
    def _():
        _start_row_gather(y_hbm, idx_next_ref, ybuf, 1 - slot, sem, FINAL_TM)

    _wait_row_gather(ybuf, slot, sem)
    m = mod_ref[0]
    o_ref[...] = _layer_norm(DEEPNORM_ALPHA * x1_ref[...] + m[5:6] * ybuf[slot], g2_ref[...], b2_ref[...])


def _final(dest3, y_sorted, x1, mod3, ln_g, ln_b):
    n_tok, d = x1.shape
    n_tiles = n_tok // FINAL_TM
    tiles_per_batch = SEQ // FINAL_TM
    idx_spec = lambda f: pl.BlockSpec((1, 1, FINAL_TM), f, memory_space=pltpu.SMEM)
    vec = pl.BlockSpec((1, d), lambda i: (0, 0))
    return pl.pallas_call(
        _final_kernel,
        out_shape=jax.ShapeDtypeStruct((n_tok, d), F32),
        grid=(n_tiles,),
        in_specs=[
            idx_spec(lambda i: (i, 0, 0)),
            idx_spec(lambda i: (jnp.minimum(i + 1, n_tiles - 1), 0, 0)),
            pl.BlockSpec(memory_space=pl.ANY),
            pl.BlockSpec((FINAL_TM, d), lambda i: (i, 0)),
            pl.BlockSpec((1, 6, d), lambda i: (i // tiles_per_batch, 0, 0)),
            vec, vec,
        ],
        out_specs=pl.BlockSpec((FINAL_TM, d), lambda i: (i, 0)),
        scratch_shapes=[pltpu.VMEM((2, FINAL_TM, d), F32), pltpu.SemaphoreType.DMA((2,))],
        compiler_params=_params(("arbitrary",)),
        name="combine_norm",
    )(dest3, dest3, y_sorted, x1, mod3, ln_g, ln_b)


def _routing_tables(route_t, cnt):
    n_tok = route_t.shape[1]
    n_tiles = n_tok // MOE_TM + N_GROUPS
    counts = cnt[:N_GROUPS, 0].astype(jnp.int32)
    padded = (counts + MOE_TM - 1) // MOE_TM * MOE_TM
    ends = jnp.cumsum(padded)
    gid = route_t[0].astype(jnp.int32)
    dest = (ends - padded)[gid] + route_t[5].astype(jnp.int32)
    tile_start = jnp.arange(n_tiles, dtype=jnp.int32) * MOE_TM
    tile_group = jnp.minimum(jnp.sum((tile_start[:, None] >= ends[None, :]).astype(jnp.int32), axis=1), N_GROUPS - 1)
    n_used = (ends[-1:] // MOE_TM).astype(jnp.int32)
    return tile_group, n_used, ends.astype(jnp.int32), dest.reshape(-1, 1, MOE_TM), n_tiles * MOE_TM


def _regroup_w_in(w):
    o = np.cumsum((512, 512, 512, 512, 128, 512, 64, 8, 2048))
    a_q, a_kv, b_q, b_kv = w[:, :o[0]], w[:, o[0]:o[2]], w[:, o[2]:o[3]], w[:, o[3]:o[4]]
    i_q, i_k, i_w, gates = w[:, o[4]:o[5]], w[:, o[5]:o[6]], w[:, o[6]:o[7]], w[:, o[7]:o[8]]
    pad = jnp.zeros((w.shape[0], LANES - IDX_HEADS), w.dtype)
    a_q = a_q * (LOG2E * A_QK_DIM ** -0.5)
    return jnp.concatenate([a_q, a_kv, b_q, i_q, gates, b_kv, i_k, i_k, i_w, pad], axis=1).astype(BF16)


def _pad_w_uk(w_uk):
    wt = jnp.transpose(w_uk, (1, 2, 0)) * (LOG2E * B_HEAD_DIM ** -0.5)
    z = jnp.zeros_like(wt)
    even = jnp.concatenate([wt, z], axis=1)
    odd = jnp.concatenate([z, wt], axis=1)
    sel = (jnp.arange(B_HEADS) % 2 == 0)[:, None, None]
    return jnp.where(sel, even, odd).astype(BF16)


def _pair_w_uv(w_uv):
    wv = jnp.transpose(w_uv, (1, 0, 2))
    z = jnp.zeros_like(wv[0::2])
    top = jnp.concatenate([wv[0::2], z], axis=2)
    bot = jnp.concatenate([z, wv[1::2]], axis=2)
    return jnp.transpose(jnp.concatenate([top, bot], axis=1), (0, 2, 1)).astype(BF16)


def kernel(x, c, ada_w, ada_b, w_in, lambda_q1, lambda_k1, lambda_q2, lambda_k2, a_subln_g, kv_norm_g, w_uk, w_uv,
           w_a_proj, w_b_proj, w_o, ln1_g, ln1_b, w_group, b_group, w_expert_router, b_expert_router,
           w_exp_gate, w_exp_up, w_exp_down, ln2_g, ln2_b):
    b, s, d = x.shape
    assert (s, d) == (SEQ, D_MODEL) and ada_w.shape[0] == DEPTH
    slopes_a, slopes_b = _alibi_slopes()
    lane_rep = lambda v: jnp.asarray(np.repeat(v[:, None, None], LANES, axis=2))
    x2 = x.reshape(b * s, d)
    l = 0
    mod3 = _modulation(c, ada_w[l], ada_b[l]).reshape(b, 6, d)

    qkv, bq, iq, gates, small = _projection(x2, mod3, _regroup_w_in(w_in[l]))
    lam_vecs = jnp.stack([lambda_q1[l], lambda_k1[l], lambda_q2[l], lambda_k2[l]]).astype(F32)
    y_a = _diff_attention(qkv.reshape(b, s, W_QKV), lam_vecs, a_subln_g[l].reshape(1, A_V_DIM), lane_rep(slopes_a))
    y_b = _sparse_attention(bq.reshape(b, s, -1), iq.reshape(b, s, -1), small.reshape(b, s, -1),
                            kv_norm_g[l].reshape(1, B_KV_RANK), _pad_w_uk(w_uk[l]), _pair_w_uv(w_uv[l]),
                            lane_rep(slopes_b))

    w_router_t = jnp.zeros((ROUTER_ROWS, d), F32).at[:N_GROUPS].set(w_group[l].T).at[
        N_GROUPS:N_GROUPS + N_EXPERTS].set(w_expert_router[l].T)
    b_router = jnp.zeros((ROUTER_ROWS, 1), F32).at[:N_GROUPS, 0].set(b_group[l]).at[
        N_GROUPS:N_GROUPS + N_EXPERTS, 0].set(b_expert_router[l])
    x1, u2, route_t, cnt = _merge(y_a.reshape(b * s, A_WIDTH), y_b.reshape(b * s, B_WIDTH), gates, x2, mod3,
                                  w_a_proj[l].astype(BF16), w_b_proj[l].astype(BF16), w_o[l].astype(BF16),
                                  ln1_g[l].reshape(1, d), ln1_b[l].reshape(1, d), w_router_t, b_router)

    tile_group, n_used, ends, dest3, n_slots = _routing_tables(route_t, cnt)
    wg, wu, wd = w_exp_gate[l].astype(BF16), w_exp_up[l].astype(BF16), w_exp_down[l].astype(BF16)
    y_sorted = _moe(tile_group, n_used, _dispatch(ends, dest3, u2, n_slots), wg, wu, wd)
    out = _final(dest3, y_sorted, x1, mod3, ln2_g[l].reshape(1, d), ln2_b[l].reshape(1, d))
    return out.reshape(b, s, d)
```

```python
import functools
import math

import numpy as np
import jax
import jax.numpy as jnp
from jax import lax
from jax.experimental import pallas as pl
from jax.experimental.pallas import tpu as pltpu

D_MODEL = 1024
SEQ = 2048
CHUNK = 64
A_QK_DIM = 64
A_V_DIM = 128
A_HEADS = 4
A_WIDTH = A_HEADS * A_V_DIM
B_HEAD_DIM = 64
B_HEADS = 8
B_WIDTH = B_HEADS * B_HEAD_DIM
B_KV_RANK = 128
IDX_HEADS = 8
IDX_DIM = 64
TOPK = 256
N_ALIBI_HEADS = A_HEADS + B_HEADS
N_GROUPS = 4
EXPERTS_PER_GROUP = 4
N_EXPERTS = 16
D_FF_EXPERT = 256
LN_EPS = 1e-5
RMS_EPS = 1e-5
DEPTH = 1
DEEPNORM_ALPHA = (2.0 * DEPTH) ** 0.25
LAM_INIT = 0.8 - 0.6 * math.exp(-0.3 * 0)
LOG2E = math.log2(math.e)

LANES = 128
VMEM_LIMIT_BYTES = 56 * 1024 * 1024

F32 = jnp.float32
BF16 = jnp.bfloat16
NEG_INF = float("-inf")

_NT = (((1,), (1,)), ((), ()))


def _dot(a, b, **kw):
    return jnp.dot(a, b, preferred_element_type=F32, **kw)


def _dot_nt(a, b, **kw):
    return lax.dot_general(a, b, _NT, preferred_element_type=F32, **kw)


def _alibi_slopes():
    n = N_ALIBI_HEADS
    slopes = (2.0 ** (-8.0 * np.arange(1, n + 1) / n)).astype(np.float32)
    a_idx = np.arange(A_HEADS) * (n // A_HEADS)
    b_idx = np.setdiff1d(np.arange(n), a_idx)
    return slopes[a_idx], slopes[b_idx]


def _layer_norm(h, g, b):
    mu = jnp.mean(h, axis=-1, keepdims=True)
    d = h - mu
    var = jnp.mean(d * d, axis=-1, keepdims=True)
    return d * lax.rsqrt(var + LN_EPS) * g + b


KEY_SLAB = 64


def _over_keys(op, x):
    part = op(x.reshape(x.shape[0] // KEY_SLAB, KEY_SLAB, x.shape[1]), axis=0)
    return op(part, axis=0, keepdims=True)


def _params(sem):
    return pltpu.CompilerParams(dimension_semantics=sem, vmem_limit_bytes=VMEM_LIMIT_BYTES)


MOD_TN = 1536


def _mod_kernel(c_ref, w_ref, b_ref, o_ref):
    c = c_ref[...]
    cond = c * jax.nn.sigmoid(c)
    o_ref[...] = _dot(cond, w_ref[...], precision=lax.Precision.HIGHEST) + b_ref[...]


def _modulation(c, ada_w, ada_b):
    b, d = c.shape
    n = ada_w.shape[1]
    return pl.pallas_call(
        _mod_kernel,
        out_shape=jax.ShapeDtypeStruct((b, n), F32),
        grid=(n // MOD_TN,),
        in_specs=[
            pl.BlockSpec((b, d), lambda j: (0, 0)),
            pl.BlockSpec((d, MOD_TN), lambda j: (0, j)),
            pl.BlockSpec((1, MOD_TN), lambda j: (0, j)),
        ],
        out_specs=pl.BlockSpec((b, MOD_TN), lambda j: (0, j)),
        compiler_params=_params(("arbitrary",)),
        name="modulation",
    )(c, ada_w, ada_b.reshape(1, n))


PROJ_TM = 512
W_QKV = 3 * A_WIDTH
W_SMALL = 3 * LANES
PROJ_COLS = (W_QKV, B_WIDTH, IDX_HEADS * IDX_DIM, 2 * D_MODEL, W_SMALL)


def _proj_kernel(x_ref, mod_ref, w_ref, qkv_ref, bq_ref, iq_ref, gates_ref, small_ref):
    m = mod_ref[0]
    u = (x_ref[...] * (1.0 + m[1:2]) + m[0:1]).astype(BF16)
    off = 0
    for o_ref, n in zip((qkv_ref, bq_ref, iq_ref, gates_ref, small_ref), PROJ_COLS):
        o_ref[...] = _dot(u, w_ref[:, off:off + n]).astype(o_ref.dtype)
        off += n


def _projection(x2, mod3, w_cat):
    n_tok, d = x2.shape
    tiles_per_batch = SEQ // PROJ_TM
    out_dtypes = (BF16, BF16, BF16, BF16, F32)
    return pl.pallas_call(
        _proj_kernel,
        out_shape=[jax.ShapeDtypeStruct((n_tok, n), dt) for n, dt in zip(PROJ_COLS, out_dtypes)],
        grid=(n_tok // PROJ_TM,),
        in_specs=[
            pl.BlockSpec((PROJ_TM, d), lambda i: (i, 0)),
            pl.BlockSpec((1, 6, d), lambda i: (i // tiles_per_batch, 0, 0)),
            pl.BlockSpec(w_cat.shape, lambda i: (0, 0)),
        ],
        out_specs=[pl.BlockSpec((PROJ_TM, n), lambda i: (i, 0)) for n in PROJ_COLS],
        compiler_params=_params(("arbitrary",)),
        name="projection",
    )(x2, mod3, w_cat)


A_TQ = 256


def _diff_attn_kernel(q_ref, k_ref, v_ref, lam_ref, g_ref, slope_ref, o_ref, bias_ref):
    slope = slope_ref[0][:, 0:1]
    r = lax.broadcasted_iota(jnp.int32, (A_TQ, SEQ), 0)
    j = lax.broadcasted_iota(jnp.int32, (A_TQ, SEQ), 1)
    dist = jnp.abs(r + (SEQ - A_TQ) - j).astype(F32)
    visible = (j - (SEQ - A_TQ)) // CHUNK <= r // CHUNK
    bias_ref[...] = jnp.where(visible, (-LOG2E) * slope * dist, NEG_INF)

    lv = lam_ref[...]
    lam = (jnp.exp(jnp.sum(lv[0:1] * lv[1:2], axis=1, keepdims=True))
           - jnp.exp(jnp.sum(lv[2:3] * lv[3:4], axis=1, keepdims=True)) + LAM_INIT)
    lane = lax.broadcasted_iota(jnp.int32, (A_TQ, 2 * A_QK_DIM), 1)
    for i in range(SEQ // A_TQ):
        q0 = i * A_TQ
        kv = q0 + A_TQ
        q = q_ref[0, q0:q0 + A_TQ, :]
        k = k_ref[0, 0:kv, :]
        bias = bias_ref[:, SEQ - kv:SEQ]
        ps, inv_ls = [], []
        for mth in range(2):
            qm = jnp.where((lane // A_QK_DIM) == mth, q, jnp.zeros_like(q))
            s = _dot_nt(qm, k) + bias
            mx = jnp.max(s, axis=-1, keepdims=True)
            p = jnp.exp2(s - mx)
            ps.append(p)
            inv_ls.append(1.0 / jnp.sum(p, axis=-1, keepdims=True))
        attn = ps[0] * inv_ls[0] - ps[1] * (lam * inv_ls[1])
        o = _dot(attn.astype(BF16), v_ref[0, 0:kv, :])
        y = o * lax.rsqrt(jnp.mean(o * o, axis=-1, keepdims=True) + RMS_EPS) * g_ref[...]
        o_ref[0, q0:q0 + A_TQ, :] = (y * (1.0 - LAM_INIT)).astype(o_ref.dtype)


def _diff_attention(qkv3, lam_vecs, subln_g, slopes):
    b, s, _ = qkv3.shape
    blk = (1, s, A_V_DIM)
    return pl.pallas_call(
        _diff_attn_kernel,
        out_shape=jax.ShapeDtypeStruct((b, s, A_WIDTH), BF16),
        grid=(b, A_HEADS),
        in_specs=[
            pl.BlockSpec(blk, lambda bi, h: (bi, 0, h)),
            pl.BlockSpec(blk, lambda bi, h: (bi, 0, A_HEADS + h)),
            pl.BlockSpec(blk, lambda bi, h: (bi, 0, 2 * A_HEADS + h)),
            pl.BlockSpec(lam_vecs.shape, lambda bi, h: (0, 0)),
            pl.BlockSpec((1, A_V_DIM), lambda bi, h: (0, 0)),
            pl.BlockSpec((1, 1, LANES), lambda bi, h: (h, 0, 0)),
        ],
        out_specs=pl.BlockSpec(blk, lambda bi, h: (bi, 0, h)),
        scratch_shapes=[pltpu.VMEM((A_TQ, SEQ), F32)],
        compiler_params=_params(("arbitrary", "arbitrary")),
        name="diff_attention",
    )(qkv3, qkv3, qkv3, lam_vecs, subln_g, slopes)


B_TQ = 128
SMALL_KV = slice(0, 128)
SMALL_IK = slice(128, 256)
SMALL_IW = slice(256, 384)
BISECT_FIRST = 18
BISECT_MORE = 3
BISECT_CHEAP_ROUNDS = 3
BISECT_MAX_ROUNDS = 80


def _key_count(mask_f32):
    return _over_keys(jnp.sum, mask_f32)


def _topk_mask(score_ref, selb_ref, kv):
    s = score_ref[0:kv, :]

    def bisect(_, st):
        lo, hi, n_lo = st
        probe = 0.5 * lo + 0.5 * hi
        n = _key_count(jnp.where(score_ref[0:kv, :] >= probe, 1.0, 0.0))
        ge = n >= TOPK
        return jnp.where(ge, probe, lo), jnp.where(ge, hi, probe), jnp.where(ge, n, n_lo)

    def inexact(n_lo):
        return jnp.sum(jnp.where(n_lo != TOPK, 1.0, 0.0))

    lo = _over_keys(jnp.min, jnp.where(s == NEG_INF, jnp.inf, s))
    state = (lo, _over_keys(jnp.max, s), _key_count(jnp.where(s >= lo, 1.0, 0.0)))
    state = lax.fori_loop(0, BISECT_FIRST, bisect, state)

    def cheap_more(c):
        return (c[0] + 1,) + lax.fori_loop(0, BISECT_MORE, bisect, c[1:])

    _, lo, hi, n_lo = lax.while_loop(lambda c: (inexact(c[3]) > 0.0) & (c[0] < BISECT_CHEAP_ROUNDS), cheap_more,
                                     (jnp.int32(0),) + state)
    all_exact = inexact(n_lo) == 0.0

    @pl.when(all_exact)
    def _():
        selb_ref[0:kv, :] = jnp.where(s >= lo, 0.0, NEG_INF)

    @pl.when(jnp.logical_not(all_exact))
    def _():
        def candidate(lo):
            sc = score_ref[0:kv, :]
            t_val = _over_keys(jnp.min, jnp.where(sc >= lo, sc, jnp.inf))
            n_gt = _key_count(jnp.where(sc > t_val, 1.0, 0.0))
            return t_val, n_gt, jnp.sum(jnp.where(n_gt >= TOPK, 1.0, 0.0))

        def refine(c):
            st = lax.fori_loop(0, BISECT_MORE, bisect, c[1:4])
            return (c[0] + 1,) + st + candidate(st[0])

        out = lax.while_loop(lambda c: (c[6] > 0.0) & (c[0] < BISECT_MAX_ROUNDS), refine,
                             (jnp.int32(0), lo, hi, n_lo) + candidate(lo))
        t_val, n_gt = out[4], out[5]
        gt = s > t_val
        eqf = jnp.where(s == t_val, 1.0, 0.0)
        need = TOPK - n_gt
        selb_ref[0:kv, :] = jnp.where(s >= t_val, 0.0, NEG_INF)
        tie_overflow = jnp.max(jnp.abs(_key_count(eqf) - need))

        @pl.when(tie_overflow > 0.0)
        def _():
            idx = lax.broadcasted_iota(jnp.int32, (kv, B_TQ), 0)

            def idx_step(b, j):
                c = j | lax.shift_left(jnp.int32(1), 10 - b)
                before = _key_count(jnp.where(idx < c, eqf, 0.0))
                return jnp.where(before < need, c, j)

            j_max = lax.fori_loop(0, 11, idx_step, jnp.zeros((1, B_TQ), jnp.int32))
            keep = gt | ((eqf > 0.0) & (idx <= j_max))
            selb_ref[0:kv, :] = jnp.where(keep, 0.0, NEG_INF)


def _sparse_attn_kernel(bq_ref, iq_ref, small_ref, kvg_ref, wuk_ref, wuvt_ref, slope_ref, o_ref,
                        bias_ref, ckv_ref, ckvt_ref, ik_ref, score_ref, selb_ref):
    @pl.when(pl.program_id(0) == 0)
    def _():
        j = lax.broadcasted_iota(jnp.int32, (SEQ, B_TQ), 0)
        r = lax.broadcasted_iota(jnp.int32, (SEQ, B_TQ), 1)
        dist = jnp.abs(r + (SEQ - B_TQ) - j).astype(F32)
        for h in range(B_HEADS):
            bias_ref[h] = (-LOG2E) * slope_ref[h][:, 0:1] * dist

    kv_lat = small_ref[0, :, SMALL_KV]
    ckv = kv_lat * lax.rsqrt(jnp.mean(kv_lat * kv_lat, axis=-1, keepdims=True) + RMS_EPS) * kvg_ref[...]
    ckv_ref[...] = ckv.astype(BF16)
    ckvt_ref[...] = ckv.T.astype(BF16)
    ik_ref[...] = small_ref[0, :, SMALL_IK].astype(BF16)

    lane = lax.broadcasted_iota(jnp.int32, (B_TQ, LANES), 1)
    kk = lax.broadcasted_iota(jnp.int32, (B_TQ, B_TQ), 0)
    qq = lax.broadcasted_iota(jnp.int32, (B_TQ, B_TQ), 1)
    diag_visible = kk // CHUNK <= qq // CHUNK
    n_pairs = B_HEADS // 2

    def half_masked(pair):
        z = jnp.zeros_like(pair)
        return jnp.concatenate([jnp.where(lane < IDX_DIM, pair, z), jnp.where(lane >= IDX_DIM, pair, z)], axis=0)

    for i in range(SEQ // B_TQ):
        q0 = i * B_TQ
        kv = q0 + B_TQ
        rows = slice(q0, q0 + B_TQ)
        if kv <= TOPK:
            if q0:
                selb_ref[0:q0, :] = jnp.zeros((q0, B_TQ), F32)
            selb_ref[q0:kv, :] = jnp.where(diag_visible, 0.0, NEG_INF)
        else:
            iq = iq_ref[0, rows, :]
            iw_t = small_ref[0, rows, SMALL_IW].T
            ik = ik_ref[0:kv, :]
            heads = jnp.concatenate([half_masked(iq[:, jp * LANES:(jp + 1) * LANES]) for jp in range(n_pairs)], axis=0)
            x = _dot_nt(ik, heads)
            score = iw_t[0:1] * jnp.maximum(x[:, 0:B_TQ], 0.0)
            for h in range(1, IDX_HEADS):
                score = score + iw_t[h:h + 1] * jnp.maximum(x[:, h * B_TQ:(h + 1) * B_TQ], 0.0)
            score_ref[0:q0, :] = score[0:q0]
            score_ref[q0:kv, :] = jnp.where(diag_visible, score[q0:kv], NEG_INF)
            _topk_mask(score_ref, selb_ref, kv)

        bq = bq_ref[0, rows, :]
        q_abs = [_dot(bq[:, (h // 2) * LANES:(h // 2 + 1) * LANES], wuk_ref[h]).astype(BF16)
                 for h in range(B_HEADS)]
        s_all = _dot_nt(ckv_ref[0:kv, :], jnp.concatenate(q_abs, axis=0))
        selb = selb_ref[0:kv, :]
        probs, inv_l = [], []
        for h in range(B_HEADS):
            sh = s_all[:, h * B_TQ:(h + 1) * B_TQ] + bias_ref[h, SEQ - kv:SEQ, :] + selb
            p = jnp.exp2(sh - _over_keys(jnp.max, sh))
            inv_l.append(1.0 / _over_keys(jnp.sum, p))
            probs.append(p.astype(BF16))
        lat_t = _dot(ckvt_ref[:, 0:kv], jnp.concatenate(probs, axis=1))
        outs = []
        for jp in range(n_pairs):
            pair = jnp.concatenate([lat_t[:, h * B_TQ:(h + 1) * B_TQ] * inv_l[h] for h in (2 * jp, 2 * jp + 1)],
                                   axis=0).astype(BF16)
            outs.append(_dot(wuvt_ref[jp], pair))
        o_ref[0, rows, :] = jnp.concatenate(outs, axis=0).T.astype(o_ref.dtype)


def _sparse_attention(bq3, iq3, small3, kv_norm_g, wuk_pad, wuvt_pair, slopes):
    b, s, _ = bq3.shape
    return pl.pallas_call(
        _sparse_attn_kernel,
        out_shape=jax.ShapeDtypeStruct((b, s, B_WIDTH), BF16),
        grid=(b,),
        in_specs=[
            pl.BlockSpec((1, s, B_WIDTH), lambda bi: (bi, 0, 0)),
            pl.BlockSpec((1, s, IDX_HEADS * IDX_DIM), lambda bi: (bi, 0, 0)),
            pl.BlockSpec((1, s, W_SMALL), lambda bi: (bi, 0, 0)),
            pl.BlockSpec((1, B_KV_RANK), lambda bi: (0, 0)),
            pl.BlockSpec(wuk_pad.shape, lambda bi: (0, 0, 0)),
            pl.BlockSpec(wuvt_pair.shape, lambda bi: (0, 0, 0)),
            pl.BlockSpec(slopes.shape, lambda bi: (0, 0, 0)),
        ],
        out_specs=pl.BlockSpec((1, s, B_WIDTH), lambda bi: (bi, 0, 0)),
        scratch_shapes=[
            pltpu.VMEM((B_HEADS, SEQ, B_TQ), F32),
            pltpu.VMEM((SEQ, B_KV_RANK), BF16),
            pltpu.VMEM((B_KV_RANK, SEQ), BF16),
            pltpu.VMEM((SEQ, LANES), BF16),
            pltpu.VMEM((SEQ, B_TQ), F32),
            pltpu.VMEM((SEQ, B_TQ), F32),
        ],
        compiler_params=_params(("arbitrary",)),
        name="sparse_attention",
    )(bq3, iq3, small3, kv_norm_g, wuk_pad, wuvt_pair, slopes)


MERGE_TM = 512
ROUTER_ROWS = 32
ROUTE_ROWS = 8
ROW_W = D_MODEL + LANES


def _first_max_onehot(rows):
    mx = rows[0]
    for r in rows[1:]:
        mx = jnp.maximum(mx, r)
    taken = jnp.zeros_like(mx)
    hot = []
    for r in rows:
        h = jnp.where((r == mx) & (taken == 0.0), 1.0, 0.0)
        taken = taken + h
        hot.append(h)
    return hot, mx


def _softmax_rows(rows):
    mx = rows[0]
    for r in rows[1:]:
        mx = jnp.maximum(mx, r)
    e = [jnp.exp(r - mx) for r in rows]
    tot = e[0]
    for r in e[1:]:
        tot = tot + r
    return [r / tot for r in e]


def _merge_kernel(ya_ref, yb_ref, gates_ref, x_ref, mod_ref, wa_ref, wb_ref, wo_ref, g1_ref, b1_ref,
                  wr_ref, br_ref, tri_ref, x1_ref, u2_ref, route_ref, cnt_ref, run_ref):
    m = mod_ref[0]
    pa = _dot(ya_ref[...], wa_ref[...])
    pb = _dot(yb_ref[...], wb_ref[...])
    gt = jax.nn.sigmoid(gates_ref[...].astype(F32))
    mixed = gt[:, 0:D_MODEL] * pa + gt[:, D_MODEL:2 * D_MODEL] * pb
    z = _dot(mixed.astype(BF16), wo_ref[...])
    x1 = _layer_norm(DEEPNORM_ALPHA * x_ref[...] + m[2:3] * z, g1_ref[...], b1_ref[...])
    x1_ref[...] = x1
    u2 = x1 * (1.0 + m[4:5]) + m[3:4]
    u2_ref[:, 0:D_MODEL] = u2

    logits = _dot_nt(wr_ref[...], u2, precision=lax.Precision.HIGHEST) + br_ref[...]
    g_prob = _softmax_rows([logits[k:k + 1] for k in range(N_GROUPS)])
    g_hot, g_top = _first_max_onehot(g_prob)
    e_logit = []
    for jx in range(EXPERTS_PER_GROUP):
        acc = jnp.zeros_like(g_top)
        for g in range(N_GROUPS):
            row = N_GROUPS + g * EXPERTS_PER_GROUP + jx
            acc = acc + logits[row:row + 1] * g_hot[g]
        e_logit.append(acc)
    e_prob = _softmax_rows(e_logit)
    hot1, p1 = _first_max_onehot(e_prob)
    rest = [jnp.where(h > 0.0, NEG_INF, p) for h, p in zip(hot1, e_prob)]
    hot2, p2 = _first_max_onehot(rest)
    tot = p1 + p2
    w1 = g_top * (p1 / tot)
    w2 = g_top * (p2 / tot)

    @pl.when(pl.program_id(0) == 0)
    def _():
        run_ref[...] = jnp.zeros_like(run_ref)

    gid = g_hot[1] + 2.0 * g_hot[2] + 3.0 * g_hot[3]
    grp = lax.broadcasted_iota(jnp.int32, (ROUTE_ROWS, MERGE_TM), 0).astype(F32)
    hot8 = jnp.where(grp == gid, 1.0, 0.0)
    before = _dot(hot8.astype(BF16), tri_ref[...])
    rank = jnp.sum(hot8 * (run_ref[:, 0:1] + before), axis=0, keepdims=True)
    record = jnp.concatenate([gid] + [w1 * hot1[jx] + w2 * hot2[jx] for jx in range(EXPERTS_PER_GROUP)]
                             + [rank, jnp.zeros((LANES - 2 - EXPERTS_PER_GROUP, MERGE_TM), F32)], axis=0)
    route_ref[...] = record[0:ROUTE_ROWS]
    u2_ref[:, D_MODEL:D_MODEL + LANES] = record.T
    run_ref[...] = run_ref[...] + jnp.sum(hot8, axis=1, keepdims=True)
    cnt_ref[...] = run_ref[...]


def _merge(ya, yb, gates, x2, mod3, wa, wb, wo, ln_g, ln_b, w_router_t, b_router):
    n_tok, d = x2.shape
    tiles_per_batch = SEQ // MERGE_TM
    tok = lambda n: pl.BlockSpec((MERGE_TM, n), lambda i: (i, 0))
    full = lambda a: pl.BlockSpec(a.shape, lambda i: (0,) * a.ndim)
    tri = jnp.asarray(np.triu(np.ones((MERGE_TM, MERGE_TM), np.float32), k=1), BF16)
    return pl.pallas_call(
        _merge_kernel,
        out_shape=[
            jax.ShapeDtypeStruct((n_tok, d), F32),
            jax.ShapeDtypeStruct((n_tok, ROW_W), F32),
            jax.ShapeDtypeStruct((ROUTE_ROWS, n_tok), F32),
            jax.ShapeDtypeStruct((ROUTE_ROWS, LANES), F32),
        ],
        grid=(n_tok // MERGE_TM,),
        in_specs=[
            tok(A_WIDTH), tok(B_WIDTH), tok(2 * d), tok(d),
            pl.BlockSpec((1, 6, d), lambda i: (i // tiles_per_batch, 0, 0)),
            full(wa), full(wb), full(wo), full(ln_g), full(ln_b), full(w_router_t), full(b_router), full(tri),
        ],
        out_specs=[tok(d), tok(ROW_W), pl.BlockSpec((ROUTE_ROWS, MERGE_TM), lambda i: (0, i)),
                   pl.BlockSpec((ROUTE_ROWS, LANES), lambda i: (0, 0))],
        scratch_shapes=[pltpu.VMEM((ROUTE_ROWS, LANES), F32)],
        compiler_params=_params(("arbitrary",)),
        name="merge_router",
    )(ya, yb, gates, x2, mod3, wa, wb, wo, ln_g, ln_b, w_router_t, b_router, tri)


def _start_row_gather(src_hbm, idx_ref, buf, slot, sem, n_rows):
    def issue(r, carry):
        pltpu.make_async_copy(src_hbm.at[pl.ds(idx_ref[0, 0, r], 1)], buf.at[slot, pl.ds(r, 1)], sem.at[slot]).start()
        return carry

    lax.fori_loop(0, n_rows, issue, 0, unroll=8)


def _wait_row_gather(buf, slot, sem):
    pltpu.make_async_copy(buf.at[slot], buf.at[slot], sem.at[slot]).wait()


MOE_TM = 512


def _dispatch_kernel(ends_ref, idx_ref, rows_ref, sorted_hbm, zero_buf, stage, sem, zsem):
    i = pl.program_id(0)
    slot = i % 2

    @pl.when(i == 0)
    def _():
        zero_buf[...] = jnp.zeros_like(zero_buf)
        for g in range(N_GROUPS):
            start = pl.multiple_of(ends_ref[g] - MOE_TM, MOE_TM)
            nonempty = ends_ref[g] > (ends_ref[g - 1] if g else 0)

            @pl.when(nonempty)
            def _():
                cp = pltpu.make_async_copy(zero_buf, sorted_hbm.at[pl.ds(start, MOE_TM)], zsem)
                cp.start()
                cp.wait()

        for k in range(N_GROUPS):
            spare = sorted_hbm.shape[0] - (k + 1) * MOE_TM

            @pl.when(spare >= ends_ref[N_GROUPS - 1])
            def _():
                cp = pltpu.make_async_copy(zero_buf, sorted_hbm.at[pl.ds(spare, MOE_TM)], zsem)
                cp.start()
                cp.wait()

    stage[slot] = rows_ref[...]

    def issue(r, carry):
        pltpu.make_async_copy(stage.at[slot, pl.ds(r, 1)], sorted_hbm.at[pl.ds(idx_ref[0, 0, r], 1)], sem.at[slot]).start()
        return carry

    lax.fori_loop(0, MOE_TM, issue, 0, unroll=8)

    def wait_slot(s):
        pltpu.make_async_copy(stage.at[s], sorted_hbm.at[pl.ds(0, MOE_TM)], sem.at[s]).wait()

    @pl.when(i > 0)
    def _():
        wait_slot(1 - slot)

    @pl.when(i == pl.num_programs(0) - 1)
    def _():
        wait_slot(slot)


def _dispatch(ends, dest3, rows, n_slots):
    n_steps = dest3.shape[0]
    w = rows.shape[1]
    return pl.pallas_call(
        _dispatch_kernel,
        out_shape=jax.ShapeDtypeStruct((n_slots, w), rows.dtype),
        grid_spec=pltpu.PrefetchScalarGridSpec(
            num_scalar_prefetch=1,
            grid=(n_steps,),
            in_specs=[
                pl.BlockSpec((1, 1, MOE_TM), lambda i, ends: (i, 0, 0), memory_space=pltpu.SMEM),
                pl.BlockSpec((MOE_TM, w), lambda i, ends: (i, 0)),
            ],
            out_specs=pl.BlockSpec(memory_space=pl.ANY),
            scratch_shapes=[pltpu.VMEM((MOE_TM, w), rows.dtype), pltpu.VMEM((2, MOE_TM, w), rows.dtype),
                            pltpu.SemaphoreType.DMA((2,)), pltpu.SemaphoreType.DMA(())],
        ),
        compiler_params=_params(("arbitrary",)),
        name="dispatch",
    )(ends, dest3, rows)


def _moe_kernel(tg_ref, nu_ref, x_ref, wg_ref, wu_ref, wd_ref, y_ref):
    j = pl.program_id(0)
    n_used = nu_ref[0]

    @pl.when(j < n_used)
    def _():
        u = x_ref[:, 0:D_MODEL].astype(BF16)
        comb = x_ref[:, D_MODEL + 1:D_MODEL + 1 + EXPERTS_PER_GROUP]
        y = None
        for jx in range(EXPERTS_PER_GROUP):
            h = jax.nn.silu(_dot(u, wg_ref[jx])) * _dot(u, wu_ref[jx])
            yj = _dot((h * comb[:, jx:jx + 1]).astype(BF16), wd_ref[jx])
            y = yj if y is None else y + yj
        y_ref[...] = y

    @pl.when(j >= n_used)
    def _():
        y_ref[...] = jnp.zeros_like(y_ref)


def _moe(tile_group, n_used, rows_sorted, wg, wu, wd):
    n_tiles = rows_sorted.shape[0] // MOE_TM
    group_of = lambda w: pl.BlockSpec((EXPERTS_PER_GROUP,) + w.shape[1:], lambda j, tg, nu: (tg[j], 0, 0))
    return pl.pallas_call(
        _moe_kernel,
        out_shape=jax.ShapeDtypeStruct((n_tiles * MOE_TM, D_MODEL), F32),
        grid_spec=pltpu.PrefetchScalarGridSpec(
            num_scalar_prefetch=2,
            grid=(n_tiles,),
            in_specs=[
                pl.BlockSpec((MOE_TM, ROW_W), lambda j, tg, nu: (jnp.minimum(j, jnp.maximum(nu[0] - 1, 0)), 0)),
                group_of(wg), group_of(wu), group_of(wd),
            ],
            out_specs=pl.BlockSpec((MOE_TM, D_MODEL), lambda j, tg, nu: (j, 0)),
        ),
        compiler_params=_params(("arbitrary",)),
        name="moe",
    )(tile_group, n_used, rows_sorted, wg, wu, wd)


FINAL_TM = 512


def _final_kernel(idx_ref, idx_next_ref, y_hbm, x1_ref, mod_ref, g2_ref, b2_ref, o_ref, ybuf, sem):
    i = pl.program_id(0)
    slot = i % 2

    @pl.when(i == 0)
    def _():
        _start_row_gather(y_hbm, idx_ref, ybuf, 0, sem, FINAL_TM)

    @pl.when(i + 1 < pl.num_programs(0))
    def _():
        _start_row_gather(y_hbm, idx_next_ref, ybuf, 1 - slot, sem, FINAL_TM)

    _wait_row_gather(ybuf, slot, sem)
    m = mod_ref[0]
    o_ref[...] = _layer_norm(DEEPNORM_ALPHA * x1_ref[...] + m[5:6] * ybuf[slot], g2_ref[...], b2_ref[...])


def _final(dest3, y_sorted, x1, mod3, ln_g, ln_b):
    n_tok, d = x1.shape
    n_tiles = n_tok // FINAL_TM
    tiles_per_batch = SEQ // FINAL_TM
    idx_spec = lambda f: pl.BlockSpec((1, 1, FINAL_TM), f, memory_space=pltpu.SMEM)
    vec = pl.BlockSpec((1, d), lambda i: (0, 0))
    return pl.pallas_call(
        _final_kernel,
        out_shape=jax.ShapeDtypeStruct((n_tok, d), F32),
        grid=(n_tiles,),
        in_specs=[
            idx_spec(lambda i: (i, 0, 0)),
            idx_spec(lambda i: (jnp.minimum(i + 1, n_tiles - 1), 0, 0)),
            pl.BlockSpec(memory_space=pl.ANY),
            pl.BlockSpec((FINAL_TM, d), lambda i: (i, 0)),
            pl.BlockSpec((1, 6, d), lambda i: (i // tiles_per_batch, 0, 0)),
            vec, vec,
        ],
        out_specs=pl.BlockSpec((FINAL_TM, d), lambda i: (i, 0)),
        scratch_shapes=[pltpu.VMEM((2, FINAL_TM, d), F32), pltpu.SemaphoreType.DMA((2,))],
        compiler_params=_params(("arbitrary",)),
        name="combine_norm",
    )(dest3, dest3, y_sorted, x1, mod3, ln_g, ln_b)


def _routing_tables(route_t, cnt):
    n_tok = route_t.shape[1]
    n_tiles = n_tok // MOE_TM + N_GROUPS
    counts = cnt[:N_GROUPS, 0].astype(jnp.int32)
    padded = (counts + MOE_TM - 1) // MOE_TM * MOE_TM
    ends = jnp.cumsum(padded)
    gid = route_t[0].astype(jnp.int32)
    dest = (ends - padded)[gid] + route_t[5].astype(jnp.int32)
    tile_start = jnp.arange(n_tiles, dtype=jnp.int32) * MOE_TM
    tile_group = jnp.minimum(jnp.sum((tile_start[:, None] >= ends[None, :]).astype(jnp.int32), axis=1), N_GROUPS - 1)
    n_used = (ends[-1:] // MOE_TM).astype(jnp.int32)
    return tile_group, n_used, ends.astype(jnp.int32), dest.reshape(-1, 1, MOE_TM), n_tiles * MOE_TM


def _regroup_w_in(w):
    o = np.cumsum((512, 512, 512, 512, 128, 512, 64, 8, 2048))
    a_q, a_kv, b_q, b_kv = w[:, :o[0]], w[:, o[0]:o[2]], w[:, o[2]:o[3]], w[:, o[3]:o[4]]
    i_q, i_k, i_w, gates = w[:, o[4]:o[5]], w[:, o[5]:o[6]], w[:, o[6]:o[7]], w[:, o[7]:o[8]]
    pad = jnp.zeros((w.shape[0], LANES - IDX_HEADS), w.dtype)
    a_q = a_q * (LOG2E * A_QK_DIM ** -0.5)
    return jnp.concatenate([a_q, a_kv, b_q, i_q, gates, b_kv, i_k, i_k, i_w, pad], axis=1).astype(BF16)


def _pad_w_uk(w_uk):
    wt = jnp.transpose(w_uk, (1, 2, 0)) * (LOG2E * B_HEAD_DIM ** -0.5)
    z = jnp.zeros_like(wt)
    even = jnp.concatenate([wt, z], axis=1)
    odd = jnp.concatenate([z, wt], axis=1)
    sel = (jnp.arange(B_HEADS) % 2 == 0)[:, None, None]
    return jnp.where(sel, even, odd).astype(BF16)


def _pair_w_uv(w_uv):
    wv = jnp.transpose(w_uv, (1, 0, 2))
    z = jnp.zeros_like(wv[0::2])
    top = jnp.concatenate([wv[0::2], z], axis=2)
    bot = jnp.concatenate([z, wv[1::2]], axis=2)
    return jnp.transpose(jnp.concatenate([top, bot], axis=1), (0, 2, 1)).astype(BF16)


def kernel(x, c, ada_w, ada_b, w_in, lambda_q1, lambda_k1, lambda_q2, lambda_k2, a_subln_g, kv_norm_g, w_uk, w_uv,
           w_a_proj, w_b_proj, w_o, ln1_g, ln1_b, w_group, b_group, w_expert_router, b_expert_router,
           w_exp_gate, w_exp_up, w_exp_down, ln2_g, ln2_b):
    b, s, d = x.shape
    assert (s, d) == (SEQ, D_MODEL) and ada_w.shape[0] == DEPTH
    slopes_a, slopes_b = _alibi_slopes()
    lane_rep = lambda v: jnp.asarray(np.repeat(v[:, None, None], LANES, axis=2))
    x2 = x.reshape(b * s, d)
    l = 0
    mod3 = _modulation(c, ada_w[l], ada_b[l]).reshape(b, 6, d)

    qkv, bq, iq, gates, small = _projection(x2, mod3, _regroup_w_in(w_in[l]))
    lam_vecs = jnp.stack([lambda_q1[l], lambda_k1[l], lambda_q2[l], lambda_k2[l]]).astype(F32)
    y_a = _diff_attention(qkv.reshape(b, s, W_QKV), lam_vecs, a_subln_g[l].reshape(1, A_V_DIM), lane_rep(slopes_a))
    y_b = _sparse_attention(bq.reshape(b, s, -1), iq.reshape(b, s, -1), small.reshape(b, s, -1),
                            kv_norm_g[l].reshape(1, B_KV_RANK), _pad_w_uk(w_uk[l]), _pair_w_uv(w_uv[l]),
                            lane_rep(slopes_b))

    w_router_t = jnp.zeros((ROUTER_ROWS, d), F32).at[:N_GROUPS].set(w_group[l].T).at[
        N_GROUPS:N_GROUPS + N_EXPERTS].set(w_expert_router[l].T)
    b_router = jnp.zeros((ROUTER_ROWS, 1), F32).at[:N_GROUPS, 0].set(b_group[l]).at[
        N_GROUPS:N_GROUPS + N_EXPERTS, 0].set(b_expert_router[l])
    x1, u2, route_t, cnt = _merge(y_a.reshape(b * s, A_WIDTH), y_b.reshape(b * s, B_WIDTH), gates, x2, mod3,
                                  w_a_proj[l].astype(BF16), w_b_proj[l].astype(BF16), w_o[l].astype(BF16),
                                  ln1_g[l].reshape(1, d), ln1_b[l].reshape(1, d), w_router_t, b_router)

    tile_group, n_used, ends, dest3, n_slots = _routing_tables(route_t, cnt)
    wg, wu, wd = w_exp_gate[l].astype(BF16), w_exp_up[l].astype(BF16), w_exp_down[l].astype(BF16)
    y_sorted = _moe(tile_group, n_used, _dispatch(ends, dest3, u2, n_slots), wg, wu, wd)
    out = _final(dest3, y_sorted, x1, mod3, ln2_g[l].reshape(1, d), ln2_b[l].reshape(1, d))
    return out.reshape(b, s, d)
```

```python
import functools
import math

import numpy as np
import jax
import jax.numpy as jnp
from jax import lax
from jax.experimental import pallas as pl
from jax.experimental.pallas import tpu as pltpu

D_MODEL = 1024
SEQ = 2048
CHUNK = 64
A_QK_DIM = 64
A_V_DIM = 128
A_HEADS = 4
A_WIDTH = A_HEADS * A_V_DIM
B_HEAD_DIM = 64
B_HEADS = 8
B_WIDTH = B_HEADS * B_HEAD_DIM
B_KV_RANK = 128
IDX_HEADS = 8
IDX_DIM = 64
TOPK = 256
N_ALIBI_HEADS = A_HEADS + B_HEADS
N_GROUPS = 4
EXPERTS_PER_GROUP = 4
N_EXPERTS = 16
D_FF_EXPERT = 256
LN_EPS = 1e-5
RMS_EPS = 1e-5
DEPTH = 1
DEEPNORM_ALPHA = (2.0 * DEPTH) ** 0.25
LAM_INIT = 0.8 - 0.6 * math.exp(-0.3 * 0)
LOG2E = math.log2(math.e)

LANES = 128
VMEM_LIMIT_BYTES = 56 * 1024 * 1024

F32 = jnp.float32
BF16 = jnp.bfloat16
NEG_INF = float("-inf")

_NT = (((1,), (1,)), ((), ()))


def _dot(a, b, **kw):
    return jnp.dot(a, b, preferred_element_type=F32, **kw)


def _dot_nt(a, b, **kw):
    return lax.dot_general(a, b, _NT, preferred_element_type=F32, **kw)


def _alibi_slopes():
    n = N_ALIBI_HEADS
    slopes = (2.0 ** (-8.0 * np.arange(1, n + 1) / n)).astype(np.float32)
    a_idx = np.arange(A_HEADS) * (n // A_HEADS)
    b_idx = np.setdiff1d(np.arange(n), a_idx)
    return slopes[a_idx], slopes[b_idx]


def _layer_norm(h, g, b):
    mu = jnp.mean(h, axis=-1, keepdims=True)
    d = h - mu
    var = jnp.mean(d * d, axis=-1, keepdims=True)
    return d * lax.rsqrt(var + LN_EPS) * g + b


KEY_SLAB = 64


def _over_keys(op, x):
    part = op(x.reshape(x.shape[0] // KEY_SLAB, KEY_SLAB, x.shape[1]), axis=0)
    return op(part, axis=0, keepdims=True)


def _params(sem):
    return pltpu.CompilerParams(dimension_semantics=sem, vmem_limit_bytes=VMEM_LIMIT_BYTES)


MOD_TN = 1536


def _mod_kernel(c_ref, w_ref, b_ref, o_ref):
    c = c_ref[...]
    cond = c * jax.nn.sigmoid(c)
    o_ref[...] = _dot(cond, w_ref[...], precision=lax.Precision.HIGHEST) + b_ref[...]


def _modulation(c, ada_w, ada_b):
    b, d = c.shape
    n = ada_w.shape[1]
    return pl.pallas_call(
        _mod_kernel,
        out_shape=jax.ShapeDtypeStruct((b, n), F32),
        grid=(n // MOD_TN,),
        in_specs=[
            pl.BlockSpec((b, d), lambda j: (0, 0)),
            pl.BlockSpec((d, MOD_TN), lambda j: (0, j)),
            pl.BlockSpec((1, MOD_TN), lambda j: (0, j)),
        ],
        out_specs=pl.BlockSpec((b, MOD_TN), lambda j: (0, j)),
        compiler_params=_params(("arbitrary",)),
        name="modulation",
    )(c, ada_w, ada_b.reshape(1, n))


PROJ_TM = 512
W_QKV = 3 * A_WIDTH
W_SMALL = 3 * LANES
PROJ_COLS = (W_QKV, B_WIDTH, IDX_HEADS * IDX_DIM, 2 * D_MODEL, W_SMALL)


def _proj_kernel(x_ref, mod_ref, w_ref, qkv_ref, bq_ref, iq_ref, gates_ref, small_ref):
    m = mod_ref[0]
    u = (x_ref[...] * (1.0 + m[1:2]) + m[0:1]).astype(BF16)
    off = 0
    for o_ref, n in zip((qkv_ref, bq_ref, iq_ref, gates_ref, small_ref), PROJ_COLS):
        o_ref[...] = _dot(u, w_ref[:, off:off + n]).astype(o_ref.dtype)
        off += n


def _projection(x2, mod3, w_cat):
    n_tok, d = x2.shape
    tiles_per_batch = SEQ // PROJ_TM
    out_dtypes = (BF16, BF16, BF16, BF16, F32)
    return pl.pallas_call(
        _proj_kernel,
        out_shape=[jax.ShapeDtypeStruct((n_tok, n), dt) for n, dt in zip(PROJ_COLS, out_dtypes)],
        grid=(n_tok // PROJ_TM,),
        in_specs=[
            pl.BlockSpec((PROJ_TM, d), lambda i: (i, 0)),
            pl.BlockSpec((1, 6, d), lambda i: (i // tiles_per_batch, 0, 0)),
            pl.BlockSpec(w_cat.shape, lambda i: (0, 0)),
        ],
        out_specs=[pl.BlockSpec((PROJ_TM, n), lambda i: (i, 0)) for n in PROJ_COLS],
        compiler_params=_params(("arbitrary",)),
        name="projection",
    )(x2, mod3, w_cat)


A_TQ = 256


def _diff_attn_kernel(q_ref, k_ref, v_ref, lam_ref, g_ref, slope_ref, o_ref, bias_ref):
    slope = slope_ref[0][:, 0:1]
    r = lax.broadcasted_iota(jnp.int32, (A_TQ, SEQ), 0)
    j = lax.broadcasted_iota(jnp.int32, (A_TQ, SEQ), 1)
    dist = jnp.abs(r + (SEQ - A_TQ) - j).astype(F32)
    visible = (j - (SEQ - A_TQ)) // CHUNK <= r // CHUNK
    bias_ref[...] = jnp.where(visible, (-LOG2E) * slope * dist, NEG_INF)

    lv = lam_ref[...]
    lam = (jnp.exp(jnp.sum(lv[0:1] * lv[1:2], axis=1, keepdims=True))
           - jnp.exp(jnp.sum(lv[2:3] * lv[3:4], axis=1, keepdims=True)) + LAM_INIT)
    lane = lax.broadcasted_iota(jnp.int32, (A_TQ, 2 * A_QK_DIM), 1)
    for i in range(SEQ // A_TQ):
        q0 = i * A_TQ
        kv = q0 + A_TQ
        q = q_ref[0, q0:q0 + A_TQ, :]
        k = k_ref[0, 0:kv, :]
        bias = bias_ref[:, SEQ - kv:SEQ]
        ps, inv_ls = [], []
        for mth in range(2):
            qm = jnp.where((lane // A_QK_DIM) == mth, q, jnp.zeros_like(q))
            s = _dot_nt(qm, k) + bias
            mx = jnp.max(s, axis=-1, keepdims=True)
            p = jnp.exp2(s - mx)
            ps.append(p)
            inv_ls.append(1.0 / jnp.sum(p, axis=-1, keepdims=True))
        attn = ps[0] * inv_ls[0] - ps[1] * (lam * inv_ls[1])
        o = _dot(attn.astype(BF16), v_ref[0, 0:kv, :])
        y = o * lax.rsqrt(jnp.mean(o * o, axis=-1, keepdims=True) + RMS_EPS) * g_ref[...]
        o_ref[0, q0:q0 + A_TQ, :] = (y * (1.0 - LAM_INIT)).astype(o_ref.dtype)


def _diff_attention(qkv3, lam_vecs, subln_g, slopes):
    b, s, _ = qkv3.shape
    blk = (1, s, A_V_DIM)
    return pl.pallas_call(
        _diff_attn_kernel,
        out_shape=jax.ShapeDtypeStruct((b, s, A_WIDTH), BF16),
        grid=(b, A_HEADS),
        in_specs=[
            pl.BlockSpec(blk, lambda bi, h: (bi, 0, h)),
            pl.BlockSpec(blk, lambda bi, h: (bi, 0, A_HEADS + h)),
            pl.BlockSpec(blk, lambda bi, h: (bi, 0, 2 * A_HEADS + h)),
            pl.BlockSpec(lam_vecs.shape, lambda bi, h: (0, 0)),
            pl.BlockSpec((1, A_V_DIM), lambda bi, h: (0, 0)),
            pl.BlockSpec((1, 1, LANES), lambda bi, h: (h, 0, 0)),
        ],
        out_specs=pl.BlockSpec(blk, lambda bi, h: (bi, 0, h)),
        scratch_shapes=[pltpu.VMEM((A_TQ, SEQ), F32)],
        compiler_params=_params(("arbitrary", "arbitrary")),
        name="diff_attention",
    )(qkv3, qkv3, qkv3, lam_vecs, subln_g, slopes)


B_TQ = 128
SMALL_KV = slice(0, 128)
SMALL_IK = slice(128, 256)
SMALL_IW = slice(256, 384)
BISECT_FIRST = 18
BISECT_MORE = 3
BISECT_MAX_ROUNDS = 80


def _key_count(mask_f32):
    return _over_keys(jnp.sum, mask_f32)


def _topk_mask(score_ref, selb_ref, kv):
    s = score_ref[0:kv, :]

    def bisect(_, st):
        lo, hi, n_lo = st
        probe = 0.5 * lo + 0.5 * hi
        n = _key_count(jnp.where(score_ref[0:kv, :] >= probe, 1.0, 0.0))
        ge = n >= TOPK
        return jnp.where(ge, probe, lo), jnp.where(ge, hi, probe), jnp.where(ge, n, n_lo)

    def inexact(n_lo):
        return jnp.sum(jnp.where(n_lo != TOPK, 1.0, 0.0))

    def candidate(lo):
        sc = score_ref[0:kv, :]
        t_val = _over_keys(jnp.min, jnp.where(sc >= lo, sc, jnp.inf))
        n_gt = _key_count(jnp.where(sc > t_val, 1.0, 0.0))
        return t_val, n_gt, jnp.sum(jnp.where(n_gt >= TOPK, 1.0, 0.0))

    def check(st):
        none = (jnp.zeros((1, B_TQ), F32), jnp.zeros((1, B_TQ), F32), jnp.float32(0.0))
        return lax.cond(inexact(st[2]) == 0.0, lambda: none, lambda: candidate(st[0]))

    def refine(c):
        st = lax.fori_loop(0, BISECT_MORE, bisect, c[1:4])
        return (c[0] + 1,) + st + check(st)

    lo = _over_keys(jnp.min, jnp.where(s == NEG_INF, jnp.inf, s))
    state = (lo, _over_keys(jnp.max, s), _key_count(jnp.where(s >= lo, 1.0, 0.0)))
    state = lax.fori_loop(0, BISECT_FIRST, bisect, state)
    out = lax.while_loop(lambda c: (c[6] > 0.0) & (c[0] < BISECT_MAX_ROUNDS), refine,
                         (jnp.int32(0),) + state + check(state))
    lo, n_lo, t_val, n_gt = out[1], out[3], out[4], out[5]
    all_exact = inexact(n_lo) == 0.0

    @pl.when(all_exact)
    def _():
        selb_ref[0:kv, :] = jnp.where(s >= lo, 0.0, NEG_INF)

    @pl.when(jnp.logical_not(all_exact))
    def _():
        gt = s > t_val
        eqf = jnp.where(s == t_val, 1.0, 0.0)
        need = TOPK - n_gt
        selb_ref[0:kv, :] = jnp.where(s >= t_val, 0.0, NEG_INF)
        tie_overflow = jnp.max(jnp.abs(_key_count(eqf) - need))

        @pl.when(tie_overflow > 0.0)
        def _():
            idx = lax.broadcasted_iota(jnp.int32, (kv, B_TQ), 0)

            def idx_step(b, j):
                c = j | lax.shift_left(jnp.int32(1), 10 - b)
                before = _key_count(jnp.where(idx < c, eqf, 0.0))
                return jnp.where(before < need, c, j)

            j_max = lax.fori_loop(0, 11, idx_step, jnp.zeros((1, B_TQ), jnp.int32))
            keep = gt | ((eqf > 0.0) & (idx <= j_max))
            selb_ref[0:kv, :] = jnp.where(keep, 0.0, NEG_INF)


def _sparse_attn_kernel(bq_ref, iq_ref, small_ref, kvg_ref, wuk_ref, wuvt_ref, slope_ref, o_ref,
                        bias_ref, ckv_ref, ckvt_ref, ik_ref, score_ref, selb_ref):
    @pl.when(pl.program_id(0) == 0)
    def _():
        j = lax.broadcasted_iota(jnp.int32, (SEQ, B_TQ), 0)
        r = lax.broadcasted_iota(jnp.int32, (SEQ, B_TQ), 1)
        dist = jnp.abs(r + (SEQ - B_TQ) - j).astype(F32)
        for h in range(B_HEADS):
            bias_ref[h] = (-LOG2E) * slope_ref[h][:, 0:1] * dist

    kv_lat = small_ref[0, :, SMALL_KV]
    ckv = kv_lat * lax.rsqrt(jnp.mean(kv_lat * kv_lat, axis=-1, keepdims=True) + RMS_EPS) * kvg_ref[...]
    ckv_ref[...] = ckv.astype(BF16)
    ckvt_ref[...] = ckv.T.astype(BF16)
    ik_ref[...] = small_ref[0, :, SMALL_IK].astype(BF16)

    lane = lax.broadcasted_iota(jnp.int32, (B_TQ, LANES), 1)
    kk = lax.broadcasted_iota(jnp.int32, (B_TQ, B_TQ), 0)
    qq = lax.broadcasted_iota(jnp.int32, (B_TQ, B_TQ), 1)
    diag_visible = kk // CHUNK <= qq // CHUNK
    n_pairs = B_HEADS // 2

    def half_masked(pair):
        z = jnp.zeros_like(pair)
        return jnp.concatenate([jnp.where(lane < IDX_DIM, pair, z), jnp.where(lane >= IDX_DIM, pair, z)], axis=0)

    for i in range(SEQ // B_TQ):
        q0 = i * B_TQ
        kv = q0 + B_TQ
        rows = slice(q0, q0 + B_TQ)
        if kv <= TOPK:
            if q0:
                selb_ref[0:q0, :] = jnp.zeros((q0, B_TQ), F32)
            selb_ref[q0:kv, :] = jnp.where(diag_visible, 0.0, NEG_INF)
        else:
            iq = iq_ref[0, rows, :]
            iw_t = small_ref[0, rows, SMALL_IW].T
            ik = ik_ref[0:kv, :]
            heads = jnp.concatenate([half_masked(iq[:, jp * LANES:(jp + 1) * LANES]) for jp in range(n_pairs)], axis=0)
            x = _dot_nt(ik, heads)
            score = iw_t[0:1] * jnp.maximum(x[:, 0:B_TQ], 0.0)
            for h in range(1, IDX_HEADS):
                score = score + iw_t[h:h + 1] * jnp.maximum(x[:, h * B_TQ:(h + 1) * B_TQ], 0.0)
            score_ref[0:q0, :] = score[0:q0]
            score_ref[q0:kv, :] = jnp.where(diag_visible, score[q0:kv], NEG_INF)
            _topk_mask(score_ref, selb_ref, kv)

        bq = bq_ref[0, rows, :]
        q_abs = [_dot(bq[:, (h // 2) * LANES:(h // 2 + 1) * LANES], wuk_ref[h]).astype(BF16)
                 for h in range(B_HEADS)]
        s_all = _dot_nt(ckv_ref[0:kv, :], jnp.concatenate(q_abs, axis=0))
        selb = selb_ref[0:kv, :]
        probs, inv_l = [], []
        for h in range(B_HEADS):
            sh = s_all[:, h * B_TQ:(h + 1) * B_TQ] + bias_ref[h, SEQ - kv:SEQ, :] + selb
            p = jnp.exp2(sh - _over_keys(jnp.max, sh))
            inv_l.append(1.0 / _over_keys(jnp.sum, p))
            probs.append(p.astype(BF16))
        lat_t = _dot(ckvt_ref[:, 0:kv], jnp.concatenate(probs, axis=1))
        outs = []
        for jp in range(n_pairs):
            pair = jnp.concatenate([lat_t[:, h * B_TQ:(h + 1) * B_TQ] * inv_l[h] for h in (2 * jp, 2 * jp + 1)],
                                   axis=0).astype(BF16)
            outs.append(_dot(wuvt_ref[jp], pair))
        o_ref[0, rows, :] = jnp.concatenate(outs, axis=0).T.astype(o_ref.dtype)


def _sparse_attention(bq3, iq3, small3, kv_norm_g, wuk_pad, wuvt_pair, slopes):
    b, s, _ = bq3.shape
    return pl.pallas_call(
        _sparse_attn_kernel,
        out_shape=jax.ShapeDtypeStruct((b, s, B_WIDTH), BF16),
        grid=(b,),
        in_specs=[
            pl.BlockSpec((1, s, B_WIDTH), lambda bi: (bi, 0, 0)),
            pl.BlockSpec((1, s, IDX_HEADS * IDX_DIM), lambda bi: (bi, 0, 0)),
            pl.BlockSpec((1, s, W_SMALL), lambda bi: (bi, 0, 0)),
            pl.BlockSpec((1, B_KV_RANK), lambda bi: (0, 0)),
            pl.BlockSpec(wuk_pad.shape, lambda bi: (0, 0, 0)),
            pl.BlockSpec(wuvt_pair.shape, lambda bi: (0, 0, 0)),
            pl.BlockSpec(slopes.shape, lambda bi: (0, 0, 0)),
        ],
        out_specs=pl.BlockSpec((1, s, B_WIDTH), lambda bi: (bi, 0, 0)),
        scratch_shapes=[
            pltpu.VMEM((B_HEADS, SEQ, B_TQ), F32),
            pltpu.VMEM((SEQ, B_KV_RANK), BF16),
            pltpu.VMEM((B_KV_RANK, SEQ), BF16),
            pltpu.VMEM((SEQ, LANES), BF16),
            pltpu.VMEM((SEQ, B_TQ), F32),
            pltpu.VMEM((SEQ, B_TQ), F32),
        ],
        compiler_params=_params(("arbitrary",)),
        name="sparse_attention",
    )(bq3, iq3, small3, kv_norm_g, wuk_pad, wuvt_pair, slopes)


MERGE_TM = 512
ROUTER_ROWS = 32
ROUTE_ROWS = 8
ROW_W = D_MODEL + LANES


def _first_max_onehot(rows):
    mx = rows[0]
    for r in rows[1:]:
        mx = jnp.maximum(mx, r)
    taken = jnp.zeros_like(mx)
    hot = []
    for r in rows:
        h = jnp.where((r == mx) & (taken == 0.0), 1.0, 0.0)
        taken = taken + h
        hot.append(h)
    return hot, mx


def _softmax_rows(rows):
    mx = rows[0]
    for r in rows[1:]:
        mx = jnp.maximum(mx, r)
    e = [jnp.exp(r - mx) for r in rows]
    tot = e[0]
    for r in e[1:]:
        tot = tot + r
    return [r / tot for r in e]


def _merge_kernel(ya_ref, yb_ref, gates_ref, x_ref, mod_ref, wa_ref, wb_ref, wo_ref, g1_ref, b1_ref,
                  wr_ref, br_ref, tri_ref, x1_ref, u2_ref, route_ref, cnt_ref, run_ref):
    m = mod_ref[0]
    pa = _dot(ya_ref[...], wa_ref[...])
    pb = _dot(yb_ref[...], wb_ref[...])
    gt = jax.nn.sigmoid(gates_ref[...].astype(F32))
    mixed = gt[:, 0:D_MODEL] * pa + gt[:, D_MODEL:2 * D_MODEL] * pb
    z = _dot(mixed.astype(BF16), wo_ref[...])
    x1 = _layer_norm(DEEPNORM_ALPHA * x_ref[...] + m[2:3] * z, g1_ref[...], b1_ref[...])
    x1_ref[...] = x1
    u2 = x1 * (1.0 + m[4:5]) + m[3:4]
    u2_ref[:, 0:D_MODEL] = u2

    logits = _dot_nt(wr_ref[...], u2, precision=lax.Precision.HIGHEST) + br_ref[...]
    g_prob = _softmax_rows([logits[k:k + 1] for k in range(N_GROUPS)])
    g_hot, g_top = _first_max_onehot(g_prob)
    e_logit = []
    for jx in range(EXPERTS_PER_GROUP):
        acc = jnp.zeros_like(g_top)
        for g in range(N_GROUPS):
            row = N_GROUPS + g * EXPERTS_PER_GROUP + jx
            acc = acc + logits[row:row + 1] * g_hot[g]
        e_logit.append(acc)
    e_prob = _softmax_rows(e_logit)
    hot1, p1 = _first_max_onehot(e_prob)
    rest = [jnp.where(h > 0.0, NEG_INF, p) for h, p in zip(hot1, e_prob)]
    hot2, p2 = _first_max_onehot(rest)
    tot = p1 + p2
    w1 = g_top * (p1 / tot)
    w2 = g_top * (p2 / tot)

    @pl.when(pl.program_id(0) == 0)
    def _():
        run_ref[...] = jnp.zeros_like(run_ref)

    gid = g_hot[1] + 2.0 * g_hot[2] + 3.0 * g_hot[3]
    grp = lax.broadcasted_iota(jnp.int32, (ROUTE_ROWS, MERGE_TM), 0).astype(F32)
    hot8 = jnp.where(grp == gid, 1.0, 0.0)
    before = _dot(hot8.astype(BF16), tri_ref[...])
    rank = jnp.sum(hot8 * (run_ref[:, 0:1] + before), axis=0, keepdims=True)
    record = jnp.concatenate([gid] + [w1 * hot1[jx] + w2 * hot2[jx] for jx in range(EXPERTS_PER_GROUP)]
                             + [rank, jnp.zeros((LANES - 2 - EXPERTS_PER_GROUP, MERGE_TM), F32)], axis=0)
    route_ref[...] = record[0:ROUTE_ROWS]
    u2_ref[:, D_MODEL:D_MODEL + LANES] = record.T
    run_ref[...] = run_ref[...] + jnp.sum(hot8, axis=1, keepdims=True)
    cnt_ref[...] = run_ref[...]


def _merge(ya, yb, gates, x2, mod3, wa, wb, wo, ln_g, ln_b, w_router_t, b_router):
    n_tok, d = x2.shape
    tiles_per_batch = SEQ // MERGE_TM
    tok = lambda n: pl.BlockSpec((MERGE_TM, n), lambda i: (i, 0))
    full = lambda a: pl.BlockSpec(a.shape, lambda i: (0,) * a.ndim)
    tri = jnp.asarray(np.triu(np.ones((MERGE_TM, MERGE_TM), np.float32), k=1), BF16)
    return pl.pallas_call(
        _merge_kernel,
        out_shape=[
            jax.ShapeDtypeStruct((n_tok, d), F32),
            jax.ShapeDtypeStruct((n_tok, ROW_W), F32),
            jax.ShapeDtypeStruct((ROUTE_ROWS, n_tok), F32),
            jax.ShapeDtypeStruct((ROUTE_ROWS, LANES), F32),
        ],
        grid=(n_tok // MERGE_TM,),
        in_specs=[
            tok(A_WIDTH), tok(B_WIDTH), tok(2 * d), tok(d),
            pl.BlockSpec((1, 6, d), lambda i: (i // tiles_per_batch, 0, 0)),
            full(wa), full(wb), full(wo), full(ln_g), full(ln_b), full(w_router_t), full(b_router), full(tri),
        ],
        out_specs=[tok(d), tok(ROW_W), pl.BlockSpec((ROUTE_ROWS, MERGE_TM), lambda i: (0, i)),
                   pl.BlockSpec((ROUTE_ROWS, LANES), lambda i: (0, 0))],
        scratch_shapes=[pltpu.VMEM((ROUTE_ROWS, LANES), F32)],
        compiler_params=_params(("arbitrary",)),
        name="merge_router",
    )(ya, yb, gates, x2, mod3, wa, wb, wo, ln_g, ln_b, w_router_t, b_router, tri)


def _start_row_gather(src_hbm, idx_ref, buf, slot, sem, n_rows):
    def issue(r, carry):
        pltpu.make_async_copy(src_hbm.at[pl.ds(idx_ref[0, 0, r], 1)], buf.at[slot, pl.ds(r, 1)], sem.at[slot]).start()
        return carry

    lax.fori_loop(0, n_rows, issue, 0, unroll=8)


def _wait_row_gather(buf, slot, sem):
    pltpu.make_async_copy(buf.at[slot], buf.at[slot], sem.at[slot]).wait()


MOE_TM = 512


def _dispatch_kernel(ends_ref, idx_ref, rows_ref, sorted_hbm, zero_buf, stage, sem, zsem):
    i = pl.program_id(0)
    slot = i % 2

    @pl.when(i == 0)
    def _():
        zero_buf[...] = jnp.zeros_like(zero_buf)
        for g in range(N_GROUPS):
            start = pl.multiple_of(ends_ref[g] - MOE_TM, MOE_TM)
            nonempty = ends_ref[g] > (ends_ref[g - 1] if g else 0)

            @pl.when(nonempty)
            def _():
                cp = pltpu.make_async_copy(zero_buf, sorted_hbm.at[pl.ds(start, MOE_TM)], zsem)
                cp.start()
                cp.wait()

        for k in range(N_GROUPS):
            spare = sorted_hbm.shape[0] - (k + 1) * MOE_TM

            @pl.when(spare >= ends_ref[N_GROUPS - 1])
            def _():
                cp = pltpu.make_async_copy(zero_buf, sorted_hbm.at[pl.ds(spare, MOE_TM)], zsem)
                cp.start()
                cp.wait()

    stage[slot] = rows_ref[...]

    def issue(r, carry):
        pltpu.make_async_copy(stage.at[slot, pl.ds(r, 1)], sorted_hbm.at[pl.ds(idx_ref[0, 0, r], 1)], sem.at[slot]).start()
        return carry

    lax.fori_loop(0, MOE_TM, issue, 0, unroll=8)

    def wait_slot(s):
        pltpu.make_async_copy(stage.at[s], sorted_hbm.at[pl.ds(0, MOE_TM)], sem.at[s]).wait()

    @pl.when(i > 0)
    def _():
        wait_slot(1 - slot)

    @pl.when(i == pl.num_programs(0) - 1)
    def _():
        wait_slot(slot)


def _dispatch(ends, dest3, rows, n_slots):
    n_steps = dest3.shape[0]
    w = rows.shape[1]
    return pl.pallas_call(
        _dispatch_kernel,
        out_shape=jax.ShapeDtypeStruct((n_slots, w), rows.dtype),
        grid_spec=pltpu.PrefetchScalarGridSpec(
            num_scalar_prefetch=1,
            grid=(n_steps,),
            in_specs=[
                pl.BlockSpec((1, 1, MOE_TM), lambda i, ends: (i, 0, 0), memory_space=pltpu.SMEM),
                pl.BlockSpec((MOE_TM, w), lambda i, ends: (i, 0)),
            ],
            out_specs=pl.BlockSpec(memory_space=pl.ANY),
            scratch_shapes=[pltpu.VMEM((MOE_TM, w), rows.dtype), pltpu.VMEM((2, MOE_TM, w), rows.dtype),
                            pltpu.SemaphoreType.DMA((2,)), pltpu.SemaphoreType.DMA(())],
        ),
        compiler_params=_params(("arbitrary",)),
        name="dispatch",
    )(ends, dest3, rows)


def _moe_kernel(tg_ref, nu_ref, x_ref, wg_ref, wu_ref, wd_ref, y_ref):
    j = pl.program_id(0)
    n_used = nu_ref[0]

    @pl.when(j < n_used)
    def _():
        u = x_ref[:, 0:D_MODEL].astype(BF16)
        comb = x_ref[:, D_MODEL + 1:D_MODEL + 1 + EXPERTS_PER_GROUP]
        y = None
        for jx in range(EXPERTS_PER_GROUP):
            h = jax.nn.silu(_dot(u, wg_ref[jx])) * _dot(u, wu_ref[jx])
            yj = _dot((h * comb[:, jx:jx + 1]).astype(BF16), wd_ref[jx])
            y = yj if y is None else y + yj
        y_ref[...] = y

    @pl.when(j >= n_used)
    def _():
        y_ref[...] = jnp.zeros_like(y_ref)


def _moe(tile_group, n_used, rows_sorted, wg, wu, wd):
    n_tiles = rows_sorted.shape[0] // MOE_TM
    group_of = lambda w: pl.BlockSpec((EXPERTS_PER_GROUP,) + w.shape[1:], lambda j, tg, nu: (tg[j], 0, 0))
    return pl.pallas_call(
        _moe_kernel,
        out_shape=jax.ShapeDtypeStruct((n_tiles * MOE_TM, D_MODEL), F32),
        grid_spec=pltpu.PrefetchScalarGridSpec(
            num_scalar_prefetch=2,
            grid=(n_tiles,),
            in_specs=[
                pl.BlockSpec((MOE_TM, ROW_W), lambda j, tg, nu: (jnp.minimum(j, jnp.maximum(nu[0] - 1, 0)), 0)),
                group_of(wg), group_of(wu), group_of(wd),
            ],
            out_specs=pl.BlockSpec((MOE_TM, D_MODEL), lambda j, tg, nu: (j, 0)),
        ),
        compiler_params=_params(("arbitrary",)),
        name="moe",
    )(tile_group, n_used, rows_sorted, wg, wu, wd)


FINAL_TM = 512


def _final_kernel(idx_ref, idx_next_ref, y_hbm, x1_ref, mod_ref, g2_ref, b2_ref, o_ref, ybuf, sem):
    i = pl.program_id(0)
    slot = i % 2

    @pl.when(i == 0)
    def _():
        _start_row_gather(y_hbm, idx_ref, ybuf, 0, sem, FINAL_TM)

    @pl.when(i + 1 < pl.num_programs(0))
    def _():
        _start_row_gather(y_hbm, idx_next_ref, ybuf, 1 - slot, sem, FINAL_TM)

    _wait_row_gather(ybuf, slot, sem)
    m = mod_ref[0]
    o_ref[...] = _layer_norm(DEEPNORM_ALPHA * x1_ref[...] + m[5:6] * ybuf[slot], g2_ref[...], b2_ref[...])


def _final(dest3, y_sorted, x1, mod3, ln_g, ln_b):
    n_tok, d = x1.shape
    n_tiles = n_tok // FINAL_TM
    tiles_per_batch = SEQ // FINAL_TM
    idx_spec = lambda f: pl.BlockSpec((1, 1, FINAL_TM), f, memory_space=pltpu.SMEM)
    vec = pl.BlockSpec((1, d), lambda i: (0, 0))
    return pl.pallas_call(
        _final_kernel,
        out_shape=jax.ShapeDtypeStruct((n_tok, d), F32),
        grid=(n_tiles,),
        in_specs=[
            idx_spec(lambda i: (i, 0, 0)),
            idx_spec(lambda i: (jnp.minimum(i + 1, n_tiles - 1), 0, 0)),
            pl.BlockSpec(memory_space=pl.ANY),
            pl.BlockSpec((FINAL_TM, d), lambda i: (i, 0)),
            pl.BlockSpec((1, 6, d), lambda i: (i // tiles_per_batch, 0, 0)),
            vec, vec,
        ],
        out_specs=pl.BlockSpec((FINAL_TM, d), lambda i: (i, 0)),
        scratch_shapes=[pltpu.VMEM((2, FINAL_TM, d), F32), pltpu.SemaphoreType.DMA((2,))],
        compiler_params=_params(("arbitrary",)),
        name="combine_norm",
    )(dest3, dest3, y_sorted, x1, mod3, ln_g, ln_b)


def _routing_tables(route_t, cnt):
    n_tok = route_t.shape[1]
    n_tiles = n_tok // MOE_TM + N_GROUPS
    counts = cnt[:N_GROUPS, 0].astype(jnp.int32)
    padded = (counts + MOE_TM - 1) // MOE_TM * MOE_TM
    ends = jnp.cumsum(padded)
    gid = route_t[0].astype(jnp.int32)
    dest = (ends - padded)[gid] + route_t[5].astype(jnp.int32)
    tile_start = jnp.arange(n_tiles, dtype=jnp.int32) * MOE_TM
    tile_group = jnp.minimum(jnp.sum((tile_start[:, None] >= ends[None, :]).astype(jnp.int32), axis=1), N_GROUPS - 1)
    n_used = (ends[-1:] // MOE_TM).astype(jnp.int32)
    return tile_group, n_used, ends.astype(jnp.int32), dest.reshape(-1, 1, MOE_TM), n_tiles * MOE_TM


def _regroup_w_in(w):
    o = np.cumsum((512, 512, 512, 512, 128, 512, 64, 8, 2048))
    a_q, a_kv, b_q, b_kv = w[:, :o[0]], w[:, o[0]:o[2]], w[:, o[2]:o[3]], w[:, o[3]:o[4]]
    i_q, i_k, i_w, gates = w[:, o[4]:o[5]], w[:, o[5]:o[6]], w[:, o[6]:o[7]], w[:, o[7]:o[8]]
    pad = jnp.zeros((w.shape[0], LANES - IDX_HEADS), w.dtype)
    a_q = a_q * (LOG2E * A_QK_DIM ** -0.5)
    return jnp.concatenate([a_q, a_kv, b_q, i_q, gates, b_kv, i_k, i_k, i_w, pad], axis=1).astype(BF16)


def _pad_w_uk(w_uk):
    wt = jnp.transpose(w_uk, (1, 2, 0)) * (LOG2E * B_HEAD_DIM ** -0.5)
    z = jnp.zeros_like(wt)
    even = jnp.concatenate([wt, z], axis=1)
    odd = jnp.concatenate([z, wt], axis=1)
    sel = (jnp.arange(B_HEADS) % 2 == 0)[:, None, None]
    return jnp.where(sel, even, odd).astype(BF16)


def _pair_w_uv(w_uv):
    wv = jnp.transpose(w_uv, (1, 0, 2))
    z = jnp.zeros_like(wv[0::2])
    top = jnp.concatenate([wv[0::2], z], axis=2)
    bot = jnp.concatenate([z, wv[1::2]], axis=2)
    return jnp.transpose(jnp.concatenate([top, bot], axis=1), (0, 2, 1)).astype(BF16)


def kernel(x, c, ada_w, ada_b, w_in, lambda_q1, lambda_k1, lambda_q2, lambda_k2, a_subln_g, kv_norm_g, w_uk, w_uv,
           w_a_proj, w_b_proj, w_o, ln1_g, ln1_b, w_group, b_group, w_expert_router, b_expert_router,
           w_exp_gate, w_exp_up, w_exp_down, ln2_g, ln2_b):
    b, s, d = x.shape
    assert (s, d) == (SEQ, D_MODEL) and ada_w.shape[0] == DEPTH
    slopes_a, slopes_b = _alibi_slopes()
    lane_rep = lambda v: jnp.asarray(np.repeat(v[:, None, None], LANES, axis=2))
    x2 = x.reshape(b * s, d)
    l = 0
    mod3 = _modulation(c, ada_w[l], ada_b[l]).reshape(b, 6, d)

    qkv, bq, iq, gates, small = _projection(x2, mod3, _regroup_w_in(w_in[l]))
    lam_vecs = jnp.stack([lambda_q1[l], lambda_k1[l], lambda_q2[l], lambda_k2[l]]).astype(F32)
    y_a = _diff_attention(qkv.reshape(b, s, W_QKV), lam_vecs, a_subln_g[l].reshape(1, A_V_DIM), lane_rep(slopes_a))
    y_b = _sparse_attention(bq.reshape(b, s, -1), iq.reshape(b, s, -1), small.reshape(b, s, -1),
                            kv_norm_g[l].reshape(1, B_KV_RANK), _pad_w_uk(w_uk[l]), _pair_w_uv(w_uv[l]),
                            lane_rep(slopes_b))

    w_router_t = jnp.zeros((ROUTER_ROWS, d), F32).at[:N_GROUPS].set(w_group[l].T).at[
        N_GROUPS:N_GROUPS + N_EXPERTS].set(w_expert_router[l].T)
    b_router = jnp.zeros((ROUTER_ROWS, 1), F32).at[:N_GROUPS, 0].set(b_group[l]).at[
        N_GROUPS:N_GROUPS + N_EXPERTS, 0].set(b_expert_router[l])
    x1, u2, route_t, cnt = _merge(y_a.reshape(b * s, A_WIDTH), y_b.reshape(b * s, B_WIDTH), gates, x2, mod3,
                                  w_a_proj[l].astype(BF16), w_b_proj[l].astype(BF16), w_o[l].astype(BF16),
                                  ln1_g[l].reshape(1, d), ln1_b[l].reshape(1, d), w_router_t, b_router)

    tile_group, n_used, ends, dest3, n_slots = _routing_tables(route_t, cnt)
    wg, wu, wd = w_exp_gate[l].astype(BF16), w_exp_up[l].astype(BF16), w_exp_down[l].astype(BF16)
    y_sorted = _moe(tile_group, n_used, _dispatch(ends, dest3, u2, n_slots), wg, wu, wd)
    out = _final(dest3, y_sorted, x1, mod3, ln2_g[l].reshape(1, d), ln2_b[l].reshape(1, d))
    return out.reshape(b, s, d)
```

```python
import functools
import math

import numpy as np
import jax
import jax.numpy as jnp
from jax import lax
from jax.experimental import pallas as pl
from jax.experimental.pallas import tpu as pltpu

D_MODEL = 1024
SEQ = 2048
CHUNK = 64
A_QK_DIM = 64
A_V_DIM = 128
A_HEADS = 4
A_WIDTH = A_HEADS * A_V_DIM
B_HEAD_DIM = 64
B_HEADS = 8
B_WIDTH = B_HEADS * B_HEAD_DIM
B_KV_RANK = 128
IDX_HEADS = 8
IDX_DIM = 64
TOPK = 256
N_ALIBI_HEADS = A_HEADS + B_HEADS
N_GROUPS = 4
EXPERTS_PER_GROUP = 4
N_EXPERTS = 16
D_FF_EXPERT = 256
LN_EPS = 1e-5
RMS_EPS = 1e-5
DEPTH = 1
DEEPNORM_ALPHA = (2.0 * DEPTH) ** 0.25
LAM_INIT = 0.8 - 0.6 * math.exp(-0.3 * 0)
LOG2E = math.log2(math.e)

LANES = 128
VMEM_LIMIT_BYTES = 56 * 1024 * 1024

F32 = jnp.float32
BF16 = jnp.bfloat16
NEG_INF = float("-inf")

_NT = (((1,), (1,)), ((), ()))


def _dot(a, b, **kw):
    return jnp.dot(a, b, preferred_element_type=F32, **kw)


def _dot_nt(a, b, **kw):
    return lax.dot_general(a, b, _NT, preferred_element_type=F32, **kw)


def _alibi_slopes():
    n = N_ALIBI_HEADS
    slopes = (2.0 ** (-8.0 * np.arange(1, n + 1) / n)).astype(np.float32)
    a_idx = np.arange(A_HEADS) * (n // A_HEADS)
    b_idx = np.setdiff1d(np.arange(n), a_idx)
    return slopes[a_idx], slopes[b_idx]


def _layer_norm(h, g, b):
    mu = jnp.mean(h, axis=-1, keepdims=True)
    d = h - mu
    var = jnp.mean(d * d, axis=-1, keepdims=True)
    return d * lax.rsqrt(var + LN_EPS) * g + b


KEY_SLAB = 64


def _over_keys(op, x):
    part = op(x.reshape(x.shape[0] // KEY_SLAB, KEY_SLAB, x.shape[1]), axis=0)
    return op(part, axis=0, keepdims=True)


def _params(sem):
    return pltpu.CompilerParams(dimension_semantics=sem, vmem_limit_bytes=VMEM_LIMIT_BYTES)


MOD_TN = 1536


def _mod_kernel(c_ref, w_ref, b_ref, o_ref):
    c = c_ref[...]
    cond = c * jax.nn.sigmoid(c)
    o_ref[...] = _dot(cond, w_ref[...], precision=lax.Precision.HIGHEST) + b_ref[...]


def _modulation(c, ada_w, ada_b):
    b, d = c.shape
    n = ada_w.shape[1]
    return pl.pallas_call(
        _mod_kernel,
        out_shape=jax.ShapeDtypeStruct((b, n), F32),
        grid=(n // MOD_TN,),
        in_specs=[
            pl.BlockSpec((b, d), lambda j: (0, 0)),
            pl.BlockSpec((d, MOD_TN), lambda j: (0, j)),
            pl.BlockSpec((1, MOD_TN), lambda j: (0, j)),
        ],
        out_specs=pl.BlockSpec((b, MOD_TN), lambda j: (0, j)),
        compiler_params=_params(("arbitrary",)),
        name="modulation",
    )(c, ada_w, ada_b.reshape(1, n))


PROJ_TM = 512
W_QKV = 3 * A_WIDTH
W_SMALL = 3 * LANES
PROJ_COLS = (W_QKV, B_WIDTH, IDX_HEADS * IDX_DIM, 2 * D_MODEL, W_SMALL)


def _proj_kernel(x_ref, mod_ref, w_ref, qkv_ref, bq_ref, iq_ref, gates_ref, small_ref):
    m = mod_ref[0]
    u = (x_ref[...] * (1.0 + m[1:2]) + m[0:1]).astype(BF16)
    off = 0
    for o_ref, n in zip((qkv_ref, bq_ref, iq_ref, gates_ref, small_ref), PROJ_COLS):
        o_ref[...] = _dot(u, w_ref[:, off:off + n]).astype(o_ref.dtype)
        off += n


def _projection(x2, mod3, w_cat):
    n_tok, d = x2.shape
    tiles_per_batch = SEQ // PROJ_TM
    out_dtypes = (BF16, BF16, BF16, BF16, F32)
    return pl.pallas_call(
        _proj_kernel,
        out_shape=[jax.ShapeDtypeStruct((n_tok, n), dt) for n, dt in zip(PROJ_COLS, out_dtypes)],
        grid=(n_tok // PROJ_TM,),
        in_specs=[
            pl.BlockSpec((PROJ_TM, d), lambda i: (i, 0)),
            pl.BlockSpec((1, 6, d), lambda i: (i // tiles_per_batch, 0, 0)),
            pl.BlockSpec(w_cat.shape, lambda i: (0, 0)),
        ],
        out_specs=[pl.BlockSpec((PROJ_TM, n), lambda i: (i, 0)) for n in PROJ_COLS],
        compiler_params=_params(("arbitrary",)),
        name="projection",
    )(x2, mod3, w_cat)


A_TQ = 256


def _diff_attn_kernel(q_ref, k_ref, v_ref, lam_ref, g_ref, slope_ref, o_ref, bias_ref):
    slope = slope_ref[0][:, 0:1]
    r = lax.broadcasted_iota(jnp.int32, (A_TQ, SEQ), 0)
    j = lax.broadcasted_iota(jnp.int32, (A_TQ, SEQ), 1)
    dist = jnp.abs(r + (SEQ - A_TQ) - j).astype(F32)
    visible = (j - (SEQ - A_TQ)) // CHUNK <= r // CHUNK
    bias_ref[...] = jnp.where(visible, (-LOG2E) * slope * dist, NEG_INF)

    lv = lam_ref[...]
    lam = (jnp.exp(jnp.sum(lv[0:1] * lv[1:2], axis=1, keepdims=True))
           - jnp.exp(jnp.sum(lv[2:3] * lv[3:4], axis=1, keepdims=True)) + LAM_INIT)
    lane = lax.broadcasted_iota(jnp.int32, (A_TQ, 2 * A_QK_DIM), 1)
    for i in range(SEQ // A_TQ):
        q0 = i * A_TQ
        kv = q0 + A_TQ
        q = q_ref[0, q0:q0 + A_TQ, :]
        k = k_ref[0, 0:kv, :]
        bias = bias_ref[:, SEQ - kv:SEQ]
        ps, inv_ls = [], []
        for mth in range(2):
            qm = jnp.where((lane // A_QK_DIM) == mth, q, jnp.zeros_like(q))
            s = _dot_nt(qm, k) + bias
            mx = jnp.max(s, axis=-1, keepdims=True)
            p = jnp.exp2(s - mx)
            ps.append(p)
            inv_ls.append(1.0 / jnp.sum(p, axis=-1, keepdims=True))
        attn = ps[0] * inv_ls[0] - ps[1] * (lam * inv_ls[1])
        o = _dot(attn.astype(BF16), v_ref[0, 0:kv, :])
        y = o * lax.rsqrt(jnp.mean(o * o, axis=-1, keepdims=True) + RMS_EPS) * g_ref[...]
        o_ref[0, q0:q0 + A_TQ, :] = (y * (1.0 - LAM_INIT)).astype(o_ref.dtype)


def _diff_attention(qkv3, lam_vecs, subln_g, slopes):
    b, s, _ = qkv3.shape
    blk = (1, s, A_V_DIM)
    return pl.pallas_call(
        _diff_attn_kernel,
        out_shape=jax.ShapeDtypeStruct((b, s, A_WIDTH), BF16),
        grid=(b, A_HEADS),
        in_specs=[
            pl.BlockSpec(blk, lambda bi, h: (bi, 0, h)),
            pl.BlockSpec(blk, lambda bi, h: (bi, 0, A_HEADS + h)),
            pl.BlockSpec(blk, lambda bi, h: (bi, 0, 2 * A_HEADS + h)),
            pl.BlockSpec(lam_vecs.shape, lambda bi, h: (0, 0)),
            pl.BlockSpec((1, A_V_DIM), lambda bi, h: (0, 0)),
            pl.BlockSpec((1, 1, LANES), lambda bi, h: (h, 0, 0)),
        ],
        out_specs=pl.BlockSpec(blk, lambda bi, h: (bi, 0, h)),
        scratch_shapes=[pltpu.VMEM((A_TQ, SEQ), F32)],
        compiler_params=_params(("arbitrary", "arbitrary")),
        name="diff_attention",
    )(qkv3, qkv3, qkv3, lam_vecs, subln_g, slopes)


B_TQ = 128
SMALL_KV = slice(0, 128)
SMALL_IK = slice(128, 256)
SMALL_IW = slice(256, 384)
ONES_ROWS = 16
BISECT_FIRST = 18
BISECT_MORE = 3
BISECT_MAX_ROUNDS = 80


def _key_count(mask_f32):
    return _over_keys(jnp.sum, mask_f32)


def _topk_mask(score_ref, selb_ref, kv):
    s = score_ref[0:kv, :]

    def bisect(_, st):
        lo, hi = st
        probe = 0.5 * lo + 0.5 * hi
        ge = _key_count(jnp.where(score_ref[0:kv, :] >= probe, 1.0, 0.0)) >= TOPK
        return jnp.where(ge, probe, lo), jnp.where(ge, hi, probe)

    def candidate(lo):
        sc = score_ref[0:kv, :]
        t_val = _over_keys(jnp.min, jnp.where(sc >= lo, sc, jnp.inf))
        n_gt = _key_count(jnp.where(sc > t_val, 1.0, 0.0))
        return t_val, n_gt, jnp.sum(jnp.where(n_gt >= TOPK, 1.0, 0.0))

    lo = _over_keys(jnp.min, jnp.where(s == NEG_INF, jnp.inf, s))
    hi = _over_keys(jnp.max, s)
    lo, hi = lax.fori_loop(0, BISECT_FIRST, bisect, (lo, hi))

    def unresolved(st):
        return (st[5] > 0.0) & (st[0] < BISECT_MAX_ROUNDS)

    def refine(st):
        lo, hi = lax.fori_loop(0, BISECT_MORE, bisect, (st[1], st[2]))
        return (st[0] + 1, lo, hi) + candidate(lo)

    _, _, _, t_val, n_gt, _ = lax.while_loop(unresolved, refine, (jnp.int32(0), lo, hi) + candidate(lo))
    gt = s > t_val
    eqf = jnp.where(s == t_val, 1.0, 0.0)
    need = TOPK - n_gt
    selb_ref[0:kv, :] = jnp.where(s >= t_val, 0.0, NEG_INF)
    tie_overflow = jnp.max(jnp.abs(_key_count(eqf) - need))

    @pl.when(tie_overflow > 0.0)
    def _():
        idx = lax.broadcasted_iota(jnp.int32, (kv, B_TQ), 0)

        def idx_step(b, j):
            c = j | lax.shift_left(jnp.int32(1), 10 - b)
            before = _key_count(jnp.where(idx < c, eqf, 0.0))
            return jnp.where(before < need, c, j)

        j_max = lax.fori_loop(0, 11, idx_step, jnp.zeros((1, B_TQ), jnp.int32))
        keep = gt | ((eqf > 0.0) & (idx <= j_max))
        selb_ref[0:kv, :] = jnp.where(keep, 0.0, NEG_INF)


def _sparse_attn_kernel(bq_ref, iq_ref, small_ref, kvg_ref, wuk_ref, wuvt_ref, slope_ref, o_ref,
                        bias_ref, ckv_ref, ckvt_ref, ik_ref, score_ref, selb_ref):
    @pl.when(pl.program_id(0) == 0)
    def _():
        j = lax.broadcasted_iota(jnp.int32, (SEQ, B_TQ), 0)
        r = lax.broadcasted_iota(jnp.int32, (SEQ, B_TQ), 1)
        dist = jnp.abs(r + (SEQ - B_TQ) - j).astype(F32)
        for h in range(B_HEADS):
            bias_ref[h] = (-LOG2E) * slope_ref[h][:, 0:1] * dist

    kv_lat = small_ref[0, :, SMALL_KV]
    ckv = kv_lat * lax.rsqrt(jnp.mean(kv_lat * kv_lat, axis=-1, keepdims=True) + RMS_EPS) * kvg_ref[...]
    ckv_ref[...] = ckv.astype(BF16)
    ckvt_ref[0:B_KV_RANK, :] = ckv.T.astype(BF16)
    ckvt_ref[B_KV_RANK:, :] = jnp.where(lax.broadcasted_iota(jnp.int32, (ONES_ROWS, SEQ), 0) == 0, 1.0, 0.0).astype(BF16)
    ik_ref[...] = small_ref[0, :, SMALL_IK].astype(BF16)

    lane = lax.broadcasted_iota(jnp.int32, (B_TQ, LANES), 1)
    kk = lax.broadcasted_iota(jnp.int32, (B_TQ, B_TQ), 0)
    qq = lax.broadcasted_iota(jnp.int32, (B_TQ, B_TQ), 1)
    diag_visible = kk // CHUNK <= qq // CHUNK
    n_pairs = B_HEADS // 2

    def half_masked(pair):
        z = jnp.zeros_like(pair)
        return jnp.concatenate([jnp.where(lane < IDX_DIM, pair, z), jnp.where(lane >= IDX_DIM, pair, z)], axis=0)

    for i in range(SEQ // B_TQ):
        q0 = i * B_TQ
        kv = q0 + B_TQ
        rows = slice(q0, q0 + B_TQ)
        if kv <= TOPK:
            if q0:
                selb_ref[0:q0, :] = jnp.zeros((q0, B_TQ), F32)
            selb_ref[q0:kv, :] = jnp.where(diag_visible, 0.0, NEG_INF)
        else:
            iq = iq_ref[0, rows, :]
            iw_t = small_ref[0, rows, SMALL_IW].T
            ik = ik_ref[0:kv, :]
            heads = jnp.concatenate([half_masked(iq[:, jp * LANES:(jp + 1) * LANES]) for jp in range(n_pairs)], axis=0)
            x = _dot_nt(ik, heads)
            score = iw_t[0:1] * jnp.maximum(x[:, 0:B_TQ], 0.0)
            for h in range(1, IDX_HEADS):
                score = score + iw_t[h:h + 1] * jnp.maximum(x[:, h * B_TQ:(h + 1) * B_TQ], 0.0)
            score_ref[0:q0, :] = score[0:q0]
            score_ref[q0:kv, :] = jnp.where(diag_visible, score[q0:kv], NEG_INF)
            _topk_mask(score_ref, selb_ref, kv)

        bq = bq_ref[0, rows, :]
        q_abs = [_dot(bq[:, (h // 2) * LANES:(h // 2 + 1) * LANES], wuk_ref[h]).astype(BF16)
                 for h in range(B_HEADS)]
        s_all = _dot_nt(ckv_ref[0:kv, :], jnp.concatenate(q_abs, axis=0))
        selb = selb_ref[0:kv, :]
        probs = []
        for h in range(B_HEADS):
            sh = s_all[:, h * B_TQ:(h + 1) * B_TQ] + bias_ref[h, SEQ - kv:SEQ, :] + selb
            probs.append(jnp.exp2(sh - _over_keys(jnp.max, sh)).astype(BF16))
        lat_t = _dot(ckvt_ref[:, 0:kv], jnp.concatenate(probs, axis=1))
        inv_l = 1.0 / lat_t[B_KV_RANK:B_KV_RANK + 1, :]
        outs = []
        for jp in range(n_pairs):
            pair = jnp.concatenate([lat_t[0:B_KV_RANK, h * B_TQ:(h + 1) * B_TQ] * inv_l[:, h * B_TQ:(h + 1) * B_TQ]
                                    for h in (2 * jp, 2 * jp + 1)], axis=0).astype(BF16)
            outs.append(_dot(wuvt_ref[jp], pair))
        o_ref[0, rows, :] = jnp.concatenate(outs, axis=0).T.astype(o_ref.dtype)


def _sparse_attention(bq3, iq3, small3, kv_norm_g, wuk_pad, wuvt_pair, slopes):
    b, s, _ = bq3.shape
    return pl.pallas_call(
        _sparse_attn_kernel,
        out_shape=jax.ShapeDtypeStruct((b, s, B_WIDTH), BF16),
        grid=(b,),
        in_specs=[
            pl.BlockSpec((1, s, B_WIDTH), lambda bi: (bi, 0, 0)),
            pl.BlockSpec((1, s, IDX_HEADS * IDX_DIM), lambda bi: (bi, 0, 0)),
            pl.BlockSpec((1, s, W_SMALL), lambda bi: (bi, 0, 0)),
            pl.BlockSpec((1, B_KV_RANK), lambda bi: (0, 0)),
            pl.BlockSpec(wuk_pad.shape, lambda bi: (0, 0, 0)),
            pl.BlockSpec(wuvt_pair.shape, lambda bi: (0, 0, 0)),
            pl.BlockSpec(slopes.shape, lambda bi: (0, 0, 0)),
        ],
        out_specs=pl.BlockSpec((1, s, B_WIDTH), lambda bi: (bi, 0, 0)),
        scratch_shapes=[
            pltpu.VMEM((B_HEADS, SEQ, B_TQ), F32),
            pltpu.VMEM((SEQ, B_KV_RANK), BF16),
            pltpu.VMEM((B_KV_RANK + ONES_ROWS, SEQ), BF16),
            pltpu.VMEM((SEQ, LANES), BF16),
            pltpu.VMEM((SEQ, B_TQ), F32),
            pltpu.VMEM((SEQ, B_TQ), F32),
        ],
        compiler_params=_params(("arbitrary",)),
        name="sparse_attention",
    )(bq3, iq3, small3, kv_norm_g, wuk_pad, wuvt_pair, slopes)


MERGE_TM = 512
ROUTER_ROWS = 32
ROUTE_ROWS = 8
ROW_W = D_MODEL + LANES


def _first_max_onehot(rows):
    mx = rows[0]
    for r in rows[1:]:
        mx = jnp.maximum(mx, r)
    taken = jnp.zeros_like(mx)
    hot = []
    for r in rows:
        h = jnp.where((r == mx) & (taken == 0.0), 1.0, 0.0)
        taken = taken + h
        hot.append(h)
    return hot, mx


def _softmax_rows(rows):
    mx = rows[0]
    for r in rows[1:]:
        mx = jnp.maximum(mx, r)
    e = [jnp.exp(r - mx) for r in rows]
    tot = e[0]
    for r in e[1:]:
        tot = tot + r
    return [r / tot for r in e]


def _merge_kernel(ya_ref, yb_ref, gates_ref, x_ref, mod_ref, wa_ref, wb_ref, wo_ref, g1_ref, b1_ref,
                  wr_ref, br_ref, tri_ref, x1_ref, u2_ref, route_ref, cnt_ref, run_ref):
    m = mod_ref[0]
    pa = _dot(ya_ref[...], wa_ref[...])
    pb = _dot(yb_ref[...], wb_ref[...])
    gt = jax.nn.sigmoid(gates_ref[...].astype(F32))
    mixed = gt[:, 0:D_MODEL] * pa + gt[:, D_MODEL:2 * D_MODEL] * pb
    z = _dot(mixed.astype(BF16), wo_ref[...])
    x1 = _layer_norm(DEEPNORM_ALPHA * x_ref[...] + m[2:3] * z, g1_ref[...], b1_ref[...])
    x1_ref[...] = x1
    u2 = x1 * (1.0 + m[4:5]) + m[3:4]
    u2_ref[:, 0:D_MODEL] = u2

    logits = _dot_nt(wr_ref[...], u2, precision=lax.Precision.HIGHEST) + br_ref[...]
    g_prob = _softmax_rows([logits[k:k + 1] for k in range(N_GROUPS)])
    g_hot, g_top = _first_max_onehot(g_prob)
    e_logit = []
    for jx in range(EXPERTS_PER_GROUP):
        acc = jnp.zeros_like(g_top)
        for g in range(N_GROUPS):
            row = N_GROUPS + g * EXPERTS_PER_GROUP + jx
            acc = acc + logits[row:row + 1] * g_hot[g]
        e_logit.append(acc)
    e_prob = _softmax_rows(e_logit)
    hot1, p1 = _first_max_onehot(e_prob)
    rest = [jnp.where(h > 0.0, NEG_INF, p) for h, p in zip(hot1, e_prob)]
    hot2, p2 = _first_max_onehot(rest)
    tot = p1 + p2
    w1 = g_top * (p1 / tot)
    w2 = g_top * (p2 / tot)

    @pl.when(pl.program_id(0) == 0)
    def _():
        run_ref[...] = jnp.zeros_like(run_ref)

    gid = g_hot[1] + 2.0 * g_hot[2] + 3.0 * g_hot[3]
    grp = lax.broadcasted_iota(jnp.int32, (ROUTE_ROWS, MERGE_TM), 0).astype(F32)
    hot8 = jnp.where(grp == gid, 1.0, 0.0)
    before = _dot(hot8.astype(BF16), tri_ref[...])
    rank = jnp.sum(hot8 * (run_ref[:, 0:1] + before), axis=0, keepdims=True)
    record = jnp.concatenate([gid] + [w1 * hot1[jx] + w2 * hot2[jx] for jx in range(EXPERTS_PER_GROUP)]
                             + [rank, jnp.zeros((LANES - 2 - EXPERTS_PER_GROUP, MERGE_TM), F32)], axis=0)
    route_ref[...] = record[0:ROUTE_ROWS]
    u2_ref[:, D_MODEL:D_MODEL + LANES] = record.T
    run_ref[...] = run_ref[...] + jnp.sum(hot8, axis=1, keepdims=True)
    cnt_ref[...] = run_ref[...]


def _merge(ya, yb, gates, x2, mod3, wa, wb, wo, ln_g, ln_b, w_router_t, b_router):
    n_tok, d = x2.shape
    tiles_per_batch = SEQ // MERGE_TM
    tok = lambda n: pl.BlockSpec((MERGE_TM, n), lambda i: (i, 0))
    full = lambda a: pl.BlockSpec(a.shape, lambda i: (0,) * a.ndim)
    tri = jnp.asarray(np.triu(np.ones((MERGE_TM, MERGE_TM), np.float32), k=1), BF16)
    return pl.pallas_call(
        _merge_kernel,
        out_shape=[
            jax.ShapeDtypeStruct((n_tok, d), F32),
            jax.ShapeDtypeStruct((n_tok, ROW_W), F32),
            jax.ShapeDtypeStruct((ROUTE_ROWS, n_tok), F32),
            jax.ShapeDtypeStruct((ROUTE_ROWS, LANES), F32),
        ],
        grid=(n_tok // MERGE_TM,),
        in_specs=[
            tok(A_WIDTH), tok(B_WIDTH), tok(2 * d), tok(d),
            pl.BlockSpec((1, 6, d), lambda i: (i // tiles_per_batch, 0, 0)),
            full(wa), full(wb), full(wo), full(ln_g), full(ln_b), full(w_router_t), full(b_router), full(tri),
        ],
        out_specs=[tok(d), tok(ROW_W), pl.BlockSpec((ROUTE_ROWS, MERGE_TM), lambda i: (0, i)),
                   pl.BlockSpec((ROUTE_ROWS, LANES), lambda i: (0, 0))],
        scratch_shapes=[pltpu.VMEM((ROUTE_ROWS, LANES), F32)],
        compiler_params=_params(("arbitrary",)),
        name="merge_router",
    )(ya, yb, gates, x2, mod3, wa, wb, wo, ln_g, ln_b, w_router_t, b_router, tri)


def _start_row_gather(src_hbm, idx_ref, buf, slot, sem, n_rows):
    def issue(r, carry):
        pltpu.make_async_copy(src_hbm.at[pl.ds(idx_ref[0, 0, r], 1)], buf.at[slot, pl.ds(r, 1)], sem.at[slot]).start()
        return carry

    lax.fori_loop(0, n_rows, issue, 0, unroll=8)


def _wait_row_gather(buf, slot, sem):
    pltpu.make_async_copy(buf.at[slot], buf.at[slot], sem.at[slot]).wait()


MOE_TM = 512


def _dispatch_kernel(ends_ref, idx_ref, rows_ref, sorted_hbm, zero_buf, stage, sem, zsem):
    i = pl.program_id(0)
    slot = i % 2

    @pl.when(i == 0)
    def _():
        zero_buf[...] = jnp.zeros_like(zero_buf)
        for g in range(N_GROUPS):
            start = pl.multiple_of(ends_ref[g] - MOE_TM, MOE_TM)
            nonempty = ends_ref[g] > (ends_ref[g - 1] if g else 0)

            @pl.when(nonempty)
            def _():
                cp = pltpu.make_async_copy(zero_buf, sorted_hbm.at[pl.ds(start, MOE_TM)], zsem)
                cp.start()
                cp.wait()

        for k in range(N_GROUPS):
            spare = sorted_hbm.shape[0] - (k + 1) * MOE_TM

            @pl.when(spare >= ends_ref[N_GROUPS - 1])
            def _():
                cp = pltpu.make_async_copy(zero_buf, sorted_hbm.at[pl.ds(spare, MOE_TM)], zsem)
                cp.start()
                cp.wait()

    stage[slot] = rows_ref[...]

    def issue(r, carry):
        pltpu.make_async_copy(stage.at[slot, pl.ds(r, 1)], sorted_hbm.at[pl.ds(idx_ref[0, 0, r], 1)], sem.at[slot]).start()
        return carry

    lax.fori_loop(0, MOE_TM, issue, 0, unroll=8)

    def wait_slot(s):
        pltpu.make_async_copy(stage.at[s], sorted_hbm.at[pl.ds(0, MOE_TM)], sem.at[s]).wait()

    @pl.when(i > 0)
    def _():
        wait_slot(1 - slot)

    @pl.when(i == pl.num_programs(0) - 1)
    def _():
        wait_slot(slot)


def _dispatch(ends, dest3, rows, n_slots):
    n_steps = dest3.shape[0]
    w = rows.shape[1]
    return pl.pallas_call(
        _dispatch_kernel,
        out_shape=jax.ShapeDtypeStruct((n_slots, w), rows.dtype),
        grid_spec=pltpu.PrefetchScalarGridSpec(
            num_scalar_prefetch=1,
            grid=(n_steps,),
            in_specs=[
                pl.BlockSpec((1, 1, MOE_TM), lambda i, ends: (i, 0, 0), memory_space=pltpu.SMEM),
                pl.BlockSpec((MOE_TM, w), lambda i, ends: (i, 0)),
            ],
            out_specs=pl.BlockSpec(memory_space=pl.ANY),
            scratch_shapes=[pltpu.VMEM((MOE_TM, w), rows.dtype), pltpu.VMEM((2, MOE_TM, w), rows.dtype),
                            pltpu.SemaphoreType.DMA((2,)), pltpu.SemaphoreType.DMA(())],
        ),
        compiler_params=_params(("arbitrary",)),
        name="dispatch",
    )(ends, dest3, rows)


def _moe_kernel(tg_ref, nu_ref, x_ref, wg_ref, wu_ref, wd_ref, y_ref):
    j = pl.program_id(0)
    n_used = nu_ref[0]

    @pl.when(j < n_used)
    def _():
        u = x_ref[:, 0:D_MODEL].astype(BF16)
        comb = x_ref[:, D_MODEL + 1:D_MODEL + 1 + EXPERTS_PER_GROUP]
        y = None
        for jx in range(EXPERTS_PER_GROUP):
            h = jax.nn.silu(_dot(u, wg_ref[jx])) * _dot(u, wu_ref[jx])
            yj = _dot((h * comb[:, jx:jx + 1]).astype(BF16), wd_ref[jx])
            y = yj if y is None else y + yj
        y_ref[...] = y

    @pl.when(j >= n_used)
    def _():
        y_ref[...] = jnp.zeros_like(y_ref)


def _moe(tile_group, n_used, rows_sorted, wg, wu, wd):
    n_tiles = rows_sorted.shape[0] // MOE_TM
    group_of = lambda w: pl.BlockSpec((EXPERTS_PER_GROUP,) + w.shape[1:], lambda j, tg, nu: (tg[j], 0, 0))
    return pl.pallas_call(
        _moe_kernel,
        out_shape=jax.ShapeDtypeStruct((n_tiles * MOE_TM, D_MODEL), F32),
        grid_spec=pltpu.PrefetchScalarGridSpec(
            num_scalar_prefetch=2,
            grid=(n_tiles,),
            in_specs=[
                pl.BlockSpec((MOE_TM, ROW_W), lambda j, tg, nu: (jnp.minimum(j, jnp.maximum(nu[0] - 1, 0)), 0)),
                group_of(wg), group_of(wu), group_of(wd),
            ],
            out_specs=pl.BlockSpec((MOE_TM, D_MODEL), lambda j, tg, nu: (j, 0)),
        ),
        compiler_params=_params(("arbitrary",)),
        name="moe",
    )(tile_group, n_used, rows_sorted, wg, wu, wd)


FINAL_TM = 512


def _final_kernel(idx_ref, idx_next_ref, y_hbm, x1_ref, mod_ref, g2_ref, b2_ref, o_ref, ybuf, sem):
    i = pl.program_id(0)
    slot = i % 2

    @pl.when(i == 0)
    def _():
        _start_row_gather(y_hbm, idx_ref, ybuf, 0, sem, FINAL_TM)

    @pl.when(i + 1 < pl.num_programs(0))
    def _():
        _start_row_gather(y_hbm, idx_next_ref, ybuf, 1 - slot, sem, FINAL_TM)

    _wait_row_gather(ybuf, slot, sem)
    m = mod_ref[0]
    o_ref[...] = _layer_norm(DEEPNORM_ALPHA * x1_ref[...] + m[5:6] * ybuf[slot], g2_ref[...], b2_ref[...])


def _final(dest3, y_sorted, x1, mod3, ln_g, ln_b):
    n_tok, d = x1.shape
    n_tiles = n_tok // FINAL_TM
    tiles_per_batch = SEQ // FINAL_TM
    idx_spec = lambda f: pl.BlockSpec((1, 1, FINAL_TM), f, memory_space=pltpu.SMEM)
    vec = pl.BlockSpec((1, d), lambda i: (0, 0))
    return pl.pallas_call(
        _final_kernel,
        out_shape=jax.ShapeDtypeStruct((n_tok, d), F32),
        grid=(n_tiles,),
        in_specs=[
            idx_spec(lambda i: (i, 0, 0)),
            idx_spec(lambda i: (jnp.minimum(i + 1, n_tiles - 1), 0, 0)),
            pl.BlockSpec(memory_space=pl.ANY),
            pl.BlockSpec((FINAL_TM, d), lambda i: (i, 0)),
            pl.BlockSpec((1, 6, d), lambda i: (i // tiles_per_batch, 0, 0)),
            vec, vec,
        ],
        out_specs=pl.BlockSpec((FINAL_TM, d), lambda i: (i, 0)),
        scratch_shapes=[pltpu.VMEM((2, FINAL_TM, d), F32), pltpu.SemaphoreType.DMA((2,))],
        compiler_params=_params(("arbitrary",)),
        name="combine_norm",
    )(dest3, dest3, y_sorted, x1, mod3, ln_g, ln_b)


def _routing_tables(route_t, cnt):
    n_tok = route_t.shape[1]
    n_tiles = n_tok // MOE_TM + N_GROUPS
    counts = cnt[:N_GROUPS, 0].astype(jnp.int32)
    padded = (counts + MOE_TM - 1) // MOE_TM * MOE_TM
    ends = jnp.cumsum(padded)
    gid = route_t[0].astype(jnp.int32)
    dest = (ends - padded)[gid] + route_t[5].astype(jnp.int32)
    tile_start = jnp.arange(n_tiles, dtype=jnp.int32) * MOE_TM
    tile_group = jnp.minimum(jnp.sum((tile_start[:, None] >= ends[None, :]).astype(jnp.int32), axis=1), N_GROUPS - 1)
    n_used = (ends[-1:] // MOE_TM).astype(jnp.int32)
    return tile_group, n_used, ends.astype(jnp.int32), dest.reshape(-1, 1, MOE_TM), n_tiles * MOE_TM


def _regroup_w_in(w):
    o = np.cumsum((512, 512, 512, 512, 128, 512, 64, 8, 2048))
    a_q, a_kv, b_q, b_kv = w[:, :o[0]], w[:, o[0]:o[2]], w[:, o[2]:o[3]], w[:, o[3]:o[4]]
    i_q, i_k, i_w, gates = w[:, o[4]:o[5]], w[:, o[5]:o[6]], w[:, o[6]:o[7]], w[:, o[7]:o[8]]
    pad = jnp.zeros((w.shape[0], LANES - IDX_HEADS), w.dtype)
    a_q = a_q * (LOG2E * A_QK_DIM ** -0.5)
    return jnp.concatenate([a_q, a_kv, b_q, i_q, gates, b_kv, i_k, i_k, i_w, pad], axis=1).astype(BF16)


def _pad_w_uk(w_uk):
    wt = jnp.transpose(w_uk, (1, 2, 0)) * (LOG2E * B_HEAD_DIM ** -0.5)
    z = jnp.zeros_like(wt)
    even = jnp.concatenate([wt, z], axis=1)
    odd = jnp.concatenate([z, wt], axis=1)
    sel = (jnp.arange(B_HEADS) % 2 == 0)[:, None, None]
    return jnp.where(sel, even, odd).astype(BF16)


def _pair_w_uv(w_uv):
    wv = jnp.transpose(w_uv, (1, 0, 2))
    z = jnp.zeros_like(wv[0::2])
    top = jnp.concatenate([wv[0::2], z], axis=2)
    bot = jnp.concatenate([z, wv[1::2]], axis=2)
    return jnp.transpose(jnp.concatenate([top, bot], axis=1), (0, 2, 1)).astype(BF16)


def kernel(x, c, ada_w, ada_b, w_in, lambda_q1, lambda_k1, lambda_q2, lambda_k2, a_subln_g, kv_norm_g, w_uk, w_uv,
           w_a_proj, w_b_proj, w_o, ln1_g, ln1_b, w_group, b_group, w_expert_router, b_expert_router,
           w_exp_gate, w_exp_up, w_exp_down, ln2_g, ln2_b):
    b, s, d = x.shape
    assert (s, d) == (SEQ, D_MODEL) and ada_w.shape[0] == DEPTH
    slopes_a, slopes_b = _alibi_slopes()
    lane_rep = lambda v: jnp.asarray(np.repeat(v[:, None, None], LANES, axis=2))
    x2 = x.reshape(b * s, d)
    l = 0
    mod3 = _modulation(c, ada_w[l], ada_b[l]).reshape(b, 6, d)

    qkv, bq, iq, gates, small = _projection(x2, mod3, _regroup_w_in(w_in[l]))
    lam_vecs = jnp.stack([lambda_q1[l], lambda_k1[l], lambda_q2[l], lambda_k2[l]]).astype(F32)
    y_a = _diff_attention(qkv.reshape(b, s, W_QKV), lam_vecs, a_subln_g[l].reshape(1, A_V_DIM), lane_rep(slopes_a))
    y_b = _sparse_attention(bq.reshape(b, s, -1), iq.reshape(b, s, -1), small.reshape(b, s, -1),
                            kv_norm_g[l].reshape(1, B_KV_RANK), _pad_w_uk(w_uk[l]), _pair_w_uv(w_uv[l]),
                            lane_rep(slopes_b))

    w_router_t = jnp.zeros((ROUTER_ROWS, d), F32).at[:N_GROUPS].set(w_group[l].T).at[
        N_GROUPS:N_GROUPS + N_EXPERTS].set(w_expert_router[l].T)
    b_router = jnp.zeros((ROUTER_ROWS, 1), F32).at[:N_GROUPS, 0].set(b_group[l]).at[
        N_GROUPS:N_GROUPS + N_EXPERTS, 0].set(b_expert_router[l])
    x1, u2, route_t, cnt = _merge(y_a.reshape(b * s, A_WIDTH), y_b.reshape(b * s, B_WIDTH), gates, x2, mod3,
                                  w_a_proj[l].astype(BF16), w_b_proj[l].astype(BF16), w_o[l].astype(BF16),
                                  ln1_g[l].reshape(1, d), ln1_b[l].reshape(1, d), w_router_t, b_router)

    tile_group, n_used, ends, dest3, n_slots = _routing_tables(route_t, cnt)
    wg, wu, wd = w_exp_gate[l].astype(BF16), w_exp_up[l].astype(BF16), w_exp_down[l].astype(BF16)
    y_sorted = _moe(tile_group, n_used, _dispatch(ends, dest3, u2, n_slots), wg, wu, wd)
    out = _final(dest3, y_sorted, x1, mod3, ln2_g[l].reshape(1, d), ln2_b[l].reshape(1, d))
    return out.reshape(b, s, d)
```

```python
import functools
import math

import numpy as np
import jax
import jax.numpy as jnp
from jax import lax
from jax.experimental import pallas as pl
from jax.experimental.pallas import tpu as pltpu

D_MODEL = 1024
SEQ = 2048
CHUNK = 64
A_QK_DIM = 64
A_V_DIM = 128
A_HEADS = 4
A_WIDTH = A_HEADS * A_V_DIM
B_HEAD_DIM = 64
B_HEADS = 8
B_WIDTH = B_HEADS * B_HEAD_DIM
B_KV_RANK = 128
IDX_HEADS = 8
IDX_DIM = 64
TOPK = 256
N_ALIBI_HEADS = A_HEADS + B_HEADS
N_GROUPS = 4
EXPERTS_PER_GROUP = 4
N_EXPERTS = 16
D_FF_EXPERT = 256
LN_EPS = 1e-5
RMS_EPS = 1e-5
DEPTH = 1
DEEPNORM_ALPHA = (2.0 * DEPTH) ** 0.25
LAM_INIT = 0.8 - 0.6 * math.exp(-0.3 * 0)
LOG2E = math.log2(math.e)

LANES = 128
VMEM_LIMIT_BYTES = 56 * 1024 * 1024

F32 = jnp.float32
BF16 = jnp.bfloat16
NEG_INF = float("-inf")

_NT = (((1,), (1,)), ((), ()))


def _dot(a, b, **kw):
    return jnp.dot(a, b, preferred_element_type=F32, **kw)


def _dot_nt(a, b, **kw):
    return lax.dot_general(a, b, _NT, preferred_element_type=F32, **kw)


def _alibi_slopes():
    n = N_ALIBI_HEADS
    slopes = (2.0 ** (-8.0 * np.arange(1, n + 1) / n)).astype(np.float32)
    a_idx = np.arange(A_HEADS) * (n // A_HEADS)
    b_idx = np.setdiff1d(np.arange(n), a_idx)
    return slopes[a_idx], slopes[b_idx]


def _layer_norm(h, g, b):
    mu = jnp.mean(h, axis=-1, keepdims=True)
    d = h - mu
    var = jnp.mean(d * d, axis=-1, keepdims=True)
    return d * lax.rsqrt(var + LN_EPS) * g + b


KEY_SLAB = 64


def _over_keys(op, x):
    part = op(x.reshape(x.shape[0] // KEY_SLAB, KEY_SLAB, x.shape[1]), axis=0)
    return op(part, axis=0, keepdims=True)


def _params(sem):
    return pltpu.CompilerParams(dimension_semantics=sem, vmem_limit_bytes=VMEM_LIMIT_BYTES)


MOD_TN = 1536


def _mod_kernel(c_ref, w_ref, b_ref, o_ref):
    c = c_ref[...]
    cond = c * jax.nn.sigmoid(c)
    o_ref[...] = _dot(cond, w_ref[...], precision=lax.Precision.HIGHEST) + b_ref[...]


def _modulation(c, ada_w, ada_b):
    b, d = c.shape
    n = ada_w.shape[1]
    return pl.pallas_call(
        _mod_kernel,
        out_shape=jax.ShapeDtypeStruct((b, n), F32),
        grid=(n // MOD_TN,),
        in_specs=[
            pl.BlockSpec((b, d), lambda j: (0, 0)),
            pl.BlockSpec((d, MOD_TN), lambda j: (0, j)),
            pl.BlockSpec((1, MOD_TN), lambda j: (0, j)),
        ],
        out_specs=pl.BlockSpec((b, MOD_TN), lambda j: (0, j)),
        compiler_params=_params(("arbitrary",)),
        name="modulation",
    )(c, ada_w, ada_b.reshape(1, n))


PROJ_TM = 512
W_QKV = 3 * A_WIDTH
W_SMALL = 3 * LANES
PROJ_COLS = (W_QKV, B_WIDTH, IDX_HEADS * IDX_DIM, 2 * D_MODEL, W_SMALL)


def _proj_kernel(x_ref, mod_ref, w_ref, qkv_ref, bq_ref, iq_ref, gates_ref, small_ref):
    m = mod_ref[0]
    u = (x_ref[...] * (1.0 + m[1:2]) + m[0:1]).astype(BF16)
    off = 0
    for o_ref, n in zip((qkv_ref, bq_ref, iq_ref, gates_ref, small_ref), PROJ_COLS):
        o_ref[...] = _dot(u, w_ref[:, off:off + n]).astype(o_ref.dtype)
        off += n


def _projection(x2, mod3, w_cat):
    n_tok, d = x2.shape
    tiles_per_batch = SEQ // PROJ_TM
    out_dtypes = (BF16, BF16, BF16, BF16, F32)
    return pl.pallas_call(
        _proj_kernel,
        out_shape=[jax.ShapeDtypeStruct((n_tok, n), dt) for n, dt in zip(PROJ_COLS, out_dtypes)],
        grid=(n_tok // PROJ_TM,),
        in_specs=[
            pl.BlockSpec((PROJ_TM, d), lambda i: (i, 0)),
            pl.BlockSpec((1, 6, d), lambda i: (i // tiles_per_batch, 0, 0)),
            pl.BlockSpec(w_cat.shape, lambda i: (0, 0)),
        ],
        out_specs=[pl.BlockSpec((PROJ_TM, n), lambda i: (i, 0)) for n in PROJ_COLS],
        compiler_params=_params(("arbitrary",)),
        name="projection",
    )(x2, mod3, w_cat)


A_TQ = 256


def _diff_attn_kernel(q_ref, k_ref, v_ref, lam_ref, g_ref, slope_ref, o_ref, bias_ref):
    slope = slope_ref[0][:, 0:1]
    r = lax.broadcasted_iota(jnp.int32, (A_TQ, SEQ), 0)
    j = lax.broadcasted_iota(jnp.int32, (A_TQ, SEQ), 1)
    dist = jnp.abs(r + (SEQ - A_TQ) - j).astype(F32)
    visible = (j - (SEQ - A_TQ)) // CHUNK <= r // CHUNK
    bias_ref[...] = jnp.where(visible, (-LOG2E) * slope * dist, NEG_INF)

    lv = lam_ref[...]
    lam = (jnp.exp(jnp.sum(lv[0:1] * lv[1:2], axis=1, keepdims=True))
           - jnp.exp(jnp.sum(lv[2:3] * lv[3:4], axis=1, keepdims=True)) + LAM_INIT)
    lane = lax.broadcasted_iota(jnp.int32, (A_TQ, 2 * A_QK_DIM), 1)
    for i in range(SEQ // A_TQ):
        q0 = i * A_TQ
        kv = q0 + A_TQ
        q = q_ref[0, q0:q0 + A_TQ, :]
        k = k_ref[0, 0:kv, :]
        bias = bias_ref[:, SEQ - kv:SEQ]
        ps, inv_ls = [], []
        for mth in range(2):
            qm = jnp.where((lane // A_QK_DIM) == mth, q, jnp.zeros_like(q))
            s = _dot_nt(qm, k) + bias
            mx = jnp.max(s, axis=-1, keepdims=True)
            p = jnp.exp2(s - mx)
            ps.append(p)
            inv_ls.append(1.0 / jnp.sum(p, axis=-1, keepdims=True))
        attn = ps[0] * inv_ls[0] - ps[1] * (lam * inv_ls[1])
        o = _dot(attn.astype(BF16), v_ref[0, 0:kv, :])
        y = o * lax.rsqrt(jnp.mean(o * o, axis=-1, keepdims=True) + RMS_EPS) * g_ref[...]
        o_ref[0, q0:q0 + A_TQ, :] = (y * (1.0 - LAM_INIT)).astype(o_ref.dtype)


def _diff_attention(qkv3, lam_vecs, subln_g, slopes):
    b, s, _ = qkv3.shape
    blk = (1, s, A_V_DIM)
    return pl.pallas_call(
        _diff_attn_kernel,
        out_shape=jax.ShapeDtypeStruct((b, s, A_WIDTH), BF16),
        grid=(b, A_HEADS),
        in_specs=[
            pl.BlockSpec(blk, lambda bi, h: (bi, 0, h)),
            pl.BlockSpec(blk, lambda bi, h: (bi, 0, A_HEADS + h)),
            pl.BlockSpec(blk, lambda bi, h: (bi, 0, 2 * A_HEADS + h)),
            pl.BlockSpec(lam_vecs.shape, lambda bi, h: (0, 0)),
            pl.BlockSpec((1, A_V_DIM), lambda bi, h: (0, 0)),
            pl.BlockSpec((1, 1, LANES), lambda bi, h: (h, 0, 0)),
        ],
        out_specs=pl.BlockSpec(blk, lambda bi, h: (bi, 0, h)),
        scratch_shapes=[pltpu.VMEM((A_TQ, SEQ), F32)],
        compiler_params=_params(("arbitrary", "arbitrary")),
        name="diff_attention",
    )(qkv3, qkv3, qkv3, lam_vecs, subln_g, slopes)


B_TQ = 128
SMALL_KV = slice(0, 128)
SMALL_IK = slice(128, 256)
SMALL_IW = slice(256, 384)
ONES_ROWS = 16
BISECT_FIRST = 18
BISECT_MORE = 3
BISECT_MAX_ROUNDS = 80


def _key_count(mask_f32):
    return _over_keys(jnp.sum, mask_f32)


def _topk_mask(score_ref, selb_ref, kv):
    s = score_ref[0:kv, :]

    def bisect(_, st):
        lo, hi = st
        probe = 0.5 * lo + 0.5 * hi
        ge = _key_count(jnp.where(score_ref[0:kv, :] >= probe, 1.0, 0.0)) >= TOPK
        return jnp.where(ge, probe, lo), jnp.where(ge, hi, probe)

    def candidate(lo):
        sc = score_ref[0:kv, :]
        t_val = _over_keys(jnp.min, jnp.where(sc >= lo, sc, jnp.inf))
        n_gt = _key_count(jnp.where(sc > t_val, 1.0, 0.0))
        return t_val, n_gt, jnp.sum(jnp.where(n_gt >= TOPK, 1.0, 0.0))

    lo = _over_keys(jnp.min, jnp.where(s == NEG_INF, jnp.inf, s))
    hi = _over_keys(jnp.max, s)
    lo, hi = lax.fori_loop(0, BISECT_FIRST, bisect, (lo, hi))

    def unresolved(st):
        return (st[5] > 0.0) & (st[0] < BISECT_MAX_ROUNDS)

    def refine(st):
        lo, hi = lax.fori_loop(0, BISECT_MORE, bisect, (st[1], st[2]))
        return (st[0] + 1, lo, hi) + candidate(lo)

    _, _, _, t_val, n_gt, _ = lax.while_loop(unresolved, refine, (jnp.int32(0), lo, hi) + candidate(lo))
    gt = s > t_val
    eqf = jnp.where(s == t_val, 1.0, 0.0)
    need = TOPK - n_gt
    selb_ref[0:kv, :] = jnp.where(s >= t_val, 0.0, NEG_INF)
    tie_overflow = jnp.max(jnp.abs(_key_count(eqf) - need))

    @pl.when(tie_overflow > 0.0)
    def _():
        idx = lax.broadcasted_iota(jnp.int32, (kv, B_TQ), 0)

        def idx_step(b, j):
            c = j | lax.shift_left(jnp.int32(1), 10 - b)
            before = _key_count(jnp.where(idx < c, eqf, 0.0))
            return jnp.where(before < need, c, j)

        j_max = lax.fori_loop(0, 11, idx_step, jnp.zeros((1, B_TQ), jnp.int32))
        keep = gt | ((eqf > 0.0) & (idx <= j_max))
        selb_ref[0:kv, :] = jnp.where(keep, 0.0, NEG_INF)


def _sparse_attn_kernel(bq_ref, iq_ref, small_ref, kvg_ref, wuk_ref, wuvt_ref, slope_ref, cparts_ref, o_ref,
                        corr_ref, ckv_ref, ckvt_ref, ik_ref, score_ref, selb_ref):
    @pl.when(pl.program_id(0) == 0)
    def _():
        k_loc = lax.broadcasted_iota(jnp.int32, (B_TQ, B_TQ), 0)
        q_loc = lax.broadcasted_iota(jnp.int32, (B_TQ, B_TQ), 1)
        later = jnp.maximum(k_loc - q_loc, 0).astype(F32)
        for h in range(B_HEADS):
            corr_ref[h] = (-2.0 * LOG2E) * slope_ref[h][:, 0:1] * later

    kv_lat = small_ref[0, :, SMALL_KV]
    ckv = kv_lat * lax.rsqrt(jnp.mean(kv_lat * kv_lat, axis=-1, keepdims=True) + RMS_EPS) * kvg_ref[...]
    pos = lax.broadcasted_iota(jnp.int32, (SEQ, LANES), 0)
    col = lax.broadcasted_iota(jnp.int32, (SEQ, LANES), 1)
    pos_cols = jnp.where(col < 3, pos // CHUNK * CHUNK, jnp.where(col < 6, pos % CHUNK, 0)).astype(F32)
    ckv_ref[:, 0:B_KV_RANK] = ckv.astype(BF16)
    ckv_ref[:, B_KV_RANK:] = pos_cols.astype(BF16)
    ckvt_ref[0:B_KV_RANK, :] = ckv.T.astype(BF16)
    ckvt_ref[B_KV_RANK:, :] = jnp.where(lax.broadcasted_iota(jnp.int32, (ONES_ROWS, SEQ), 0) == 0, 1.0, 0.0).astype(BF16)
    ik_ref[...] = small_ref[0, :, SMALL_IK].astype(BF16)

    lane = lax.broadcasted_iota(jnp.int32, (B_TQ, LANES), 1)
    kk = lax.broadcasted_iota(jnp.int32, (B_TQ, B_TQ), 0)
    qq = lax.broadcasted_iota(jnp.int32, (B_TQ, B_TQ), 1)
    diag_visible = kk // CHUNK <= qq // CHUNK
    n_pairs = B_HEADS // 2

    def half_masked(pair):
        z = jnp.zeros_like(pair)
        return jnp.concatenate([jnp.where(lane < IDX_DIM, pair, z), jnp.where(lane >= IDX_DIM, pair, z)], axis=0)

    for i in range(SEQ // B_TQ):
        q0 = i * B_TQ
        kv = q0 + B_TQ
        rows = slice(q0, q0 + B_TQ)
        if kv <= TOPK:
            if q0:
                selb_ref[0:q0, :] = jnp.zeros((q0, B_TQ), F32)
            selb_ref[q0:kv, :] = jnp.where(diag_visible, 0.0, NEG_INF)
        else:
            iq = iq_ref[0, rows, :]
            iw_t = small_ref[0, rows, SMALL_IW].T
            ik = ik_ref[0:kv, :]
            heads = jnp.concatenate([half_masked(iq[:, jp * LANES:(jp + 1) * LANES]) for jp in range(n_pairs)], axis=0)
            x = _dot_nt(ik, heads)
            score = iw_t[0:1] * jnp.maximum(x[:, 0:B_TQ], 0.0)
            for h in range(1, IDX_HEADS):
                score = score + iw_t[h:h + 1] * jnp.maximum(x[:, h * B_TQ:(h + 1) * B_TQ], 0.0)
            score_ref[0:q0, :] = score[0:q0]
            score_ref[q0:kv, :] = jnp.where(diag_visible, score[q0:kv], NEG_INF)
            _topk_mask(score_ref, selb_ref, kv)

        bq = bq_ref[0, rows, :]
        q_aug = [jnp.concatenate([_dot(bq[:, (h // 2) * LANES:(h // 2 + 1) * LANES], wuk_ref[h]).astype(BF16),
                                  jnp.broadcast_to(cparts_ref[h:h + 1, :], (B_TQ, LANES))], axis=1)
                 for h in range(B_HEADS)]
        s_all = _dot_nt(ckv_ref[0:kv, :], jnp.concatenate(q_aug, axis=0))
        selb = selb_ref[0:kv, :]
        probs = []
        for h in range(B_HEADS):
            sh = s_all[:, h * B_TQ:(h + 1) * B_TQ]
            diag = sh[q0:kv] + corr_ref[h]
            sh = (jnp.concatenate([sh[0:q0], diag], axis=0) if q0 else diag) + selb
            probs.append(jnp.exp2(sh - _over_keys(jnp.max, sh)).astype(BF16))
        lat_t = _dot(ckvt_ref[:, 0:kv], jnp.concatenate(probs, axis=1))
        inv_l = 1.0 / lat_t[B_KV_RANK:B_KV_RANK + 1, :]
        outs = []
        for jp in range(n_pairs):
            pair = jnp.concatenate([lat_t[0:B_KV_RANK, h * B_TQ:(h + 1) * B_TQ] * inv_l[:, h * B_TQ:(h + 1) * B_TQ]
                                    for h in (2 * jp, 2 * jp + 1)], axis=0).astype(BF16)
            outs.append(_dot(wuvt_ref[jp], pair))
        o_ref[0, rows, :] = jnp.concatenate(outs, axis=0).T.astype(o_ref.dtype)


def _slope_pieces(slopes):
    c = jnp.asarray(slopes, F32) * LOG2E
    c1 = c.astype(BF16).astype(F32)
    c2 = (c - c1).astype(BF16).astype(F32)
    c3 = (c - c1 - c2).astype(BF16).astype(F32)
    pieces = jnp.stack([c1, c2, c3, c1, c2, c3], axis=1)
    return jnp.pad(pieces, ((0, 0), (0, LANES - pieces.shape[1]))).astype(BF16)


def _sparse_attention(bq3, iq3, small3, kv_norm_g, wuk_pad, wuvt_pair, slopes, slope_pieces):
    b, s, _ = bq3.shape
    return pl.pallas_call(
        _sparse_attn_kernel,
        out_shape=jax.ShapeDtypeStruct((b, s, B_WIDTH), BF16),
        grid=(b,),
        in_specs=[
            pl.BlockSpec((1, s, B_WIDTH), lambda bi: (bi, 0, 0)),
            pl.BlockSpec((1, s, IDX_HEADS * IDX_DIM), lambda bi: (bi, 0, 0)),
            pl.BlockSpec((1, s, W_SMALL), lambda bi: (bi, 0, 0)),
            pl.BlockSpec((1, B_KV_RANK), lambda bi: (0, 0)),
            pl.BlockSpec(wuk_pad.shape, lambda bi: (0, 0, 0)),
            pl.BlockSpec(wuvt_pair.shape, lambda bi: (0, 0, 0)),
            pl.BlockSpec(slopes.shape, lambda bi: (0, 0, 0)),
            pl.BlockSpec(slope_pieces.shape, lambda bi: (0, 0)),
        ],
        out_specs=pl.BlockSpec((1, s, B_WIDTH), lambda bi: (bi, 0, 0)),
        scratch_shapes=[
            pltpu.VMEM((B_HEADS, B_TQ, B_TQ), F32),
            pltpu.VMEM((SEQ, B_KV_RANK + LANES), BF16),
            pltpu.VMEM((B_KV_RANK + ONES_ROWS, SEQ), BF16),
            pltpu.VMEM((SEQ, LANES), BF16),
            pltpu.VMEM((SEQ, B_TQ), F32),
            pltpu.VMEM((SEQ, B_TQ), F32),
        ],
        compiler_params=_params(("arbitrary",)),
        name="sparse_attention",
    )(bq3, iq3, small3, kv_norm_g, wuk_pad, wuvt_pair, slopes, slope_pieces)


MERGE_TM = 512
ROUTER_ROWS = 32
ROUTE_ROWS = 8
ROW_W = D_MODEL + LANES


def _first_max_onehot(rows):
    mx = rows[0]
    for r in rows[1:]:
        mx = jnp.maximum(mx, r)
    taken = jnp.zeros_like(mx)
    hot = []
    for r in rows:
        h = jnp.where((r == mx) & (taken == 0.0), 1.0, 0.0)
        taken = taken + h
        hot.append(h)
    return hot, mx


def _softmax_rows(rows):
    mx = rows[0]
    for r in rows[1:]:
        mx = jnp.maximum(mx, r)
    e = [jnp.exp(r - mx) for r in rows]
    tot = e[0]
    for r in e[1:]:
        tot = tot + r
    return [r / tot for r in e]


def _merge_kernel(ya_ref, yb_ref, gates_ref, x_ref, mod_ref, wa_ref, wb_ref, wo_ref, g1_ref, b1_ref,
                  wr_ref, br_ref, tri_ref, x1_ref, u2_ref, route_ref, cnt_ref, run_ref):
    m = mod_ref[0]
    pa = _dot(ya_ref[...], wa_ref[...])
    pb = _dot(yb_ref[...], wb_ref[...])
    gt = jax.nn.sigmoid(gates_ref[...].astype(F32))
    mixed = gt[:, 0:D_MODEL] * pa + gt[:, D_MODEL:2 * D_MODEL] * pb
    z = _dot(mixed.astype(BF16), wo_ref[...])
    x1 = _layer_norm(DEEPNORM_ALPHA * x_ref[...] + m[2:3] * z, g1_ref[...], b1_ref[...])
    x1_ref[...] = x1
    u2 = x1 * (1.0 + m[4:5]) + m[3:4]
    u2_ref[:, 0:D_MODEL] = u2

    logits = _dot_nt(wr_ref[...], u2, precision=lax.Precision.HIGHEST) + br_ref[...]
    g_prob = _softmax_rows([logits[k:k + 1] for k in range(N_GROUPS)])
    g_hot, g_top = _first_max_onehot(g_prob)
    e_logit = []
    for jx in range(EXPERTS_PER_GROUP):
        acc = jnp.zeros_like(g_top)
        for g in range(N_GROUPS):
            row = N_GROUPS + g * EXPERTS_PER_GROUP + jx
            acc = acc + logits[row:row + 1] * g_hot[g]
        e_logit.append(acc)
    e_prob = _softmax_rows(e_logit)
    hot1, p1 = _first_max_onehot(e_prob)
    rest = [jnp.where(h > 0.0, NEG_INF, p) for h, p in zip(hot1, e_prob)]
    hot2, p2 = _first_max_onehot(rest)
    tot = p1 + p2
    w1 = g_top * (p1 / tot)
    w2 = g_top * (p2 / tot)

    @pl.when(pl.program_id(0) == 0)
    def _():
        run_ref[...] = jnp.zeros_like(run_ref)

    gid = g_hot[1] + 2.0 * g_hot[2] + 3.0 * g_hot[3]
    grp = lax.broadcasted_iota(jnp.int32, (ROUTE_ROWS, MERGE_TM), 0).astype(F32)
    hot8 = jnp.where(grp == gid, 1.0, 0.0)
    before = _dot(hot8.astype(BF16), tri_ref[...])
    rank = jnp.sum(hot8 * (run_ref[:, 0:1] + before), axis=0, keepdims=True)
    record = jnp.concatenate([gid] + [w1 * hot1[jx] + w2 * hot2[jx] for jx in range(EXPERTS_PER_GROUP)]
                             + [rank, jnp.zeros((LANES - 2 - EXPERTS_PER_GROUP, MERGE_TM), F32)], axis=0)
    route_ref[...] = record[0:ROUTE_ROWS]
    u2_ref[:, D_MODEL:D_MODEL + LANES] = record.T
    run_ref[...] = run_ref[...] + jnp.sum(hot8, axis=1, keepdims=True)
    cnt_ref[...] = run_ref[...]


def _merge(ya, yb, gates, x2, mod3, wa, wb, wo, ln_g, ln_b, w_router_t, b_router):
    n_tok, d = x2.shape
    tiles_per_batch = SEQ // MERGE_TM
    tok = lambda n: pl.BlockSpec((MERGE_TM, n), lambda i: (i, 0))
    full = lambda a: pl.BlockSpec(a.shape, lambda i: (0,) * a.ndim)
    tri = jnp.asarray(np.triu(np.ones((MERGE_TM, MERGE_TM), np.float32), k=1), BF16)
    return pl.pallas_call(
        _merge_kernel,
        out_shape=[
            jax.ShapeDtypeStruct((n_tok, d), F32),
            jax.ShapeDtypeStruct((n_tok, ROW_W), F32),
            jax.ShapeDtypeStruct((ROUTE_ROWS, n_tok), F32),
            jax.ShapeDtypeStruct((ROUTE_ROWS, LANES), F32),
        ],
        grid=(n_tok // MERGE_TM,),
        in_specs=[
            tok(A_WIDTH), tok(B_WIDTH), tok(2 * d), tok(d),
            pl.BlockSpec((1, 6, d), lambda i: (i // tiles_per_batch, 0, 0)),
            full(wa), full(wb), full(wo), full(ln_g), full(ln_b), full(w_router_t), full(b_router), full(tri),
        ],
        out_specs=[tok(d), tok(ROW_W), pl.BlockSpec((ROUTE_ROWS, MERGE_TM), lambda i: (0, i)),
                   pl.BlockSpec((ROUTE_ROWS, LANES), lambda i: (0, 0))],
        scratch_shapes=[pltpu.VMEM((ROUTE_ROWS, LANES), F32)],
        compiler_params=_params(("arbitrary",)),
        name="merge_router",
    )(ya, yb, gates, x2, mod3, wa, wb, wo, ln_g, ln_b, w_router_t, b_router, tri)


def _start_row_gather(src_hbm, idx_ref, buf, slot, sem, n_rows):
    def issue(r, carry):
        pltpu.make_async_copy(src_hbm.at[pl.ds(idx_ref[0, 0, r], 1)], buf.at[slot, pl.ds(r, 1)], sem.at[slot]).start()
        return carry

    lax.fori_loop(0, n_rows, issue, 0, unroll=8)


def _wait_row_gather(buf, slot, sem):
    pltpu.make_async_copy(buf.at[slot], buf.at[slot], sem.at[slot]).wait()


MOE_TM = 512


def _dispatch_kernel(ends_ref, idx_ref, rows_ref, sorted_hbm, zero_buf, stage, sem, zsem):
    i = pl.program_id(0)
    slot = i % 2

    @pl.when(i == 0)
    def _():
        zero_buf[...] = jnp.zeros_like(zero_buf)
        for g in range(N_GROUPS):
            start = pl.multiple_of(ends_ref[g] - MOE_TM, MOE_TM)
            nonempty = ends_ref[g] > (ends_ref[g - 1] if g else 0)

            @pl.when(nonempty)
            def _():
                cp = pltpu.make_async_copy(zero_buf, sorted_hbm.at[pl.ds(start, MOE_TM)], zsem)
                cp.start()
                cp.wait()

        for k in range(N_GROUPS):
            spare = sorted_hbm.shape[0] - (k + 1) * MOE_TM

            @pl.when(spare >= ends_ref[N_GROUPS - 1])
            def _():
                cp = pltpu.make_async_copy(zero_buf, sorted_hbm.at[pl.ds(spare, MOE_TM)], zsem)
                cp.start()
                cp.wait()

    stage[slot] = rows_ref[...]

    def issue(r, carry):
        pltpu.make_async_copy(stage.at[slot, pl.ds(r, 1)], sorted_hbm.at[pl.ds(idx_ref[0, 0, r], 1)], sem.at[slot]).start()
        return carry

    lax.fori_loop(0, MOE_TM, issue, 0, unroll=8)

    def wait_slot(s):
        pltpu.make_async_copy(stage.at[s], sorted_hbm.at[pl.ds(0, MOE_TM)], sem.at[s]).wait()

    @pl.when(i > 0)
    def _():
        wait_slot(1 - slot)

    @pl.when(i == pl.num_programs(0) - 1)
    def _():
        wait_slot(slot)


def _dispatch(ends, dest3, rows, n_slots):
    n_steps = dest3.shape[0]
    w = rows.shape[1]
    return pl.pallas_call(
        _dispatch_kernel,
        out_shape=jax.ShapeDtypeStruct((n_slots, w), rows.dtype),
        grid_spec=pltpu.PrefetchScalarGridSpec(
            num_scalar_prefetch=1,
            grid=(n_steps,),
            in_specs=[
                pl.BlockSpec((1, 1, MOE_TM), lambda i, ends: (i, 0, 0), memory_space=pltpu.SMEM),
                pl.BlockSpec((MOE_TM, w), lambda i, ends: (i, 0)),
            ],
            out_specs=pl.BlockSpec(memory_space=pl.ANY),
            scratch_shapes=[pltpu.VMEM((MOE_TM, w), rows.dtype), pltpu.VMEM((2, MOE_TM, w), rows.dtype),
                            pltpu.SemaphoreType.DMA((2,)), pltpu.SemaphoreType.DMA(())],
        ),
        compiler_params=_params(("arbitrary",)),
        name="dispatch",
    )(ends, dest3, rows)


def _moe_kernel(tg_ref, nu_ref, x_ref, wg_ref, wu_ref, wd_ref, y_ref):
    j = pl.program_id(0)
    n_used = nu_ref[0]

    @pl.when(j < n_used)
    def _():
        u = x_ref[:, 0:D_MODEL].astype(BF16)
        comb = x_ref[:, D_MODEL + 1:D_MODEL + 1 + EXPERTS_PER_GROUP]
        y = None
        for jx in range(EXPERTS_PER_GROUP):
            h = jax.nn.silu(_dot(u, wg_ref[jx])) * _dot(u, wu_ref[jx])
            yj = _dot((h * comb[:, jx:jx + 1]).astype(BF16), wd_ref[jx])
            y = yj if y is None else y + yj
        y_ref[...] = y

    @pl.when(j >= n_used)
    def _():
        y_ref[...] = jnp.zeros_like(y_ref)


def _moe(tile_group, n_used, rows_sorted, wg, wu, wd):
    n_tiles = rows_sorted.shape[0] // MOE_TM
    group_of = lambda w: pl.BlockSpec((EXPERTS_PER_GROUP,) + w.shape[1:], lambda j, tg, nu: (tg[j], 0, 0))
    return pl.pallas_call(
        _moe_kernel,
        out_shape=jax.ShapeDtypeStruct((n_tiles * MOE_TM, D_MODEL), F32),
        grid_spec=pltpu.PrefetchScalarGridSpec(
            num_scalar_prefetch=2,
            grid=(n_tiles,),
            in_specs=[
                pl.BlockSpec((MOE_TM, ROW_W), lambda j, tg, nu: (jnp.minimum(j, jnp.maximum(nu[0] - 1, 0)), 0)),
                group_of(wg), group_of(wu), group_of(wd),
            ],
            out_specs=pl.BlockSpec((MOE_TM, D_MODEL), lambda j, tg, nu: (j, 0)),
        ),
        compiler_params=_params(("arbitrary",)),
        name="moe",
    )(tile_group, n_used, rows_sorted, wg, wu, wd)


FINAL_TM = 512


def _final_kernel(idx_ref, idx_next_ref, y_hbm, x1_ref, mod_ref, g2_ref, b2_ref, o_ref, ybuf, sem):
    i = pl.program_id(0)
    slot = i % 2

    @pl.when(i == 0)
    def _():
        _start_row_gather(y_hbm, idx_ref, ybuf, 0, sem, FINAL_TM)

    @pl.when(i + 1 < pl.num_programs(0))
    def _():
        _start_row_gather(y_hbm, idx_next_ref, ybuf, 1 - slot, sem, FINAL_TM)

    _wait_row_gather(ybuf, slot, sem)
    m = mod_ref[0]
    o_ref[...] = _layer_norm(DEEPNORM_ALPHA * x1_ref[...] + m[5:6] * ybuf[slot], g2_ref[...], b2_ref[...])


def _final(dest3, y_sorted, x1, mod3, ln_g, ln_b):
    n_tok, d = x1.shape
    n_tiles = n_tok // FINAL_TM
    tiles_per_batch = SEQ // FINAL_TM
    idx_spec = lambda f: pl.BlockSpec((1, 1, FINAL_TM), f, memory_space=pltpu.SMEM)
    vec = pl.BlockSpec((1, d), lambda i: (0, 0))
    return pl.pallas_call(
        _final_kernel,
        out_shape=jax.ShapeDtypeStruct((n_tok, d), F32),
        grid=(n_tiles,),
        in_specs=[
            idx_spec(lambda i: (i, 0, 0)),
            idx_spec(lambda i: (jnp.minimum(i + 1, n_tiles - 1), 0, 0)),
            pl.BlockSpec(memory_space=pl.ANY),
            pl.BlockSpec((FINAL_TM, d), lambda i: (i, 0)),
            pl.BlockSpec((1, 6, d), lambda i: (i // tiles_per_batch, 0, 0)),
            vec, vec,
        ],
        out_specs=pl.BlockSpec((FINAL_TM, d), lambda i: (i, 0)),
        scratch_shapes=[pltpu.VMEM((2, FINAL_TM, d), F32), pltpu.SemaphoreType.DMA((2,))],
        compiler_params=_params(("arbitrary",)),
        name="combine_norm",
    )(dest3, dest3, y_sorted, x1, mod3, ln_g, ln_b)


def _routing_tables(route_t, cnt):
    n_tok = route_t.shape[1]
    n_tiles = n_tok // MOE_TM + N_GROUPS
    counts = cnt[:N_GROUPS, 0].astype(jnp.int32)
    padded = (counts + MOE_TM - 1) // MOE_TM * MOE_TM
    ends = jnp.cumsum(padded)
    gid = route_t[0].astype(jnp.int32)
    dest = (ends - padded)[gid] + route_t[5].astype(jnp.int32)
    tile_start = jnp.arange(n_tiles, dtype=jnp.int32) * MOE_TM
    tile_group = jnp.minimum(jnp.sum((tile_start[:, None] >= ends[None, :]).astype(jnp.int32), axis=1), N_GROUPS - 1)
    n_used = (ends[-1:] // MOE_TM).astype(jnp.int32)
    return tile_group, n_used, ends.astype(jnp.int32), dest.reshape(-1, 1, MOE_TM), n_tiles * MOE_TM


def _regroup_w_in(w):
    o = np.cumsum((512, 512, 512, 512, 128, 512, 64, 8, 2048))
    a_q, a_kv, b_q, b_kv = w[:, :o[0]], w[:, o[0]:o[2]], w[:, o[2]:o[3]], w[:, o[3]:o[4]]
    i_q, i_k, i_w, gates = w[:, o[4]:o[5]], w[:, o[5]:o[6]], w[:, o[6]:o[7]], w[:, o[7]:o[8]]
    pad = jnp.zeros((w.shape[0], LANES - IDX_HEADS), w.dtype)
    a_q = a_q * (LOG2E * A_QK_DIM ** -0.5)
    return jnp.concatenate([a_q, a_kv, b_q, i_q, gates, b_kv, i_k, i_k, i_w, pad], axis=1).astype(BF16)


def _pad_w_uk(w_uk):
    wt = jnp.transpose(w_uk, (1, 2, 0)) * (LOG2E * B_HEAD_DIM ** -0.5)
    z = jnp.zeros_like(wt)
    even = jnp.concatenate([wt, z], axis=1)
    odd = jnp.concatenate([z, wt], axis=1)
    sel = (jnp.arange(B_HEADS) % 2 == 0)[:, None, None]
    return jnp.where(sel, even, odd).astype(BF16)


def _pair_w_uv(w_uv):
    wv = jnp.transpose(w_uv, (1, 0, 2))
    z = jnp.zeros_like(wv[0::2])
    top = jnp.concatenate([wv[0::2], z], axis=2)
    bot = jnp.concatenate([z, wv[1::2]], axis=2)
    return jnp.transpose(jnp.concatenate([top, bot], axis=1), (0, 2, 1)).astype(BF16)


def kernel(x, c, ada_w, ada_b, w_in, lambda_q1, lambda_k1, lambda_q2, lambda_k2, a_subln_g, kv_norm_g, w_uk, w_uv,
           w_a_proj, w_b_proj, w_o, ln1_g, ln1_b, w_group, b_group, w_expert_router, b_expert_router,
           w_exp_gate, w_exp_up, w_exp_down, ln2_g, ln2_b):
    b, s, d = x.shape
    assert (s, d) == (SEQ, D_MODEL) and ada_w.shape[0] == DEPTH
    slopes_a, slopes_b = _alibi_slopes()
    lane_rep = lambda v: jnp.asarray(np.repeat(v[:, None, None], LANES, axis=2))
    x2 = x.reshape(b * s, d)
    l = 0
    mod3 = _modulation(c, ada_w[l], ada_b[l]).reshape(b, 6, d)

    qkv, bq, iq, gates, small = _projection(x2, mod3, _regroup_w_in(w_in[l]))
    lam_vecs = jnp.stack([lambda_q1[l], lambda_k1[l], lambda_q2[l], lambda_k2[l]]).astype(F32)
    y_a = _diff_attention(qkv.reshape(b, s, W_QKV), lam_vecs, a_subln_g[l].reshape(1, A_V_DIM), lane_rep(slopes_a))
    y_b = _sparse_attention(bq.reshape(b, s, -1), iq.reshape(b, s, -1), small.reshape(b, s, -1),
                            kv_norm_g[l].reshape(1, B_KV_RANK), _pad_w_uk(w_uk[l]), _pair_w_uv(w_uv[l]),
                            lane_rep(slopes_b), _slope_pieces(slopes_b))

    w_router_t = jnp.zeros((ROUTER_ROWS, d), F32).at[:N_GROUPS].set(w_group[l].T).at[
        N_GROUPS:N_GROUPS + N_EXPERTS].set(w_expert_router[l].T)
    b_router = jnp.zeros((ROUTER_ROWS, 1), F32).at[:N_GROUPS, 0].set(b_group[l]).at[
        N_GROUPS:N_GROUPS + N_EXPERTS, 0].set(b_expert_router[l])
    x1, u2, route_t, cnt = _merge(y_a.reshape(b * s, A_WIDTH), y_b.reshape(b * s, B_WIDTH), gates, x2, mod3,
                                  w_a_proj[l].astype(BF16), w_b_proj[l].astype(BF16), w_o[l].astype(BF16),
                                  ln1_g[l].reshape(1, d), ln1_b[l].reshape(1, d), w_router_t, b_router)

    tile_group, n_used, ends, dest3, n_slots = _routing_tables(route_t, cnt)
    wg, wu, wd = w_exp_gate[l].astype(BF16), w_exp_up[l].astype(BF16), w_exp_down[l].astype(BF16)
    y_sorted = _moe(tile_group, n_used, _dispatch(ends, dest3, u2, n_slots), wg, wu, wd)
    out = _final(dest3, y_sorted, x1, mod3, ln2_g[l].reshape(1, d), ln2_b[l].reshape(1, d))
    return out.reshape(b, s, d)
```

```python
import functools
import math

import numpy as np
import jax
import jax.numpy as jnp
from jax import lax
from jax.experimental import pallas as pl
from jax.experimental.pallas import tpu as pltpu

D_MODEL = 1024
SEQ = 2048
CHUNK = 64
A_QK_DIM = 64
A_V_DIM = 128
A_HEADS = 4
A_WIDTH = A_HEADS * A_V_DIM
B_HEAD_DIM = 64
B_HEADS = 8
B_WIDTH = B_HEADS * B_HEAD_DIM
B_KV_RANK = 128
IDX_HEADS = 8
IDX_DIM = 64
TOPK = 256
N_ALIBI_HEADS = A_HEADS + B_HEADS
N_GROUPS = 4
EXPERTS_PER_GROUP = 4
N_EXPERTS = 16
D_FF_EXPERT = 256
LN_EPS = 1e-5
RMS_EPS = 1e-5
DEPTH = 1
DEEPNORM_ALPHA = (2.0 * DEPTH) ** 0.25
LAM_INIT = 0.8 - 0.6 * math.exp(-0.3 * 0)
LOG2E = math.log2(math.e)

LANES = 128
VMEM_LIMIT_BYTES = 56 * 1024 * 1024

F32 = jnp.float32
BF16 = jnp.bfloat16
NEG_INF = float("-inf")

_NT = (((1,), (1,)), ((), ()))


def _dot(a, b, **kw):
    return jnp.dot(a, b, preferred_element_type=F32, **kw)


def _dot_nt(a, b, **kw):
    return lax.dot_general(a, b, _NT, preferred_element_type=F32, **kw)


def _alibi_slopes():
    n = N_ALIBI_HEADS
    slopes = (2.0 ** (-8.0 * np.arange(1, n + 1) / n)).astype(np.float32)
    a_idx = np.arange(A_HEADS) * (n // A_HEADS)
    b_idx = np.setdiff1d(np.arange(n), a_idx)
    return slopes[a_idx], slopes[b_idx]


def _layer_norm(h, g, b):
    mu = jnp.mean(h, axis=-1, keepdims=True)
    d = h - mu
    var = jnp.mean(d * d, axis=-1, keepdims=True)
    return d * lax.rsqrt(var + LN_EPS) * g + b


KEY_SLAB = 64


def _over_keys(op, x):
    part = op(x.reshape(x.shape[0] // KEY_SLAB, KEY_SLAB, x.shape[1]), axis=0)
    return op(part, axis=0, keepdims=True)


def _params(sem):
    return pltpu.CompilerParams(dimension_semantics=sem, vmem_limit_bytes=VMEM_LIMIT_BYTES)


MOD_TN = 1536


def _mod_kernel(c_ref, w_ref, b_ref, o_ref):
    c = c_ref[...]
    cond = c * jax.nn.sigmoid(c)
    o_ref[...] = _dot(cond, w_ref[...], precision=lax.Precision.HIGHEST) + b_ref[...]


def _modulation(c, ada_w, ada_b):
    b, d = c.shape
    n = ada_w.shape[1]
    return pl.pallas_call(
        _mod_kernel,
        out_shape=jax.ShapeDtypeStruct((b, n), F32),
        grid=(n // MOD_TN,),
        in_specs=[
            pl.BlockSpec((b, d), lambda j: (0, 0)),
            pl.BlockSpec((d, MOD_TN), lambda j: (0, j)),
            pl.BlockSpec((1, MOD_TN), lambda j: (0, j)),
        ],
        out_specs=pl.BlockSpec((b, MOD_TN), lambda j: (0, j)),
        compiler_params=_params(("arbitrary",)),
        name="modulation",
    )(c, ada_w, ada_b.reshape(1, n))


PROJ_TM = 512
W_QKV = 3 * A_WIDTH
W_SMALL = 3 * LANES
PROJ_COLS = (W_QKV, B_WIDTH, IDX_HEADS * IDX_DIM, 2 * D_MODEL, W_SMALL)


def _proj_kernel(x_ref, mod_ref, w_ref, qkv_ref, bq_ref, iq_ref, gates_ref, small_ref):
    m = mod_ref[0]
    u = (x_ref[...] * (1.0 + m[1:2]) + m[0:1]).astype(BF16)
    off = 0
    for o_ref, n in zip((qkv_ref, bq_ref, iq_ref, gates_ref, small_ref), PROJ_COLS):
        o_ref[...] = _dot(u, w_ref[:, off:off + n]).astype(o_ref.dtype)
        off += n


def _projection(x2, mod3, w_cat):
    n_tok, d = x2.shape
    tiles_per_batch = SEQ // PROJ_TM
    out_dtypes = (BF16, BF16, BF16, BF16, F32)
    return pl.pallas_call(
        _proj_kernel,
        out_shape=[jax.ShapeDtypeStruct((n_tok, n), dt) for n, dt in zip(PROJ_COLS, out_dtypes)],
        grid=(n_tok // PROJ_TM,),
        in_specs=[
            pl.BlockSpec((PROJ_TM, d), lambda i: (i, 0)),
            pl.BlockSpec((1, 6, d), lambda i: (i // tiles_per_batch, 0, 0)),
            pl.BlockSpec(w_cat.shape, lambda i: (0, 0)),
        ],
        out_specs=[pl.BlockSpec((PROJ_TM, n), lambda i: (i, 0)) for n in PROJ_COLS],
        compiler_params=_params(("arbitrary",)),
        name="projection",
    )(x2, mod3, w_cat)


A_TQ = 256


def _diff_attn_kernel(q_ref, k_ref, v_ref, lam_ref, g_ref, slope_ref, o_ref, bias_ref, vone_ref):
    slope = slope_ref[0][:, 0:1]
    r = lax.broadcasted_iota(jnp.int32, (A_TQ, SEQ), 0)
    j = lax.broadcasted_iota(jnp.int32, (A_TQ, SEQ), 1)
    dist = jnp.abs(r + (SEQ - A_TQ) - j).astype(F32)
    visible = (j - (SEQ - A_TQ)) // CHUNK <= r // CHUNK
    bias_ref[...] = jnp.where(visible, (-LOG2E) * slope * dist, NEG_INF)
    vone_ref[:, 0:A_V_DIM] = v_ref[0]
    vone_ref[:, A_V_DIM:] = jnp.where(lax.broadcasted_iota(jnp.int32, (SEQ, A_V_DIM), 1) == 0, 1.0, 0.0).astype(BF16)

    lv = lam_ref[...]
    lam = (jnp.exp(jnp.sum(lv[0:1] * lv[1:2], axis=1, keepdims=True))
           - jnp.exp(jnp.sum(lv[2:3] * lv[3:4], axis=1, keepdims=True)) + LAM_INIT)
    lane = lax.broadcasted_iota(jnp.int32, (A_TQ, 2 * A_QK_DIM), 1)
    for i in range(SEQ // A_TQ):
        q0 = i * A_TQ
        kv = q0 + A_TQ
        q = q_ref[0, q0:q0 + A_TQ, :]
        k = k_ref[0, 0:kv, :]
        bias = bias_ref[:, SEQ - kv:SEQ]
        ps = []
        for mth in range(2):
            qm = jnp.where((lane // A_QK_DIM) == mth, q, jnp.zeros_like(q))
            s = _dot_nt(qm, k) + bias
            ps.append(jnp.exp2(s - jnp.max(s, axis=-1, keepdims=True)).astype(BF16))
        pv = _dot(jnp.concatenate(ps, axis=0), vone_ref[0:kv, :])
        norm = pv[:, 0:A_V_DIM] * (1.0 / pv[:, A_V_DIM:A_V_DIM + 1])
        o = norm[0:A_TQ] - lam * norm[A_TQ:2 * A_TQ]
        y = o * lax.rsqrt(jnp.mean(o * o, axis=-1, keepdims=True) + RMS_EPS) * g_ref[...]
        o_ref[0, q0:q0 + A_TQ, :] = (y * (1.0 - LAM_INIT)).astype(o_ref.dtype)


def _diff_attention(qkv3, lam_vecs, subln_g, slopes):
    b, s, _ = qkv3.shape
    blk = (1, s, A_V_DIM)
    return pl.pallas_call(
        _diff_attn_kernel,
        out_shape=jax.ShapeDtypeStruct((b, s, A_WIDTH), BF16),
        grid=(b, A_HEADS),
        in_specs=[
            pl.BlockSpec(blk, lambda bi, h: (bi, 0, h)),
            pl.BlockSpec(blk, lambda bi, h: (bi, 0, A_HEADS + h)),
            pl.BlockSpec(blk, lambda bi, h: (bi, 0, 2 * A_HEADS + h)),
            pl.BlockSpec(lam_vecs.shape, lambda bi, h: (0, 0)),
            pl.BlockSpec((1, A_V_DIM), lambda bi, h: (0, 0)),
            pl.BlockSpec((1, 1, LANES), lambda bi, h: (h, 0, 0)),
        ],
        out_specs=pl.BlockSpec(blk, lambda bi, h: (bi, 0, h)),
        scratch_shapes=[pltpu.VMEM((A_TQ, SEQ), F32), pltpu.VMEM((SEQ, 2 * A_V_DIM), BF16)],
        compiler_params=_params(("arbitrary", "arbitrary")),
        name="diff_attention",
    )(qkv3, qkv3, qkv3, lam_vecs, subln_g, slopes)


B_TQ = 128
SMALL_KV = slice(0, 128)
SMALL_IK = slice(128, 256)
SMALL_IW = slice(256, 384)
ONES_ROWS = 16
BISECT_FIRST = 18
BISECT_MORE = 3
BISECT_MAX_ROUNDS = 80


def _key_count(mask_f32):
    return _over_keys(jnp.sum, mask_f32)


def _topk_mask(score_ref, selb_ref, kv):
    s = score_ref[0:kv, :]

    def bisect(_, st):
        lo, hi = st
        probe = 0.5 * lo + 0.5 * hi
        ge = _key_count(jnp.where(score_ref[0:kv, :] >= probe, 1.0, 0.0)) >= TOPK
        return jnp.where(ge, probe, lo), jnp.where(ge, hi, probe)

    def candidate(lo):
        sc = score_ref[0:kv, :]
        t_val = _over_keys(jnp.min, jnp.where(sc >= lo, sc, jnp.inf))
        n_gt = _key_count(jnp.where(sc > t_val, 1.0, 0.0))
        return t_val, n_gt, jnp.sum(jnp.where(n_gt >= TOPK, 1.0, 0.0))

    lo = _over_keys(jnp.min, jnp.where(s == NEG_INF, jnp.inf, s))
    hi = _over_keys(jnp.max, s)
    lo, hi = lax.fori_loop(0, BISECT_FIRST, bisect, (lo, hi))

    def unresolved(st):
        return (st[5] > 0.0) & (st[0] < BISECT_MAX_ROUNDS)

    def refine(st):
        lo, hi = lax.fori_loop(0, BISECT_MORE, bisect, (st[1], st[2]))
        return (st[0] + 1, lo, hi) + candidate(lo)

    _, _, _, t_val, n_gt, _ = lax.while_loop(unresolved, refine, (jnp.int32(0), lo, hi) + candidate(lo))
    gt = s > t_val
    eqf = jnp.where(s == t_val, 1.0, 0.0)
    need = TOPK - n_gt
    selb_ref[0:kv, :] = jnp.where(s >= t_val, 0.0, NEG_INF)
    tie_overflow = jnp.max(jnp.abs(_key_count(eqf) - need))

    @pl.when(tie_overflow > 0.0)
    def _():
        idx = lax.broadcasted_iota(jnp.int32, (kv, B_TQ), 0)

        def idx_step(b, j):
            c = j | lax.shift_left(jnp.int32(1), 10 - b)
            before = _key_count(jnp.where(idx < c, eqf, 0.0))
            return jnp.where(before < need, c, j)

        j_max = lax.fori_loop(0, 11, idx_step, jnp.zeros((1, B_TQ), jnp.int32))
        keep = gt | ((eqf > 0.0) & (idx <= j_max))
        selb_ref[0:kv, :] = jnp.where(keep, 0.0, NEG_INF)


def _sparse_attn_kernel(bq_ref, iq_ref, small_ref, kvg_ref, wuk_ref, wuvt_ref, slope_ref, cparts_ref, o_ref,
                        corr_ref, ckv_ref, ckvt_ref, ik_ref, score_ref, selb_ref):
    @pl.when(pl.program_id(0) == 0)
    def _():
        k_loc = lax.broadcasted_iota(jnp.int32, (B_TQ, B_TQ), 0)
        q_loc = lax.broadcasted_iota(jnp.int32, (B_TQ, B_TQ), 1)
        later = jnp.maximum(k_loc - q_loc, 0).astype(F32)
        for h in range(B_HEADS):
            corr_ref[h] = (-2.0 * LOG2E) * slope_ref[h][:, 0:1] * later

    kv_lat = small_ref[0, :, SMALL_KV]
    ckv = kv_lat * lax.rsqrt(jnp.mean(kv_lat * kv_lat, axis=-1, keepdims=True) + RMS_EPS) * kvg_ref[...]
    pos = lax.broadcasted_iota(jnp.int32, (SEQ, LANES), 0)
    col = lax.broadcasted_iota(jnp.int32, (SEQ, LANES), 1)
    pos_cols = jnp.where(col < 3, pos // CHUNK * CHUNK, jnp.where(col < 6, pos % CHUNK, 0)).astype(F32)
    ckv_ref[:, 0:B_KV_RANK] = ckv.astype(BF16)
    ckv_ref[:, B_KV_RANK:] = pos_cols.astype(BF16)
    ckvt_ref[0:B_KV_RANK, :] = ckv.T.astype(BF16)
    ckvt_ref[B_KV_RANK:, :] = jnp.where(lax.broadcasted_iota(jnp.int32, (ONES_ROWS, SEQ), 0) == 0, 1.0, 0.0).astype(BF16)
    ik_ref[...] = small_ref[0, :, SMALL_IK].astype(BF16)

    lane = lax.broadcasted_iota(jnp.int32, (B_TQ, LANES), 1)
    kk = lax.broadcasted_iota(jnp.int32, (B_TQ, B_TQ), 0)
    qq = lax.broadcasted_iota(jnp.int32, (B_TQ, B_TQ), 1)
    diag_visible = kk // CHUNK <= qq // CHUNK
    n_pairs = B_HEADS // 2

    def half_masked(pair):
        z = jnp.zeros_like(pair)
        return jnp.concatenate([jnp.where(lane < IDX_DIM, pair, z), jnp.where(lane >= IDX_DIM, pair, z)], axis=0)

    for i in range(SEQ // B_TQ):
        q0 = i * B_TQ
        kv = q0 + B_TQ
        rows = slice(q0, q0 + B_TQ)
        if kv <= TOPK:
            if q0:
                selb_ref[0:q0, :] = jnp.zeros((q0, B_TQ), F32)
            selb_ref[q0:kv, :] = jnp.where(diag_visible, 0.0, NEG_INF)
        else:
            iq = iq_ref[0, rows, :]
            iw_t = small_ref[0, rows, SMALL_IW].T
            ik = ik_ref[0:kv, :]
            heads = jnp.concatenate([half_masked(iq[:, jp * LANES:(jp + 1) * LANES]) for jp in range(n_pairs)], axis=0)
            x = _dot_nt(ik, heads)
            score = iw_t[0:1] * jnp.maximum(x[:, 0:B_TQ], 0.0)
            for h in range(1, IDX_HEADS):
                score = score + iw_t[h:h + 1] * jnp.maximum(x[:, h * B_TQ:(h + 1) * B_TQ], 0.0)
            score_ref[0:q0, :] = score[0:q0]
            score_ref[q0:kv, :] = jnp.where(diag_visible, score[q0:kv], NEG_INF)
            _topk_mask(score_ref, selb_ref, kv)

        bq = bq_ref[0, rows, :]
        q_aug = [jnp.concatenate([_dot(bq[:, (h // 2) * LANES:(h // 2 + 1) * LANES], wuk_ref[h]).astype(BF16),
                                  jnp.broadcast_to(cparts_ref[h:h + 1, :], (B_TQ, LANES))], axis=1)
                 for h in range(B_HEADS)]
        s_all = _dot_nt(ckv_ref[0:kv, :], jnp.concatenate(q_aug, axis=0))
        selb = selb_ref[0:kv, :]
        probs = []
        for h in range(B_HEADS):
            sh = s_all[:, h * B_TQ:(h + 1) * B_TQ]
            diag = sh[q0:kv] + corr_ref[h]
            sh = (jnp.concatenate([sh[0:q0], diag], axis=0) if q0 else diag) + selb
            probs.append(jnp.exp2(sh - _over_keys(jnp.max, sh)).astype(BF16))
        lat_t = _dot(ckvt_ref[:, 0:kv], jnp.concatenate(probs, axis=1))
        inv_l = 1.0 / lat_t[B_KV_RANK:B_KV_RANK + 1, :]
        outs = []
        for jp in range(n_pairs):
            pair = jnp.concatenate([lat_t[0:B_KV_RANK, h * B_TQ:(h + 1) * B_TQ] * inv_l[:, h * B_TQ:(h + 1) * B_TQ]
                                    for h in (2 * jp, 2 * jp + 1)], axis=0).astype(BF16)
            outs.append(_dot(wuvt_ref[jp], pair))
        o_ref[0, rows, :] = jnp.concatenate(outs, axis=0).T.astype(o_ref.dtype)


def _slope_pieces(slopes):
    c = jnp.asarray(slopes, F32) * LOG2E
    c1 = c.astype(BF16).astype(F32)
    c2 = (c - c1).astype(BF16).astype(F32)
    c3 = (c - c1 - c2).astype(BF16).astype(F32)
    pieces = jnp.stack([c1, c2, c3, c1, c2, c3], axis=1)
    return jnp.pad(pieces, ((0, 0), (0, LANES - pieces.shape[1]))).astype(BF16)


def _sparse_attention(bq3, iq3, small3, kv_norm_g, wuk_pad, wuvt_pair, slopes, slope_pieces):
    b, s, _ = bq3.shape
    return pl.pallas_call(
        _sparse_attn_kernel,
        out_shape=jax.ShapeDtypeStruct((b, s, B_WIDTH), BF16),
        grid=(b,),
        in_specs=[
            pl.BlockSpec((1, s, B_WIDTH), lambda bi: (bi, 0, 0)),
            pl.BlockSpec((1, s, IDX_HEADS * IDX_DIM), lambda bi: (bi, 0, 0)),
            pl.BlockSpec((1, s, W_SMALL), lambda bi: (bi, 0, 0)),
            pl.BlockSpec((1, B_KV_RANK), lambda bi: (0, 0)),
            pl.BlockSpec(wuk_pad.shape, lambda bi: (0, 0, 0)),
            pl.BlockSpec(wuvt_pair.shape, lambda bi: (0, 0, 0)),
            pl.BlockSpec(slopes.shape, lambda bi: (0, 0, 0)),
            pl.BlockSpec(slope_pieces.shape, lambda bi: (0, 0)),
        ],
        out_specs=pl.BlockSpec((1, s, B_WIDTH), lambda bi: (bi, 0, 0)),
        scratch_shapes=[
            pltpu.VMEM((B_HEADS, B_TQ, B_TQ), F32),
            pltpu.VMEM((SEQ, B_KV_RANK + LANES), BF16),
            pltpu.VMEM((B_KV_RANK + ONES_ROWS, SEQ), BF16),
            pltpu.VMEM((SEQ, LANES), BF16),
            pltpu.VMEM((SEQ, B_TQ), F32),
            pltpu.VMEM((SEQ, B_TQ), F32),
        ],
        compiler_params=_params(("arbitrary",)),
        name="sparse_attention",
    )(bq3, iq3, small3, kv_norm_g, wuk_pad, wuvt_pair, slopes, slope_pieces)


MERGE_TM = 512
ROUTER_ROWS = 32
ROUTE_ROWS = 8
ROW_W = D_MODEL + LANES


def _first_max_onehot(rows):
    mx = rows[0]
    for r in rows[1:]:
        mx = jnp.maximum(mx, r)
    taken = jnp.zeros_like(mx)
    hot = []
    for r in rows:
        h = jnp.where((r == mx) & (taken == 0.0), 1.0, 0.0)
        taken = taken + h
        hot.append(h)
    return hot, mx


def _softmax_rows(rows):
    mx = rows[0]
    for r in rows[1:]:
        mx = jnp.maximum(mx, r)
    e = [jnp.exp(r - mx) for r in rows]
    tot = e[0]
    for r in e[1:]:
        tot = tot + r
    return [r / tot for r in e]


def _merge_kernel(ya_ref, yb_ref, gates_ref, x_ref, mod_ref, wa_ref, wb_ref, wo_ref, g1_ref, b1_ref,
                  wr_ref, br_ref, tri_ref, x1_ref, u2_ref, route_ref, cnt_ref, run_ref):
    m = mod_ref[0]
    pa = _dot(ya_ref[...], wa_ref[...])
    pb = _dot(yb_ref[...], wb_ref[...])
    gt = jax.nn.sigmoid(gates_ref[...].astype(F32))
    mixed = gt[:, 0:D_MODEL] * pa + gt[:, D_MODEL:2 * D_MODEL] * pb
    z = _dot(mixed.astype(BF16), wo_ref[...])
    x1 = _layer_norm(DEEPNORM_ALPHA * x_ref[...] + m[2:3] * z, g1_ref[...], b1_ref[...])
    x1_ref[...] = x1
    u2 = x1 * (1.0 + m[4:5]) + m[3:4]
    u2_ref[:, 0:D_MODEL] = u2

    logits = _dot_nt(wr_ref[...], u2, precision=lax.Precision.HIGHEST) + br_ref[...]
    g_prob = _softmax_rows([logits[k:k + 1] for k in range(N_GROUPS)])
    g_hot, g_top = _first_max_onehot(g_prob)
    e_logit = []
    for jx in range(EXPERTS_PER_GROUP):
        acc = jnp.zeros_like(g_top)
        for g in range(N_GROUPS):
            row = N_GROUPS + g * EXPERTS_PER_GROUP + jx
            acc = acc + logits[row:row + 1] * g_hot[g]
        e_logit.append(acc)
    e_prob = _softmax_rows(e_logit)
    hot1, p1 = _first_max_onehot(e_prob)
    rest = [jnp.where(h > 0.0, NEG_INF, p) for h, p in zip(hot1, e_prob)]
    hot2, p2 = _first_max_onehot(rest)
    tot = p1 + p2
    w1 = g_top * (p1 / tot)
    w2 = g_top * (p2 / tot)

    @pl.when(pl.program_id(0) == 0)
    def _():
        run_ref[...] = jnp.zeros_like(run_ref)

    gid = g_hot[1] + 2.0 * g_hot[2] + 3.0 * g_hot[3]
    grp = lax.broadcasted_iota(jnp.int32, (ROUTE_ROWS, MERGE_TM), 0).astype(F32)
    hot8 = jnp.where(grp == gid, 1.0, 0.0)
    before = _dot(hot8.astype(BF16), tri_ref[...])
    rank = jnp.sum(hot8 * (run_ref[:, 0:1] + before), axis=0, keepdims=True)
    record = jnp.concatenate([gid] + [w1 * hot1[jx] + w2 * hot2[jx] for jx in range(EXPERTS_PER_GROUP)]
                             + [rank, jnp.zeros((LANES - 2 - EXPERTS_PER_GROUP, MERGE_TM), F32)], axis=0)
    route_ref[...] = record[0:ROUTE_ROWS]
    u2_ref[:, D_MODEL:D_MODEL + LANES] = record.T
    run_ref[...] = run_ref[...] + jnp.sum(hot8, axis=1, keepdims=True)
    cnt_ref[...] = run_ref[...]


def _merge(ya, yb, gates, x2, mod3, wa, wb, wo, ln_g, ln_b, w_router_t, b_router):
    n_tok, d = x2.shape
    tiles_per_batch = SEQ // MERGE_TM
    tok = lambda n: pl.BlockSpec((MERGE_TM, n), lambda i: (i, 0))
    full = lambda a: pl.BlockSpec(a.shape, lambda i: (0,) * a.ndim)
    tri = jnp.asarray(np.triu(np.ones((MERGE_TM, MERGE_TM), np.float32), k=1), BF16)
    return pl.pallas_call(
        _merge_kernel,
        out_shape=[
            jax.ShapeDtypeStruct((n_tok, d), F32),
            jax.ShapeDtypeStruct((n_tok, ROW_W), F32),
            jax.ShapeDtypeStruct((ROUTE_ROWS, n_tok), F32),
            jax.ShapeDtypeStruct((ROUTE_ROWS, LANES), F32),
        ],
        grid=(n_tok // MERGE_TM,),
        in_specs=[
            tok(A_WIDTH), tok(B_WIDTH), tok(2 * d), tok(d),
            pl.BlockSpec((1, 6, d), lambda i: (i // tiles_per_batch, 0, 0)),
            full(wa), full(wb), full(wo), full(ln_g), full(ln_b), full(w_router_t), full(b_router), full(tri),
        ],
        out_specs=[tok(d), tok(ROW_W), pl.BlockSpec((ROUTE_ROWS, MERGE_TM), lambda i: (0, i)),
                   pl.BlockSpec((ROUTE_ROWS, LANES), lambda i: (0, 0))],
        scratch_shapes=[pltpu.VMEM((ROUTE_ROWS, LANES), F32)],
        compiler_params=_params(("arbitrary",)),
        name="merge_router",
    )(ya, yb, gates, x2, mod3, wa, wb, wo, ln_g, ln_b, w_router_t, b_router, tri)


def _start_row_gather(src_hbm, idx_ref, buf, slot, sem, n_rows):
    def issue(r, carry):
        pltpu.make_async_copy(src_hbm.at[pl.ds(idx_ref[0, 0, r], 1)], buf.at[slot, pl.ds(r, 1)], sem.at[slot]).start()
        return carry

    lax.fori_loop(0, n_rows, issue, 0, unroll=8)


def _wait_row_gather(buf, slot, sem):
    pltpu.make_async_copy(buf.at[slot], buf.at[slot], sem.at[slot]).wait()


MOE_TM = 512


def _dispatch_kernel(ends_ref, idx_ref, rows_ref, sorted_hbm, zero_buf, stage, sem, zsem):
    i = pl.program_id(0)
    slot = i % 2

    @pl.when(i == 0)
    def _():
        zero_buf[...] = jnp.zeros_like(zero_buf)
        for g in range(N_GROUPS):
            start = pl.multiple_of(ends_ref[g] - MOE_TM, MOE_TM)
            nonempty = ends_ref[g] > (ends_ref[g - 1] if g else 0)

            @pl.when(nonempty)
            def _():
                cp = pltpu.make_async_copy(zero_buf, sorted_hbm.at[pl.ds(start, MOE_TM)], zsem)
                cp.start()
                cp.wait()

        for k in range(N_GROUPS):
            spare = sorted_hbm.shape[0] - (k + 1) * MOE_TM

            @pl.when(spare >= ends_ref[N_GROUPS - 1])
            def _():
                cp = pltpu.make_async_copy(zero_buf, sorted_hbm.at[pl.ds(spare, MOE_TM)], zsem)
                cp.start()
                cp.wait()

    stage[slot] = rows_ref[...]

    def issue(r, carry):
        pltpu.make_async_copy(stage.at[slot, pl.ds(r, 1)], sorted_hbm.at[pl.ds(idx_ref[0, 0, r], 1)], sem.at[slot]).start()
        return carry

    lax.fori_loop(0, MOE_TM, issue, 0, unroll=8)

    def wait_slot(s):
        pltpu.make_async_copy(stage.at[s], sorted_hbm.at[pl.ds(0, MOE_TM)], sem.at[s]).wait()

    @pl.when(i > 0)
    def _():
        wait_slot(1 - slot)

    @pl.when(i == pl.num_programs(0) - 1)
    def _():
        wait_slot(slot)


def _dispatch(ends, dest3, rows, n_slots):
    n_steps = dest3.shape[0]
    w = rows.shape[1]
    return pl.pallas_call(
        _dispatch_kernel,
        out_shape=jax.ShapeDtypeStruct((n_slots, w), rows.dtype),
        grid_spec=pltpu.PrefetchScalarGridSpec(
            num_scalar_prefetch=1,
            grid=(n_steps,),
            in_specs=[
                pl.BlockSpec((1, 1, MOE_TM), lambda i, ends: (i, 0, 0), memory_space=pltpu.SMEM),
                pl.BlockSpec((MOE_TM, w), lambda i, ends: (i, 0)),
            ],
            out_specs=pl.BlockSpec(memory_space=pl.ANY),
            scratch_shapes=[pltpu.VMEM((MOE_TM, w), rows.dtype), pltpu.VMEM((2, MOE_TM, w), rows.dtype),
                            pltpu.SemaphoreType.DMA((2,)), pltpu.SemaphoreType.DMA(())],
        ),
        compiler_params=_params(("arbitrary",)),
        name="dispatch",
    )(ends, dest3, rows)


def _moe_kernel(tg_ref, nu_ref, x_ref, wg_ref, wu_ref, wd_ref, y_ref):
    j = pl.program_id(0)
    n_used = nu_ref[0]

    @pl.when(j < n_used)
    def _():
        u = x_ref[:, 0:D_MODEL].astype(BF16)
        comb = x_ref[:, D_MODEL + 1:D_MODEL + 1 + EXPERTS_PER_GROUP]
        y = None
        for jx in range(EXPERTS_PER_GROUP):
            h = jax.nn.silu(_dot(u, wg_ref[jx])) * _dot(u, wu_ref[jx])
            yj = _dot((h * comb[:, jx:jx + 1]).astype(BF16), wd_ref[jx])
            y = yj if y is None else y + yj
        y_ref[...] = y

    @pl.when(j >= n_used)
    def _():
        y_ref[...] = jnp.zeros_like(y_ref)


def _moe(tile_group, n_used, rows_sorted, wg, wu, wd):
    n_tiles = rows_sorted.shape[0] // MOE_TM
    group_of = lambda w: pl.BlockSpec((EXPERTS_PER_GROUP,) + w.shape[1:], lambda j, tg, nu: (tg[j], 0, 0))
    return pl.pallas_call(
        _moe_kernel,
        out_shape=jax.ShapeDtypeStruct((n_tiles * MOE_TM, D_MODEL), F32),
        grid_spec=pltpu.PrefetchScalarGridSpec(
            num_scalar_prefetch=2,
            grid=(n_tiles,),
            in_specs=[
                pl.BlockSpec((MOE_TM, ROW_W), lambda j, tg, nu: (jnp.minimum(j, jnp.maximum(nu[0] - 1, 0)), 0)),
                group_of(wg), group_of(wu), group_of(wd),
            ],
            out_specs=pl.BlockSpec((MOE_TM, D_MODEL), lambda j, tg, nu: (j, 0)),
        ),
        compiler_params=_params(("arbitrary",)),
        name="moe",
    )(tile_group, n_used, rows_sorted, wg, wu, wd)


FINAL_TM = 512


def _final_kernel(idx_ref, idx_next_ref, y_hbm, x1_ref, mod_ref, g2_ref, b2_ref, o_ref, ybuf, sem):
    i = pl.program_id(0)
    slot = i % 2

    @pl.when(i == 0)
    def _():
        _start_row_gather(y_hbm, idx_ref, ybuf, 0, sem, FINAL_TM)

    @pl.when(i + 1 < pl.num_programs(0))
    def _():
        _start_row_gather(y_hbm, idx_next_ref, ybuf, 1 - slot, sem, FINAL_TM)

    _wait_row_gather(ybuf, slot, sem)
    m = mod_ref[0]
    o_ref[...] = _layer_norm(DEEPNORM_ALPHA * x1_ref[...] + m[5:6] * ybuf[slot], g2_ref[...], b2_ref[...])


def _final(dest3, y_sorted, x1, mod3, ln_g, ln_b):
    n_tok, d = x1.shape
    n_tiles = n_tok // FINAL_TM
    tiles_per_batch = SEQ // FINAL_TM
    idx_spec = lambda f: pl.BlockSpec((1, 1, FINAL_TM), f, memory_space=pltpu.SMEM)
    vec = pl.BlockSpec((1, d), lambda i: (0, 0))
    return pl.pallas_call(
        _final_kernel,
        out_shape=jax.ShapeDtypeStruct((n_tok, d), F32),
        grid=(n_tiles,),
        in_specs=[
            idx_spec(lambda i: (i, 0, 0)),
            idx_spec(lambda i: (jnp.minimum(i + 1, n_tiles - 1), 0, 0)),
            pl.BlockSpec(memory_space=pl.ANY),
            pl.BlockSpec((FINAL_TM, d), lambda i: (i, 0)),
            pl.BlockSpec((1, 6, d), lambda i: (i // tiles_per_batch, 0, 0)),
            vec, vec,
        ],
        out_specs=pl.BlockSpec((FINAL_TM, d), lambda i: (i, 0)),
        scratch_shapes=[pltpu.VMEM((2, FINAL_TM, d), F32), pltpu.SemaphoreType.DMA((2,))],
        compiler_params=_params(("arbitrary",)),
        name="combine_norm",
    )(dest3, dest3, y_sorted, x1, mod3, ln_g, ln_b)


def _routing_tables(route_t, cnt):
    n_tok = route_t.shape[1]
    n_tiles = n_tok // MOE_TM + N_GROUPS
    counts = cnt[:N_GROUPS, 0].astype(jnp.int32)
    padded = (counts + MOE_TM - 1) // MOE_TM * MOE_TM
    ends = jnp.cumsum(padded)
    gid = route_t[0].astype(jnp.int32)
    dest = (ends - padded)[gid] + route_t[5].astype(jnp.int32)
    tile_start = jnp.arange(n_tiles, dtype=jnp.int32) * MOE_TM
    tile_group = jnp.minimum(jnp.sum((tile_start[:, None] >= ends[None, :]).astype(jnp.int32), axis=1), N_GROUPS - 1)
    n_used = (ends[-1:] // MOE_TM).astype(jnp.int32)
    return tile_group, n_used, ends.astype(jnp.int32), dest.reshape(-1, 1, MOE_TM), n_tiles * MOE_TM


def _regroup_w_in(w):
    o = np.cumsum((512, 512, 512, 512, 128, 512, 64, 8, 2048))
    a_q, a_kv, b_q, b_kv = w[:, :o[0]], w[:, o[0]:o[2]], w[:, o[2]:o[3]], w[:, o[3]:o[4]]
    i_q, i_k, i_w, gates = w[:, o[4]:o[5]], w[:, o[5]:o[6]], w[:, o[6]:o[7]], w[:, o[7]:o[8]]
    pad = jnp.zeros((w.shape[0], LANES - IDX_HEADS), w.dtype)
    a_q = a_q * (LOG2E * A_QK_DIM ** -0.5)
    return jnp.concatenate([a_q, a_kv, b_q, i_q, gates, b_kv, i_k, i_k, i_w, pad], axis=1).astype(BF16)


def _pad_w_uk(w_uk):
    wt = jnp.transpose(w_uk, (1, 2, 0)) * (LOG2E * B_HEAD_DIM ** -0.5)
    z = jnp.zeros_like(wt)
    even = jnp.concatenate([wt, z], axis=1)
    odd = jnp.concatenate([z, wt], axis=1)
    sel = (jnp.arange(B_HEADS) % 2 == 0)[:, None, None]
    return jnp.where(sel, even, odd).astype(BF16)


def _pair_w_uv(w_uv):
    wv = jnp.transpose(w_uv, (1, 0, 2))
    z = jnp.zeros_like(wv[0::2])
    top = jnp.concatenate([wv[0::2], z], axis=2)
    bot = jnp.concatenate([z, wv[1::2]], axis=2)
    return jnp.transpose(jnp.concatenate([top, bot], axis=1), (0, 2, 1)).astype(BF16)


def kernel(x, c, ada_w, ada_b, w_in, lambda_q1, lambda_k1, lambda_q2, lambda_k2, a_subln_g, kv_norm_g, w_uk, w_uv,
           w_a_proj, w_b_proj, w_o, ln1_g, ln1_b, w_group, b_group, w_expert_router, b_expert_router,
           w_exp_gate, w_exp_up, w_exp_down, ln2_g, ln2_b):
    b, s, d = x.shape
    assert (s, d) == (SEQ, D_MODEL) and ada_w.shape[0] == DEPTH
    slopes_a, slopes_b = _alibi_slopes()
    lane_rep = lambda v: jnp.asarray(np.repeat(v[:, None, None], LANES, axis=2))
    x2 = x.reshape(b * s, d)
    l = 0
    mod3 = _modulation(c, ada_w[l], ada_b[l]).reshape(b, 6, d)

    qkv, bq, iq, gates, small = _projection(x2, mod3, _regroup_w_in(w_in[l]))
    lam_vecs = jnp.stack([lambda_q1[l], lambda_k1[l], lambda_q2[l], lambda_k2[l]]).astype(F32)
    y_a = _diff_attention(qkv.reshape(b, s, W_QKV), lam_vecs, a_subln_g[l].reshape(1, A_V_DIM), lane_rep(slopes_a))
    y_b = _sparse_attention(bq.reshape(b, s, -1), iq.reshape(b, s, -1), small.reshape(b, s, -1),
                            kv_norm_g[l].reshape(1, B_KV_RANK), _pad_w_uk(w_uk[l]), _pair_w_uv(w_uv[l]),
                            lane_rep(slopes_b), _slope_pieces(slopes_b))

    w_router_t = jnp.zeros((ROUTER_ROWS, d), F32).at[:N_GROUPS].set(w_group[l].T).at[
        N_GROUPS:N_GROUPS + N_EXPERTS].set(w_expert_router[l].T)
    b_router = jnp.zeros((ROUTER_ROWS, 1), F32).at[:N_GROUPS, 0].set(b_group[l]).at[
        N_GROUPS:N_GROUPS + N_EXPERTS, 0].set(b_expert_router[l])
    x1, u2, route_t, cnt = _merge(y_a.reshape(b * s, A_WIDTH), y_b.reshape(b * s, B_WIDTH), gates, x2, mod3,
                                  w_a_proj[l].astype(BF16), w_b_proj[l].astype(BF16), w_o[l].astype(BF16),
                                  ln1_g[l].reshape(1, d), ln1_b[l].reshape(1, d), w_router_t, b_router)

    tile_group, n_used, ends, dest3, n_slots = _routing_tables(route_t, cnt)
    wg, wu, wd = w_exp_gate[l].astype(BF16), w_exp_up[l].astype(BF16), w_exp_down[l].astype(BF16)
    y_sorted = _moe(tile_group, n_used, _dispatch(ends, dest3, u2, n_slots), wg, wu, wd)
    out = _final(dest3, y_sorted, x1, mod3, ln2_g[l].reshape(1, d), ln2_b[l].reshape(1, d))
    return out.reshape(b, s, d)
```

```python
import functools
import math

import numpy as np
import jax
import jax.numpy as jnp
from jax import lax
from jax.experimental import pallas as pl
from jax.experimental.pallas import tpu as pltpu

D_MODEL = 1024
SEQ = 2048
CHUNK = 64
A_QK_DIM = 64
A_V_DIM = 128
A_HEADS = 4
A_WIDTH = A_HEADS * A_V_DIM
B_HEAD_DIM = 64
B_HEADS = 8
B_WIDTH = B_HEADS * B_HEAD_DIM
B_KV_RANK = 128
IDX_HEADS = 8
IDX_DIM = 64
TOPK = 256
N_ALIBI_HEADS = A_HEADS + B_HEADS
N_GROUPS = 4
EXPERTS_PER_GROUP = 4
N_EXPERTS = 16
D_FF_EXPERT = 256
LN_EPS = 1e-5
RMS_EPS = 1e-5
DEPTH = 1
DEEPNORM_ALPHA = (2.0 * DEPTH) ** 0.25
LAM_INIT = 0.8 - 0.6 * math.exp(-0.3 * 0)
LOG2E = math.log2(math.e)

LANES = 128
VMEM_LIMIT_BYTES = 56 * 1024 * 1024

F32 = jnp.float32
BF16 = jnp.bfloat16
NEG_INF = float("-inf")

_NT = (((1,), (1,)), ((), ()))


def _dot(a, b, **kw):
    return jnp.dot(a, b, preferred_element_type=F32, **kw)


def _dot_nt(a, b, **kw):
    return lax.dot_general(a, b, _NT, preferred_element_type=F32, **kw)


def _alibi_slopes():
    n = N_ALIBI_HEADS
    slopes = (2.0 ** (-8.0 * np.arange(1, n + 1) / n)).astype(np.float32)
    a_idx = np.arange(A_HEADS) * (n // A_HEADS)
    b_idx = np.setdiff1d(np.arange(n), a_idx)
    return slopes[a_idx], slopes[b_idx]


def _layer_norm(h, g, b):
    mu = jnp.mean(h, axis=-1, keepdims=True)
    d = h - mu
    var = jnp.mean(d * d, axis=-1, keepdims=True)
    return d * lax.rsqrt(var + LN_EPS) * g + b


KEY_SLAB = 64


def _over_keys(op, x):
    part = op(x.reshape(x.shape[0] // KEY_SLAB, KEY_SLAB, x.shape[1]), axis=0)
    return op(part, axis=0, keepdims=True)


def _params(sem):
    return pltpu.CompilerParams(dimension_semantics=sem, vmem_limit_bytes=VMEM_LIMIT_BYTES)


MOD_TN = 1536


def _mod_kernel(c_ref, w_ref, b_ref, o_ref):
    c = c_ref[...]
    cond = c * jax.nn.sigmoid(c)
    o_ref[...] = _dot(cond, w_ref[...], precision=lax.Precision.HIGHEST) + b_ref[...]


def _modulation(c, ada_w, ada_b):
    b, d = c.shape
    n = ada_w.shape[1]
    return pl.pallas_call(
        _mod_kernel,
        out_shape=jax.ShapeDtypeStruct((b, n), F32),
        grid=(n // MOD_TN,),
        in_specs=[
            pl.BlockSpec((b, d), lambda j: (0, 0)),
            pl.BlockSpec((d, MOD_TN), lambda j: (0, j)),
            pl.BlockSpec((1, MOD_TN), lambda j: (0, j)),
        ],
        out_specs=pl.BlockSpec((b, MOD_TN), lambda j: (0, j)),
        compiler_params=_params(("arbitrary",)),
        name="modulation",
    )(c, ada_w, ada_b.reshape(1, n))


PROJ_TM = 512
W_QKV = 3 * A_WIDTH
W_SMALL = 3 * LANES
PROJ_COLS = (W_QKV, B_WIDTH, IDX_HEADS * IDX_DIM, 2 * D_MODEL, W_SMALL)


def _proj_kernel(x_ref, mod_ref, w_ref, qkv_ref, bq_ref, iq_ref, gates_ref, small_ref):
    m = mod_ref[0]
    u = (x_ref[...] * (1.0 + m[1:2]) + m[0:1]).astype(BF16)
    off = 0
    for o_ref, n in zip((qkv_ref, bq_ref, iq_ref, gates_ref, small_ref), PROJ_COLS):
        o_ref[...] = _dot(u, w_ref[:, off:off + n]).astype(o_ref.dtype)
        off += n


def _projection(x2, mod3, w_cat):
    n_tok, d = x2.shape
    tiles_per_batch = SEQ // PROJ_TM
    out_dtypes = (BF16, BF16, BF16, BF16, F32)
    return pl.pallas_call(
        _proj_kernel,
        out_shape=[jax.ShapeDtypeStruct((n_tok, n), dt) for n, dt in zip(PROJ_COLS, out_dtypes)],
        grid=(n_tok // PROJ_TM,),
        in_specs=[
            pl.BlockSpec((PROJ_TM, d), lambda i: (i, 0)),
            pl.BlockSpec((1, 6, d), lambda i: (i // tiles_per_batch, 0, 0)),
            pl.BlockSpec(w_cat.shape, lambda i: (0, 0)),
        ],
        out_specs=[pl.BlockSpec((PROJ_TM, n), lambda i: (i, 0)) for n in PROJ_COLS],
        compiler_params=_params(("arbitrary",)),
        name="projection",
    )(x2, mod3, w_cat)


A_TQ = 256


def _diff_attn_kernel(q_ref, k_ref, v_ref, lam_ref, g_ref, slope_ref, cparts_ref, o_ref, diag_ref, kpos_ref, vone_ref):
    slope = slope_ref[0][:, 0:1]
    r = lax.broadcasted_iota(jnp.int32, (A_TQ, A_TQ), 0)
    j = lax.broadcasted_iota(jnp.int32, (A_TQ, A_TQ), 1)
    later = jnp.maximum(j - r, 0).astype(F32)
    diag_ref[...] = jnp.where(j // CHUNK <= r // CHUNK, (-2.0 * LOG2E) * slope * later, NEG_INF)
    pos = lax.broadcasted_iota(jnp.int32, (SEQ, LANES), 0)
    col = lax.broadcasted_iota(jnp.int32, (SEQ, LANES), 1)
    pos_cols = jnp.where(col < 3, pos // CHUNK * CHUNK, jnp.where(col < 6, pos % CHUNK, 0)).astype(F32)
    kpos_ref[:, 0:2 * A_QK_DIM] = k_ref[0]
    kpos_ref[:, 2 * A_QK_DIM:] = pos_cols.astype(BF16)
    cparts = jnp.broadcast_to(cparts_ref[0], (A_TQ, LANES))
    vone_ref[:, 0:A_V_DIM] = v_ref[0]
    vone_ref[:, A_V_DIM:] = jnp.where(lax.broadcasted_iota(jnp.int32, (SEQ, A_V_DIM), 1) == 0, 1.0, 0.0).astype(BF16)

    lv = lam_ref[...]
    lam = (jnp.exp(jnp.sum(lv[0:1] * lv[1:2], axis=1, keepdims=True))
           - jnp.exp(jnp.sum(lv[2:3] * lv[3:4], axis=1, keepdims=True)) + LAM_INIT)
    lane = lax.broadcasted_iota(jnp.int32, (A_TQ, 2 * A_QK_DIM), 1)
    for i in range(SEQ // A_TQ):
        q0 = i * A_TQ
        kv = q0 + A_TQ
        q = q_ref[0, q0:q0 + A_TQ, :]
        k = kpos_ref[0:kv, :]
        ps = []
        for mth in range(2):
            qm = jnp.where((lane // A_QK_DIM) == mth, q, jnp.zeros_like(q))
            s = _dot_nt(jnp.concatenate([qm, cparts], axis=1), k)
            last = s[:, q0:kv] + diag_ref[...]
            s = jnp.concatenate([s[:, 0:q0], last], axis=1) if q0 else last
            ps.append(jnp.exp2(s - jnp.max(s, axis=-1, keepdims=True)).astype(BF16))
        pv = _dot(jnp.concatenate(ps, axis=0), vone_ref[0:kv, :])
        norm = pv[:, 0:A_V_DIM] * (1.0 / pv[:, A_V_DIM:A_V_DIM + 1])
        o = norm[0:A_TQ] - lam * norm[A_TQ:2 * A_TQ]
        y = o * lax.rsqrt(jnp.mean(o * o, axis=-1, keepdims=True) + RMS_EPS) * g_ref[...]
        o_ref[0, q0:q0 + A_TQ, :] = (y * (1.0 - LAM_INIT)).astype(o_ref.dtype)


def _diff_attention(qkv3, lam_vecs, subln_g, slopes, slope_pieces):
    b, s, _ = qkv3.shape
    blk = (1, s, A_V_DIM)
    return pl.pallas_call(
        _diff_attn_kernel,
        out_shape=jax.ShapeDtypeStruct((b, s, A_WIDTH), BF16),
        grid=(b, A_HEADS),
        in_specs=[
            pl.BlockSpec(blk, lambda bi, h: (bi, 0, h)),
            pl.BlockSpec(blk, lambda bi, h: (bi, 0, A_HEADS + h)),
            pl.BlockSpec(blk, lambda bi, h: (bi, 0, 2 * A_HEADS + h)),
            pl.BlockSpec(lam_vecs.shape, lambda bi, h: (0, 0)),
            pl.BlockSpec((1, A_V_DIM), lambda bi, h: (0, 0)),
            pl.BlockSpec((1, 1, LANES), lambda bi, h: (h, 0, 0)),
            pl.BlockSpec((1, 1, LANES), lambda bi, h: (h, 0, 0)),
        ],
        out_specs=pl.BlockSpec(blk, lambda bi, h: (bi, 0, h)),
        scratch_shapes=[pltpu.VMEM((A_TQ, A_TQ), F32),
                        pltpu.VMEM((SEQ, 2 * A_QK_DIM + LANES), BF16),
                        pltpu.VMEM((SEQ, 2 * A_V_DIM), BF16)],
        compiler_params=_params(("arbitrary", "arbitrary")),
        name="diff_attention",
    )(qkv3, qkv3, qkv3, lam_vecs, subln_g, slopes, slope_pieces)


B_TQ = 128
SMALL_KV = slice(0, 128)
SMALL_IK = slice(128, 256)
SMALL_IW = slice(256, 384)
ONES_ROWS = 16
BISECT_FIRST = 18
BISECT_MORE = 3
BISECT_MAX_ROUNDS = 80


def _key_count(mask_f32):
    return _over_keys(jnp.sum, mask_f32)


def _topk_mask(score_ref, selb_ref, kv):
    s = score_ref[0:kv, :]

    def bisect(_, st):
        lo, hi = st
        probe = 0.5 * lo + 0.5 * hi
        ge = _key_count(jnp.where(score_ref[0:kv, :] >= probe, 1.0, 0.0)) >= TOPK
        return jnp.where(ge, probe, lo), jnp.where(ge, hi, probe)

    def candidate(lo):
        sc = score_ref[0:kv, :]
        t_val = _over_keys(jnp.min, jnp.where(sc >= lo, sc, jnp.inf))
        n_gt = _key_count(jnp.where(sc > t_val, 1.0, 0.0))
        return t_val, n_gt, jnp.sum(jnp.where(n_gt >= TOPK, 1.0, 0.0))

    lo = _over_keys(jnp.min, jnp.where(s == NEG_INF, jnp.inf, s))
    hi = _over_keys(jnp.max, s)
    lo, hi = lax.fori_loop(0, BISECT_FIRST, bisect, (lo, hi))

    def unresolved(st):
        return (st[5] > 0.0) & (st[0] < BISECT_MAX_ROUNDS)

    def refine(st):
        lo, hi = lax.fori_loop(0, BISECT_MORE, bisect, (st[1], st[2]))
        return (st[0] + 1, lo, hi) + candidate(lo)

    _, _, _, t_val, n_gt, _ = lax.while_loop(unresolved, refine, (jnp.int32(0), lo, hi) + candidate(lo))
    gt = s > t_val
    eqf = jnp.where(s == t_val, 1.0, 0.0)
    need = TOPK - n_gt
    selb_ref[0:kv, :] = jnp.where(s >= t_val, 0.0, NEG_INF)
    tie_overflow = jnp.max(jnp.abs(_key_count(eqf) - need))

    @pl.when(tie_overflow > 0.0)
    def _():
        idx = lax.broadcasted_iota(jnp.int32, (kv, B_TQ), 0)

        def idx_step(b, j):
            c = j | lax.shift_left(jnp.int32(1), 10 - b)
            before = _key_count(jnp.where(idx < c, eqf, 0.0))
            return jnp.where(before < need, c, j)

        j_max = lax.fori_loop(0, 11, idx_step, jnp.zeros((1, B_TQ), jnp.int32))
        keep = gt | ((eqf > 0.0) & (idx <= j_max))
        selb_ref[0:kv, :] = jnp.where(keep, 0.0, NEG_INF)


def _sparse_attn_kernel(bq_ref, iq_ref, small_ref, kvg_ref, wuk_ref, wuvt_ref, slope_ref, cparts_ref, o_ref,
                        corr_ref, ckv_ref, ckvt_ref, ik_ref, score_ref, selb_ref):
    @pl.when(pl.program_id(0) == 0)
    def _():
        k_loc = lax.broadcasted_iota(jnp.int32, (B_TQ, B_TQ), 0)
        q_loc = lax.broadcasted_iota(jnp.int32, (B_TQ, B_TQ), 1)
        later = jnp.maximum(k_loc - q_loc, 0).astype(F32)
        for h in range(B_HEADS):
            corr_ref[h] = (-2.0 * LOG2E) * slope_ref[h][:, 0:1] * later

    kv_lat = small_ref[0, :, SMALL_KV]
    ckv = kv_lat * lax.rsqrt(jnp.mean(kv_lat * kv_lat, axis=-1, keepdims=True) + RMS_EPS) * kvg_ref[...]
    pos = lax.broadcasted_iota(jnp.int32, (SEQ, LANES), 0)
    col = lax.broadcasted_iota(jnp.int32, (SEQ, LANES), 1)
    pos_cols = jnp.where(col < 3, pos // CHUNK * CHUNK, jnp.where(col < 6, pos % CHUNK, 0)).astype(F32)
    ckv_ref[:, 0:B_KV_RANK] = ckv.astype(BF16)
    ckv_ref[:, B_KV_RANK:] = pos_cols.astype(BF16)
    ckvt_ref[0:B_KV_RANK, :] = ckv.T.astype(BF16)
    ckvt_ref[B_KV_RANK:, :] = jnp.where(lax.broadcasted_iota(jnp.int32, (ONES_ROWS, SEQ), 0) == 0, 1.0, 0.0).astype(BF16)
    ik_ref[...] = small_ref[0, :, SMALL_IK].astype(BF16)

    lane = lax.broadcasted_iota(jnp.int32, (B_TQ, LANES), 1)
    kk = lax.broadcasted_iota(jnp.int32, (B_TQ, B_TQ), 0)
    qq = lax.broadcasted_iota(jnp.int32, (B_TQ, B_TQ), 1)
    diag_visible = kk // CHUNK <= qq // CHUNK
    n_pairs = B_HEADS // 2

    def half_masked(pair):
        z = jnp.zeros_like(pair)
        return jnp.concatenate([jnp.where(lane < IDX_DIM, pair, z), jnp.where(lane >= IDX_DIM, pair, z)], axis=0)

    for i in range(SEQ // B_TQ):
        q0 = i * B_TQ
        kv = q0 + B_TQ
        rows = slice(q0, q0 + B_TQ)
        if kv <= TOPK:
            if q0:
                selb_ref[0:q0, :] = jnp.zeros((q0, B_TQ), F32)
            selb_ref[q0:kv, :] = jnp.where(diag_visible, 0.0, NEG_INF)
        else:
            iq = iq_ref[0, rows, :]
            iw_t = small_ref[0, rows, SMALL_IW].T
            ik = ik_ref[0:kv, :]
            heads = jnp.concatenate([half_masked(iq[:, jp * LANES:(jp + 1) * LANES]) for jp in range(n_pairs)], axis=0)
            x = _dot_nt(ik, heads)
            score = iw_t[0:1] * jnp.maximum(x[:, 0:B_TQ], 0.0)
            for h in range(1, IDX_HEADS):
                score = score + iw_t[h:h + 1] * jnp.maximum(x[:, h * B_TQ:(h + 1) * B_TQ], 0.0)
            score_ref[0:q0, :] = score[0:q0]
            score_ref[q0:kv, :] = jnp.where(diag_visible, score[q0:kv], NEG_INF)
            _topk_mask(score_ref, selb_ref, kv)

        bq = bq_ref[0, rows, :]
        q_aug = [jnp.concatenate([_dot(bq[:, (h // 2) * LANES:(h // 2 + 1) * LANES], wuk_ref[h]).astype(BF16),
                                  jnp.broadcast_to(cparts_ref[h:h + 1, :], (B_TQ, LANES))], axis=1)
                 for h in range(B_HEADS)]
        s_all = _dot_nt(ckv_ref[0:kv, :], jnp.concatenate(q_aug, axis=0))
        selb = selb_ref[0:kv, :]
        probs = []
        for h in range(B_HEADS):
            sh = s_all[:, h * B_TQ:(h + 1) * B_TQ]
            diag = sh[q0:kv] + corr_ref[h]
            sh = (jnp.concatenate([sh[0:q0], diag], axis=0) if q0 else diag) + selb
            probs.append(jnp.exp2(sh - _over_keys(jnp.max, sh)).astype(BF16))
        lat_t = _dot(ckvt_ref[:, 0:kv], jnp.concatenate(probs, axis=1))
        inv_l = 1.0 / lat_t[B_KV_RANK:B_KV_RANK + 1, :]
        outs = []
        for jp in range(n_pairs):
            pair = jnp.concatenate([lat_t[0:B_KV_RANK, h * B_TQ:(h + 1) * B_TQ] * inv_l[:, h * B_TQ:(h + 1) * B_TQ]
                                    for h in (2 * jp, 2 * jp + 1)], axis=0).astype(BF16)
            outs.append(_dot(wuvt_ref[jp], pair))
        o_ref[0, rows, :] = jnp.concatenate(outs, axis=0).T.astype(o_ref.dtype)


def _slope_pieces(slopes):
    c = jnp.asarray(slopes, F32) * LOG2E
    c1 = c.astype(BF16).astype(F32)
    c2 = (c - c1).astype(BF16).astype(F32)
    c3 = (c - c1 - c2).astype(BF16).astype(F32)
    pieces = jnp.stack([c1, c2, c3, c1, c2, c3], axis=1)
    return jnp.pad(pieces, ((0, 0), (0, LANES - pieces.shape[1]))).astype(BF16)


def _sparse_attention(bq3, iq3, small3, kv_norm_g, wuk_pad, wuvt_pair, slopes, slope_pieces):
    b, s, _ = bq3.shape
    return pl.pallas_call(
        _sparse_attn_kernel,
        out_shape=jax.ShapeDtypeStruct((b, s, B_WIDTH), BF16),
        grid=(b,),
        in_specs=[
            pl.BlockSpec((1, s, B_WIDTH), lambda bi: (bi, 0, 0)),
            pl.BlockSpec((1, s, IDX_HEADS * IDX_DIM), lambda bi: (bi, 0, 0)),
            pl.BlockSpec((1, s, W_SMALL), lambda bi: (bi, 0, 0)),
            pl.BlockSpec((1, B_KV_RANK), lambda bi: (0, 0)),
            pl.BlockSpec(wuk_pad.shape, lambda bi: (0, 0, 0)),
            pl.BlockSpec(wuvt_pair.shape, lambda bi: (0, 0, 0)),
            pl.BlockSpec(slopes.shape, lambda bi: (0, 0, 0)),
            pl.BlockSpec(slope_pieces.shape, lambda bi: (0, 0)),
        ],
        out_specs=pl.BlockSpec((1, s, B_WIDTH), lambda bi: (bi, 0, 0)),
        scratch_shapes=[
            pltpu.VMEM((B_HEADS, B_TQ, B_TQ), F32),
            pltpu.VMEM((SEQ, B_KV_RANK + LANES), BF16),
            pltpu.VMEM((B_KV_RANK + ONES_ROWS, SEQ), BF16),
            pltpu.VMEM((SEQ, LANES), BF16),
            pltpu.VMEM((SEQ, B_TQ), F32),
            pltpu.VMEM((SEQ, B_TQ), F32),
        ],
        compiler_params=_params(("arbitrary",)),
        name="sparse_attention",
    )(bq3, iq3, small3, kv_norm_g, wuk_pad, wuvt_pair, slopes, slope_pieces)


MERGE_TM = 512
ROUTER_ROWS = 32
ROUTE_ROWS = 8
ROW_W = D_MODEL + LANES


def _first_max_onehot(rows):
    mx = rows[0]
    for r in rows[1:]:
        mx = jnp.maximum(mx, r)
    taken = jnp.zeros_like(mx)
    hot = []
    for r in rows:
        h = jnp.where((r == mx) & (taken == 0.0), 1.0, 0.0)
        taken = taken + h
        hot.append(h)
    return hot, mx


def _softmax_rows(rows):
    mx = rows[0]
    for r in rows[1:]:
        mx = jnp.maximum(mx, r)
    e = [jnp.exp(r - mx) for r in rows]
    tot = e[0]
    for r in e[1:]:
        tot = tot + r
    return [r / tot for r in e]


def _merge_kernel(ya_ref, yb_ref, gates_ref, x_ref, mod_ref, wa_ref, wb_ref, wo_ref, g1_ref, b1_ref,
                  wr_ref, br_ref, tri_ref, x1_ref, u2_ref, route_ref, cnt_ref, run_ref):
    m = mod_ref[0]
    pa = _dot(ya_ref[...], wa_ref[...])
    pb = _dot(yb_ref[...], wb_ref[...])
    gt = jax.nn.sigmoid(gates_ref[...].astype(F32))
    mixed = gt[:, 0:D_MODEL] * pa + gt[:, D_MODEL:2 * D_MODEL] * pb
    z = _dot(mixed.astype(BF16), wo_ref[...])
    x1 = _layer_norm(DEEPNORM_ALPHA * x_ref[...] + m[2:3] * z, g1_ref[...], b1_ref[...])
    x1_ref[...] = x1
    u2 = x1 * (1.0 + m[4:5]) + m[3:4]
    u2_ref[:, 0:D_MODEL] = u2

    logits = _dot_nt(wr_ref[...], u2, precision=lax.Precision.HIGHEST) + br_ref[...]
    g_prob = _softmax_rows([logits[k:k + 1] for k in range(N_GROUPS)])
    g_hot, g_top = _first_max_onehot(g_prob)
    e_logit = []
    for jx in range(EXPERTS_PER_GROUP):
        acc = jnp.zeros_like(g_top)
        for g in range(N_GROUPS):
            row = N_GROUPS + g * EXPERTS_PER_GROUP + jx
            acc = acc + logits[row:row + 1] * g_hot[g]
        e_logit.append(acc)
    e_prob = _softmax_rows(e_logit)
    hot1, p1 = _first_max_onehot(e_prob)
    rest = [jnp.where(h > 0.0, NEG_INF, p) for h, p in zip(hot1, e_prob)]
    hot2, p2 = _first_max_onehot(rest)
    tot = p1 + p2
    w1 = g_top * (p1 / tot)
    w2 = g_top * (p2 / tot)

    @pl.when(pl.program_id(0) == 0)
    def _():
        run_ref[...] = jnp.zeros_like(run_ref)

    gid = g_hot[1] + 2.0 * g_hot[2] + 3.0 * g_hot[3]
    grp = lax.broadcasted_iota(jnp.int32, (ROUTE_ROWS, MERGE_TM), 0).astype(F32)
    hot8 = jnp.where(grp == gid, 1.0, 0.0)
    before = _dot(hot8.astype(BF16), tri_ref[...])
    rank = jnp.sum(hot8 * (run_ref[:, 0:1] + before), axis=0, keepdims=True)
    record = jnp.concatenate([gid] + [w1 * hot1[jx] + w2 * hot2[jx] for jx in range(EXPERTS_PER_GROUP)]
                             + [rank, jnp.zeros((LANES - 2 - EXPERTS_PER_GROUP, MERGE_TM), F32)], axis=0)
    route_ref[...] = record[0:ROUTE_ROWS]
    u2_ref[:, D_MODEL:D_MODEL + LANES] = record.T
    run_ref[...] = run_ref[...] + jnp.sum(hot8, axis=1, keepdims=True)
    cnt_ref[...] = run_ref[...]


def _merge(ya, yb, gates, x2, mod3, wa, wb, wo, ln_g, ln_b, w_router_t, b_router):
    n_tok, d = x2.shape
    tiles_per_batch = SEQ // MERGE_TM
    tok = lambda n: pl.BlockSpec((MERGE_TM, n), lambda i: (i, 0))
    full = lambda a: pl.BlockSpec(a.shape, lambda i: (0,) * a.ndim)
    tri = jnp.asarray(np.triu(np.ones((MERGE_TM, MERGE_TM), np.float32), k=1), BF16)
    return pl.pallas_call(
        _merge_kernel,
        out_shape=[
            jax.ShapeDtypeStruct((n_tok, d), F32),
            jax.ShapeDtypeStruct((n_tok, ROW_W), F32),
            jax.ShapeDtypeStruct((ROUTE_ROWS, n_tok), F32),
            jax.ShapeDtypeStruct((ROUTE_ROWS, LANES), F32),
        ],
        grid=(n_tok // MERGE_TM,),
        in_specs=[
            tok(A_WIDTH), tok(B_WIDTH), tok(2 * d), tok(d),
            pl.BlockSpec((1, 6, d), lambda i: (i // tiles_per_batch, 0, 0)),
            full(wa), full(wb), full(wo), full(ln_g), full(ln_b), full(w_router_t), full(b_router), full(tri),
        ],
        out_specs=[tok(d), tok(ROW_W), pl.BlockSpec((ROUTE_ROWS, MERGE_TM), lambda i: (0, i)),
                   pl.BlockSpec((ROUTE_ROWS, LANES), lambda i: (0, 0))],
        scratch_shapes=[pltpu.VMEM((ROUTE_ROWS, LANES), F32)],
        compiler_params=_params(("arbitrary",)),
        name="merge_router",
    )(ya, yb, gates, x2, mod3, wa, wb, wo, ln_g, ln_b, w_router_t, b_router, tri)


def _start_row_gather(src_hbm, idx_ref, buf, slot, sem, n_rows):
    def issue(r, carry):
        pltpu.make_async_copy(src_hbm.at[pl.ds(idx_ref[0, 0, r], 1)], buf.at[slot, pl.ds(r, 1)], sem.at[slot]).start()
        return carry

    lax.fori_loop(0, n_rows, issue, 0, unroll=8)


def _wait_row_gather(buf, slot, sem):
    pltpu.make_async_copy(buf.at[slot], buf.at[slot], sem.at[slot]).wait()


MOE_TM = 512


def _dispatch_kernel(ends_ref, idx_ref, rows_ref, sorted_hbm, zero_buf, stage, sem, zsem):
    i = pl.program_id(0)
    slot = i % 2

    @pl.when(i == 0)
    def _():
        zero_buf[...] = jnp.zeros_like(zero_buf)
        for g in range(N_GROUPS):
            start = pl.multiple_of(ends_ref[g] - MOE_TM, MOE_TM)
            nonempty = ends_ref[g] > (ends_ref[g - 1] if g else 0)

            @pl.when(nonempty)
            def _():
                cp = pltpu.make_async_copy(zero_buf, sorted_hbm.at[pl.ds(start, MOE_TM)], zsem)
                cp.start()
                cp.wait()

        for k in range(N_GROUPS):
            spare = sorted_hbm.shape[0] - (k + 1) * MOE_TM

            @pl.when(spare >= ends_ref[N_GROUPS - 1])
            def _():
                cp = pltpu.make_async_copy(zero_buf, sorted_hbm.at[pl.ds(spare, MOE_TM)], zsem)
                cp.start()
                cp.wait()

    stage[slot] = rows_ref[...]

    def issue(r, carry):
        pltpu.make_async_copy(stage.at[slot, pl.ds(r, 1)], sorted_hbm.at[pl.ds(idx_ref[0, 0, r], 1)], sem.at[slot]).start()
        return carry

    lax.fori_loop(0, MOE_TM, issue, 0, unroll=8)

    def wait_slot(s):
        pltpu.make_async_copy(stage.at[s], sorted_hbm.at[pl.ds(0, MOE_TM)], sem.at[s]).wait()

    @pl.when(i > 0)
    def _():
        wait_slot(1 - slot)

    @pl.when(i == pl.num_programs(0) - 1)
    def _():
        wait_slot(slot)


def _dispatch(ends, dest3, rows, n_slots):
    n_steps = dest3.shape[0]
    w = rows.shape[1]
    return pl.pallas_call(
        _dispatch_kernel,
        out_shape=jax.ShapeDtypeStruct((n_slots, w), rows.dtype),
        grid_spec=pltpu.PrefetchScalarGridSpec(
            num_scalar_prefetch=1,
            grid=(n_steps,),
            in_specs=[
                pl.BlockSpec((1, 1, MOE_TM), lambda i, ends: (i, 0, 0), memory_space=pltpu.SMEM),
                pl.BlockSpec((MOE_TM, w), lambda i, ends: (i, 0)),
            ],
            out_specs=pl.BlockSpec(memory_space=pl.ANY),
            scratch_shapes=[pltpu.VMEM((MOE_TM, w), rows.dtype), pltpu.VMEM((2, MOE_TM, w), rows.dtype),
                            pltpu.SemaphoreType.DMA((2,)), pltpu.SemaphoreType.DMA(())],
        ),
        compiler_params=_params(("arbitrary",)),
        name="dispatch",
    )(ends, dest3, rows)


def _moe_kernel(tg_ref, nu_ref, x_ref, wg_ref, wu_ref, wd_ref, y_ref):
    j = pl.program_id(0)
    n_used = nu_ref[0]

    @pl.when(j < n_used)
    def _():
        u = x_ref[:, 0:D_MODEL].astype(BF16)
        comb = x_ref[:, D_MODEL + 1:D_MODEL + 1 + EXPERTS_PER_GROUP]
        y = None
        for jx in range(EXPERTS_PER_GROUP):
            h = jax.nn.silu(_dot(u, wg_ref[jx])) * _dot(u, wu_ref[jx])
            yj = _dot((h * comb[:, jx:jx + 1]).astype(BF16), wd_ref[jx])
            y = yj if y is None else y + yj
        y_ref[...] = y

    @pl.when(j >= n_used)
    def _():
        y_ref[...] = jnp.zeros_like(y_ref)


def _moe(tile_group, n_used, rows_sorted, wg, wu, wd):
    n_tiles = rows_sorted.shape[0] // MOE_TM
    group_of = lambda w: pl.BlockSpec((EXPERTS_PER_GROUP,) + w.shape[1:], lambda j, tg, nu: (tg[j], 0, 0))
    return pl.pallas_call(
        _moe_kernel,
        out_shape=jax.ShapeDtypeStruct((n_tiles * MOE_TM, D_MODEL), F32),
        grid_spec=pltpu.PrefetchScalarGridSpec(
            num_scalar_prefetch=2,
            grid=(n_tiles,),
            in_specs=[
                pl.BlockSpec((MOE_TM, ROW_W), lambda j, tg, nu: (jnp.minimum(j, jnp.maximum(nu[0] - 1, 0)), 0)),
                group_of(wg), group_of(wu), group_of(wd),
            ],
            out_specs=pl.BlockSpec((MOE_TM, D_MODEL), lambda j, tg, nu: (j, 0)),
        ),
        compiler_params=_params(("arbitrary",)),
        name="moe",
    )(tile_group, n_used, rows_sorted, wg, wu, wd)


FINAL_TM = 512


def _final_kernel(idx_ref, idx_next_ref, y_hbm, x1_ref, mod_ref, g2_ref, b2_ref, o_ref, ybuf, sem):
    i = pl.program_id(0)
    slot = i % 2

    @pl.when(i == 0)
    def _():
        _start_row_gather(y_hbm, idx_ref, ybuf, 0, sem, FINAL_TM)

    @pl.when(i + 1 < pl.num_programs(0))
    def _():
        _start_row_gather(y_hbm, idx_next_ref, ybuf, 1 - slot, sem, FINAL_TM)

    _wait_row_gather(ybuf, slot, sem)
    m = mod_ref[0]
    o_ref[...] = _layer_norm(DEEPNORM_ALPHA * x1_ref[...] + m[5:6] * ybuf[slot], g2_ref[...], b2_ref[...])


def _final(dest3, y_sorted, x1, mod3, ln_g, ln_b):
    n_tok, d = x1.shape
    n_tiles = n_tok // FINAL_TM
    tiles_per_batch = SEQ // FINAL_TM
    idx_spec = lambda f: pl.BlockSpec((1, 1, FINAL_TM), f, memory_space=pltpu.SMEM)
    vec = pl.BlockSpec((1, d), lambda i: (0, 0))
    return pl.pallas_call(
        _final_kernel,
        out_shape=jax.ShapeDtypeStruct((n_tok, d), F32),
        grid=(n_tiles,),
        in_specs=[
            idx_spec(lambda i: (i, 0, 0)),
            idx_spec(lambda i: (jnp.minimum(i + 1, n_tiles - 1), 0, 0)),
            pl.BlockSpec(memory_space=pl.ANY),
            pl.BlockSpec((FINAL_TM, d), lambda i: (i, 0)),
            pl.BlockSpec((1, 6, d), lambda i: (i // tiles_per_batch, 0, 0)),
            vec, vec,
        ],
        out_specs=pl.BlockSpec((FINAL_TM, d), lambda i: (i, 0)),
        scratch_shapes=[pltpu.VMEM((2, FINAL_TM, d), F32), pltpu.SemaphoreType.DMA((2,))],
        compiler_params=_params(("arbitrary",)),
        name="combine_norm",
    )(dest3, dest3, y_sorted, x1, mod3, ln_g, ln_b)


def _routing_tables(route_t, cnt):
    n_tok = route_t.shape[1]
    n_tiles = n_tok // MOE_TM + N_GROUPS
    counts = cnt[:N_GROUPS, 0].astype(jnp.int32)
    padded = (counts + MOE_TM - 1) // MOE_TM * MOE_TM
    ends = jnp.cumsum(padded)
    gid = route_t[0].astype(jnp.int32)
    dest = (ends - padded)[gid] + route_t[5].astype(jnp.int32)
    tile_start = jnp.arange(n_tiles, dtype=jnp.int32) * MOE_TM
    tile_group = jnp.minimum(jnp.sum((tile_start[:, None] >= ends[None, :]).astype(jnp.int32), axis=1), N_GROUPS - 1)
    n_used = (ends[-1:] // MOE_TM).astype(jnp.int32)
    return tile_group, n_used, ends.astype(jnp.int32), dest.reshape(-1, 1, MOE_TM), n_tiles * MOE_TM


def _regroup_w_in(w):
    o = np.cumsum((512, 512, 512, 512, 128, 512, 64, 8, 2048))
    a_q, a_kv, b_q, b_kv = w[:, :o[0]], w[:, o[0]:o[2]], w[:, o[2]:o[3]], w[:, o[3]:o[4]]
    i_q, i_k, i_w, gates = w[:, o[4]:o[5]], w[:, o[5]:o[6]], w[:, o[6]:o[7]], w[:, o[7]:o[8]]
    pad = jnp.zeros((w.shape[0], LANES - IDX_HEADS), w.dtype)
    a_q = a_q * (LOG2E * A_QK_DIM ** -0.5)
    return jnp.concatenate([a_q, a_kv, b_q, i_q, gates, b_kv, i_k, i_k, i_w, pad], axis=1).astype(BF16)


def _pad_w_uk(w_uk):
    wt = jnp.transpose(w_uk, (1, 2, 0)) * (LOG2E * B_HEAD_DIM ** -0.5)
    z = jnp.zeros_like(wt)
    even = jnp.concatenate([wt, z], axis=1)
    odd = jnp.concatenate([z, wt], axis=1)
    sel = (jnp.arange(B_HEADS) % 2 == 0)[:, None, None]
    return jnp.where(sel, even, odd).astype(BF16)


def _pair_w_uv(w_uv):
    wv = jnp.transpose(w_uv, (1, 0, 2))
    z = jnp.zeros_like(wv[0::2])
    top = jnp.concatenate([wv[0::2], z], axis=2)
    bot = jnp.concatenate([z, wv[1::2]], axis=2)
    return jnp.transpose(jnp.concatenate([top, bot], axis=1), (0, 2, 1)).astype(BF16)


def kernel(x, c, ada_w, ada_b, w_in, lambda_q1, lambda_k1, lambda_q2, lambda_k2, a_subln_g, kv_norm_g, w_uk, w_uv,
           w_a_proj, w_b_proj, w_o, ln1_g, ln1_b, w_group, b_group, w_expert_router, b_expert_router,
           w_exp_gate, w_exp_up, w_exp_down, ln2_g, ln2_b):
    b, s, d = x.shape
    assert (s, d) == (SEQ, D_MODEL) and ada_w.shape[0] == DEPTH
    slopes_a, slopes_b = _alibi_slopes()
    lane_rep = lambda v: jnp.asarray(np.repeat(v[:, None, None], LANES, axis=2))
    x2 = x.reshape(b * s, d)
    l = 0
    mod3 = _modulation(c, ada_w[l], ada_b[l]).reshape(b, 6, d)

    qkv, bq, iq, gates, small = _projection(x2, mod3, _regroup_w_in(w_in[l]))
    lam_vecs = jnp.stack([lambda_q1[l], lambda_k1[l], lambda_q2[l], lambda_k2[l]]).astype(F32)
    y_a = _diff_attention(qkv.reshape(b, s, W_QKV), lam_vecs, a_subln_g[l].reshape(1, A_V_DIM), lane_rep(slopes_a),
                          _slope_pieces(slopes_a).reshape(A_HEADS, 1, LANES))
    y_b = _sparse_attention(bq.reshape(b, s, -1), iq.reshape(b, s, -1), small.reshape(b, s, -1),
                            kv_norm_g[l].reshape(1, B_KV_RANK), _pad_w_uk(w_uk[l]), _pair_w_uv(w_uv[l]),
                            lane_rep(slopes_b), _slope_pieces(slopes_b))

    w_router_t = jnp.zeros((ROUTER_ROWS, d), F32).at[:N_GROUPS].set(w_group[l].T).at[
        N_GROUPS:N_GROUPS + N_EXPERTS].set(w_expert_router[l].T)
    b_router = jnp.zeros((ROUTER_ROWS, 1), F32).at[:N_GROUPS, 0].set(b_group[l]).at[
        N_GROUPS:N_GROUPS + N_EXPERTS, 0].set(b_expert_router[l])
    x1, u2, route_t, cnt = _merge(y_a.reshape(b * s, A_WIDTH), y_b.reshape(b * s, B_WIDTH), gates, x2, mod3,
                                  w_a_proj[l].astype(BF16), w_b_proj[l].astype(BF16), w_o[l].astype(BF16),
                                  ln1_g[l].reshape(1, d), ln1_b[l].reshape(1, d), w_router_t, b_router)

    tile_group, n_used, ends, dest3, n_slots = _routing_tables(route_t, cnt)
    wg, wu, wd = w_exp_gate[l].astype(BF16), w_exp_up[l].astype(BF16), w_exp_down[l].astype(BF16)
    y_sorted = _moe(tile_group, n_used, _dispatch(ends, dest3, u2, n_slots), wg, wu, wd)
    out = _final(dest3, y_sorted, x1, mod3, ln2_g[l].reshape(1, d), ln2_b[l].reshape(1, d))
    return out.reshape(b, s, d)
```
